```python
import jax, jax.numpy as jnp
from jax import lax
import numpy as np

D_MODEL = 1024
BATCH = 8
SEQ = 2048
DEPTH = 2
DEC_BATCH = 128
DEC_SEQ = 8
PAST_LEN = 16384
PAGE_SIZE = 128

N_META = 16
N_MIXERS = 2
SC_WIDTH = 3
CF_WIDTH = 31
D_SC = D_MODEL
D_CF = D_MODEL
N_GROUPS = 4
EXPERTS_PER_GROUP = 8
N_EXPERTS = N_GROUPS * EXPERTS_PER_GROUP
TOP_K = 2
D_EXPERT = D_MODEL // 4
ALPHA = float((2 * DEPTH) ** 0.25)
BETA = float((8 * DEPTH) ** -0.25)
N_A = (DEPTH + 1) // 2
N_B = DEPTH // 2
LN_EPS = 1e-5

kernel_name = "hybrid_shortconv_conformer_hmoe_step"


def layer_norm(x, g, b):
    xf = x.astype(jnp.float32)
    mu = xf.mean(-1, keepdims=True)
    var = jnp.square(xf - mu).mean(-1, keepdims=True)
    y = (xf - mu) * lax.rsqrt(var + LN_EPS)
    return (y * g.astype(jnp.float32) + b.astype(jnp.float32)).astype(x.dtype)


def causal_dwconv(full, w):
    return lax.conv_general_dilated(
        full, w[:, None, :].astype(full.dtype), window_strides=(1,), padding="VALID",
        dimension_numbers=("NWC", "WIO", "NWC"), feature_group_count=full.shape[-1])


def short_conv_mixer(h, hist, w_in, conv_w, w_out):
    bcv = jnp.einsum("btd,de->bte", h, w_in)
    b_gate, c_gate, v = jnp.split(bcv, 3, axis=-1)
    full = jnp.concatenate([hist.astype(h.dtype), c_gate * v], axis=1)
    conv = causal_dwconv(full, conv_w)
    y = jnp.einsum("btc,cd->btd", b_gate * conv, w_out)
    return y, full[:, -(SC_WIDTH - 1):]


def conformer_conv_mixer(h, hist, w1, b1, conv_w, conv_b, ln_g, ln_b, w2, b2):
    ag = jnp.einsum("btd,de->bte", h, w1) + b1
    a, g = jnp.split(ag, 2, axis=-1)
    full = jnp.concatenate([hist.astype(h.dtype), a * jax.nn.sigmoid(g)], axis=1)
    c = causal_dwconv(full, conv_w) + conv_b
    c = jax.nn.silu(layer_norm(c, ln_g, ln_b))
    y = jnp.einsum("btc,cd->btd", c, w2) + b2
    return y, full[:, -(CF_WIDTH - 1):]


def hierarchical_moe(h, rg_w, rg_b, re_w, re_b, w_gate, w_up, w_down):
    bsz, t, d = h.shape
    x = h.reshape(-1, d)
    glog = (x @ rg_w + rg_b).astype(jnp.float32)
    gprob = jax.nn.softmax(glog, axis=-1)
    gp, gidx = lax.top_k(gprob, 1)
    elog = (x @ re_w + re_b).astype(jnp.float32).reshape(-1, N_GROUPS, EXPERTS_PER_GROUP)
    sel = jnp.take_along_axis(elog, gidx[:, :, None], axis=1)[:, 0]
    eprob = jax.nn.softmax(sel, axis=-1)
    ew, eidx = lax.top_k(eprob, TOP_K)
    ew = ew / ew.sum(-1, keepdims=True)
    in_group = jnp.einsum("nk,nke->ne", ew, jax.nn.one_hot(eidx, EXPERTS_PER_GROUP, dtype=jnp.float32))
    grp = gp * jax.nn.one_hot(gidx[:, 0], N_GROUPS, dtype=jnp.float32)
    gate = (grp[:, :, None] * in_group[:, None, :]).reshape(-1, N_EXPERTS).astype(h.dtype)
    hid = jax.nn.silu(jnp.einsum("nd,edf->nef", x, w_gate)) * jnp.einsum("nd,edf->nef", x, w_up)
    out = jnp.einsum("nef,efd->nd", hid * gate[:, :, None], w_down)
    return out.reshape(bsz, t, d)


def run_trunk(h, st_a, st_b, sc_w_in, sc_conv_w, sc_w_out, cf_w_pw1, cf_b_pw1, cf_conv_w,
              cf_conv_b, cf_ln_g, cf_ln_b, cf_w_pw2, cf_b_pw2, ln1_g, ln1_b, ln2_g, ln2_b,
              rt_group_w, rt_group_b, rt_expert_w, rt_expert_b, moe_w_gate, moe_w_up, moe_w_down):
    new_a, new_b = [], []
    ia, ib = 0, 0
    for i in range(DEPTH):
        if i % N_MIXERS == 0:
            m, s = short_conv_mixer(h, st_a[ia], sc_w_in[ia], sc_conv_w[ia], sc_w_out[ia])
            new_a.append(s)
            ia += 1
        else:
            m, s = conformer_conv_mixer(h, st_b[ib], cf_w_pw1[ib], cf_b_pw1[ib], cf_conv_w[ib],
                                        cf_conv_b[ib], cf_ln_g[ib], cf_ln_b[ib], cf_w_pw2[ib], cf_b_pw2[ib])
            new_b.append(s)
            ib += 1
        h = layer_norm(ALPHA * h + m, ln1_g[i], ln1_b[i])
        f = hierarchical_moe(h, rt_group_w[i], rt_group_b[i], rt_expert_w[i], rt_expert_b[i],
                             moe_w_gate[i], moe_w_up[i], moe_w_down[i])
        h = layer_norm(ALPHA * h + f, ln2_g[i], ln2_b[i])
    return h, jnp.stack(new_a), jnp.stack(new_b)


def setup_inputs(seed: int = 0) -> dict:
    key = jax.random.key(seed)
    ks = jax.random.split(key, 32)
    f32 = jnp.float32
    nrm = lambda k, shape, s: jax.random.normal(k, shape, f32) * s
    d, F = D_MODEL, D_EXPERT
    return {
        "x_prompt": nrm(ks[0], (BATCH, SEQ, d), 1.0),
        "x_sample": nrm(ks[1], (DEC_BATCH, DEC_SEQ, d), 1.0),
        "state_conv_a": nrm(ks[2], (N_A, DEC_BATCH, SC_WIDTH - 1, D_SC), 1.0),
        "state_conv_b": nrm(ks[3], (N_B, DEC_BATCH, CF_WIDTH - 1, D_CF), 1.0),
        "meta_tokens": nrm(ks[4], (N_META, d), 1.0),
        "sc_w_in": nrm(ks[5], (N_A, d, 3 * D_SC), d ** -0.5),
        "sc_conv_w": nrm(ks[6], (N_A, SC_WIDTH, D_SC), SC_WIDTH ** -0.5),
        "sc_w_out": nrm(ks[7], (N_A, D_SC, d), BETA * D_SC ** -0.5),
        "cf_w_pw1": nrm(ks[8], (N_B, d, 2 * D_CF), d ** -0.5),
        "cf_b_pw1": nrm(ks[9], (N_B, 2 * D_CF), 0.02),
        "cf_conv_w": nrm(ks[10], (N_B, CF_WIDTH, D_CF), CF_WIDTH ** -0.5),
        "cf_conv_b": nrm(ks[11], (N_B, D_CF), 0.02),
        "cf_ln_g": 1.0 + nrm(ks[12], (N_B, D_CF), 0.02),
        "cf_ln_b": nrm(ks[13], (N_B, D_CF), 0.02),
        "cf_w_pw2": nrm(ks[14], (N_B, D_CF, d), BETA * D_CF ** -0.5),
        "cf_b_pw2": nrm(ks[15], (N_B, d), 0.02),
        "ln1_g": 1.0 + nrm(ks[16], (DEPTH, d), 0.02),
        "ln1_b": nrm(ks[17], (DEPTH, d), 0.02),
        "ln2_g": 1.0 + nrm(ks[18], (DEPTH, d), 0.02),
        "ln2_b": nrm(ks[19], (DEPTH, d), 0.02),
        "rt_group_w": nrm(ks[20], (DEPTH, d, N_GROUPS), d ** -0.5),
        "rt_group_b": nrm(ks[21], (DEPTH, N_GROUPS), 0.01),
        "rt_expert_w": nrm(ks[22], (DEPTH, d, N_EXPERTS), d ** -0.5),
        "rt_expert_b": nrm(ks[23], (DEPTH, N_EXPERTS), 0.01),
        "moe_w_gate": nrm(ks[24], (DEPTH, N_EXPERTS, d, F), d ** -0.5),
        "moe_w_up": nrm(ks[25], (DEPTH, N_EXPERTS, d, F), d ** -0.5),
        "moe_w_down": nrm(ks[26], (DEPTH, N_EXPERTS, F, d), BETA * F ** -0.5),
    }


def reference(x_prompt, x_sample, state_conv_a, state_conv_b, meta_tokens, sc_w_in, sc_conv_w,
              sc_w_out, cf_w_pw1, cf_b_pw1, cf_conv_w, cf_conv_b, cf_ln_g, cf_ln_b, cf_w_pw2,
              cf_b_pw2, ln1_g, ln1_b, ln2_g, ln2_b, rt_group_w, rt_group_b, rt_expert_w,
              rt_expert_b, moe_w_gate, moe_w_up, moe_w_down):
    weights = (sc_w_in, sc_conv_w, sc_w_out, cf_w_pw1, cf_b_pw1, cf_conv_w, cf_conv_b, cf_ln_g,
               cf_ln_b, cf_w_pw2, cf_b_pw2, ln1_g, ln1_b, ln2_g, ln2_b, rt_group_w, rt_group_b,
               rt_expert_w, rt_expert_b, moe_w_gate, moe_w_up, moe_w_down)
    bsz = x_prompt.shape[0]
    meta = jnp.broadcast_to(meta_tokens.astype(x_prompt.dtype)[None], (bsz, N_META, D_MODEL))
    hp = jnp.concatenate([meta, x_prompt], axis=1)
    zero_a = jnp.zeros((N_A, bsz, SC_WIDTH - 1, D_SC), x_prompt.dtype)
    zero_b = jnp.zeros((N_B, bsz, CF_WIDTH - 1, D_CF), x_prompt.dtype)
    hp, new_a_p, new_b_p = run_trunk(hp, zero_a, zero_b, *weights)
    y_prompt = hp[:, N_META:]
    hs, new_a_s, new_b_s = run_trunk(x_sample, state_conv_a, state_conv_b, *weights)
    return (y_prompt, hs, new_a_p, new_b_p, new_a_s, new_b_s)
```

```python
import functools

import jax
import jax.numpy as jnp
from jax import lax
from jax.experimental import pallas as pl
from jax.experimental.pallas import tpu as pltpu

F32 = jnp.float32
BF16 = jnp.bfloat16

LN_EPS = 1e-5
N_GROUPS = 4
EXPERTS_PER_GROUP = 8
N_EXPERTS = N_GROUPS * EXPERTS_PER_GROUP
SC_WIDTH = 3
CF_WIDTH = 31
ROUTER_ROWS = 8 + N_EXPERTS
SUBLANES = 8
CONV_ROWS = 16
VMEM_LIMIT = 56 * 1024 * 1024


def _dot(a, b):
    return jnp.dot(a, b, preferred_element_type=F32)


def _layer_norm(z, g, b):
    mu = jnp.mean(z, axis=-1, keepdims=True)
    zc = z - mu
    var = jnp.mean(zc * zc, axis=-1, keepdims=True)
    return zc * lax.rsqrt(var + LN_EPS) * g + b


def _sigmoid(x):
    return 1.0 / (1.0 + jnp.exp(-x))


def _route(h1b, rw_ref, rb_ref):
    t = h1b.shape[0]
    logits = lax.dot_general(rw_ref[...], h1b, (((1,), (1,)), ((), ())),
                             preferred_element_type=F32) + rb_ref[...]
    g = [logits[i:i + 1, :] for i in range(N_GROUPS)]
    gmax = jnp.maximum(jnp.maximum(g[0], g[1]), jnp.maximum(g[2], g[3]))
    gidx = jnp.where(g[0] == gmax, 0, jnp.where(g[1] == gmax, 1, jnp.where(g[2] == gmax, 2, 3)))
    gsum = (jnp.exp(g[0] - gmax) + jnp.exp(g[1] - gmax)) + (jnp.exp(g[2] - gmax) + jnp.exp(g[3] - gmax))
    gp = 1.0 / gsum
    sel = logits[8 + 8 * (N_GROUPS - 1):8 + 8 * N_GROUPS, :]
    for gi in range(N_GROUPS - 2, -1, -1):
        sel = jnp.where(gidx == gi, logits[8 + 8 * gi:16 + 8 * gi, :], sel)
    row = lax.broadcasted_iota(jnp.int32, (EXPERTS_PER_GROUP, t), 0)
    m1 = jnp.max(sel, axis=0, keepdims=True)
    i1 = jnp.min(jnp.where(sel == m1, row, EXPERTS_PER_GROUP), axis=0, keepdims=True)
    sel2 = jnp.where(row == i1, -jnp.inf, sel)
    m2 = jnp.max(sel2, axis=0, keepdims=True)
    i2 = jnp.min(jnp.where(sel2 == m2, row, EXPERTS_PER_GROUP), axis=0, keepdims=True)
    d = jnp.exp(m2 - m1)
    w1 = 1.0 / (1.0 + d)
    w2 = d / (1.0 + d)
    return gidx * EXPERTS_PER_GROUP + i1, gidx * EXPERTS_PER_GROUP + i2, gp * w1, gp * w2


def _post_mixer(x, y, alpha, g_ref, b_ref, rw_ref, rb_ref, h1_ref, gate_ref):
    h1 = _layer_norm(alpha * x + y, g_ref[...], b_ref[...])
    h1_ref[...] = h1.reshape(h1_ref.shape)
    e1, e2, g1, g2 = _route(h1.astype(BF16), rw_ref, rb_ref)
    t = h1.shape[0]
    row = lax.broadcasted_iota(jnp.int32, (128, t), 0)
    dense = jnp.where(row == e1, g1, 0.0) + jnp.where(row == e2, g2, 0.0)
    gate_ref[...] = dense.T.reshape(gate_ref.shape)


def _load_history(ubuf, hist_ref, rows, hist_rows):
    j = pl.program_id(1)

    @pl.when(j == 0)
    def _():
        ubuf[0:hist_rows, :] = hist_ref[...].reshape(hist_rows, ubuf.shape[1])

    @pl.when(j > 0)
    def _():
        ubuf[0:hist_rows, :] = ubuf[rows:rows + hist_rows, :]


def _mixer_a_kernel(x_ref, hist_ref, win_ref, cw_ref, wout_ref, g_ref, b_ref, rw_ref, rb_ref,
                    h1_ref, gate_ref, tail_ref, ubuf, *, alpha):
    npos, nseq, d = x_ref.shape
    rows, hist_rows = npos * nseq, (SC_WIDTH - 1) * nseq
    _load_history(ubuf, hist_ref, rows, hist_rows)
    x = x_ref[...].reshape(rows, d)
    xb = x.astype(BF16)
    ubuf[hist_rows:hist_rows + rows, :] = _dot(xb, win_ref[:, d:2 * d]) * _dot(xb, win_ref[:, 2 * d:3 * d])
    tail_ref[...] = ubuf[rows:rows + hist_rows, :].reshape(tail_ref.shape)
    conv = cw_ref[0:1, :] * ubuf[0:rows, :]
    for k in range(1, SC_WIDTH):
        conv = conv + cw_ref[k:k + 1, :] * ubuf[k * nseq:k * nseq + rows, :]
    bg = _dot(xb, win_ref[:, 0:d])
    y = _dot((bg * conv).astype(BF16), wout_ref[...])
    _post_mixer(x, y, alpha, g_ref, b_ref, rw_ref, rb_ref, h1_ref, gate_ref)


def _mixer_b_kernel(x_ref, hist_ref, w1_ref, b1_ref, cw_ref, cb_ref, lg_ref, lb_ref, w2_ref, b2_ref,
                    g_ref, b_ref, rw_ref, rb_ref,
                    h1_ref, gate_ref, tail_ref, ubuf, cbuf, *, alpha):
    npos, nseq, d = x_ref.shape
    rows, hist_rows = npos * nseq, (CF_WIDTH - 1) * nseq
    _load_history(ubuf, hist_ref, rows, hist_rows)
    x = x_ref[...].reshape(rows, d)
    xb = x.astype(BF16)
    a = _dot(xb, w1_ref[:, 0:d]) + b1_ref[:, 0:d]
    gl = _dot(xb, w1_ref[:, d:2 * d]) + b1_ref[:, d:2 * d]
    ubuf[hist_rows:hist_rows + rows, :] = a * _sigmoid(gl)
    tail_ref[...] = ubuf[rows:rows + hist_rows, :].reshape(tail_ref.shape)

    def chunk(c, carry):
        r0 = pl.multiple_of(c * CONV_ROWS, CONV_ROWS)
        acc = jnp.broadcast_to(cb_ref[...], (CONV_ROWS, d))
        for k in range(CF_WIDTH):
            acc = acc + cw_ref[k:k + 1, :] * ubuf[pl.ds(pl.multiple_of(r0 + k * nseq, SUBLANES), CONV_ROWS), :]
        cbuf[pl.ds(r0, CONV_ROWS), :] = acc
        return carry

    lax.fori_loop(0, rows // CONV_ROWS, chunk, 0)
    cn = _layer_norm(cbuf[...], lg_ref[...], lb_ref[...])
    cn = cn * _sigmoid(cn)
    y = _dot(cn.astype(BF16), w2_ref[...]) + b2_ref[...]
    _post_mixer(x, y, alpha, g_ref, b_ref, rw_ref, rb_ref, h1_ref, gate_ref)


def _const_spec(shape):
    nd = len(shape)
    return pl.BlockSpec(shape, lambda *_: (0,) * nd, pipeline_mode=pl.Buffered(1))


def _mixer(kind, x3d, hist, p, *, pos_tile, seq_tile, alpha):
    npos, nseq, d = x3d.shape
    width = SC_WIDTH if kind == "a" else CF_WIDTH
    rows = pos_tile * seq_tile
    if kind == "a":
        kern = functools.partial(_mixer_a_kernel, alpha=alpha)
        weights = (p["w_in"], p["conv_w"], p["w_out"])
        wspecs = [_const_spec((d, 3 * d)), _const_spec((8, d)), _const_spec((d, d))]
        scratch = [pltpu.VMEM((rows + (width - 1) * seq_tile, d), F32)]
    else:
        kern = functools.partial(_mixer_b_kernel, alpha=alpha)
        weights = (p["w_pw1"], p["b_pw1"], p["conv_w"], p["conv_b"], p["cln_g"], p["cln_b"], p["w_pw2"], p["b_pw2"])
        wspecs = [_const_spec((d, 2 * d)), _const_spec((1, 2 * d)), _const_spec((32, d)), _const_spec((1, d)),
                  _const_spec((1, d)), _const_spec((1, d)), _const_spec((d, d)), _const_spec((1, d))]
        scratch = [pltpu.VMEM((rows + (width - 1) * seq_tile, d), F32), pltpu.VMEM((rows, d), F32)]
    common = (p["ln1_g"], p["ln1_b"], p["rw"], p["rb"])
    cspecs = [_const_spec((1, d)), _const_spec((1, d)), _const_spec((ROUTER_ROWS, d)), _const_spec((ROUTER_ROWS, 1))]
    return pl.pallas_call(
        kern,
        grid=(nseq // seq_tile, npos // pos_tile),
        in_specs=[
            pl.BlockSpec((pos_tile, seq_tile, d), lambda s, j: (j, s, 0)),
            pl.BlockSpec((width - 1, seq_tile, d), lambda s, j: (0, s, 0)),
        ] + wspecs + cspecs,
        out_specs=[
            pl.BlockSpec((pos_tile, seq_tile, d), lambda s, j: (j, s, 0)),
            pl.BlockSpec((pos_tile, seq_tile, 128), lambda s, j: (j, s, 0)),
            pl.BlockSpec((width - 1, seq_tile, d), lambda s, j: (0, s, 0)),
        ],
        out_shape=[
            jax.ShapeDtypeStruct((npos, nseq, d), F32),
            jax.ShapeDtypeStruct((npos, nseq, 128), F32),
            jax.ShapeDtypeStruct((width - 1, nseq, d), F32),
        ],
        scratch_shapes=scratch,
        compiler_params=pltpu.CompilerParams(
            dimension_semantics=("arbitrary", "arbitrary"), vmem_limit_bytes=VMEM_LIMIT),
        name="mixer_" + kind,
    )(x3d, hist, *weights, *common)


def _moe_dense_kernel(h_ref, gate_ref, wg_ref, wu_ref, wd_ref, g_ref, b_ref, o_ref, acc, hb, *, alpha):
    e = pl.program_id(1)

    @pl.when(e == 0)
    def _():
        acc[...] = jnp.zeros_like(acc)
        hb[...] = h_ref[...].astype(BF16)

    lane = lax.broadcasted_iota(jnp.int32, gate_ref.shape, 1)
    gcol = jnp.sum(jnp.where(lane == e, gate_ref[...], 0.0), axis=1, keepdims=True)
    hg = _dot(hb[...], wg_ref[...])
    hu = _dot(hb[...], wu_ref[...])
    hid = hg * _sigmoid(hg) * hu * gcol
    acc[...] += _dot(hid.astype(BF16), wd_ref[...])

    @pl.when(e == pl.num_programs(1) - 1)
    def _():
        o_ref[...] = _layer_norm(alpha * h_ref[...] + acc[...], g_ref[...], b_ref[...])


def _moe_dense(h3d, gate3d, p, *, tile, alpha):
    npos, nseq, d = h3d.shape
    n = npos * nseq
    h1, gate = h3d.reshape(n, d), gate3d.reshape(n, 128)
    f = p["w_gate"].shape[-1]
    kern = functools.partial(_moe_dense_kernel, alpha=alpha)
    out = pl.pallas_call(
        kern,
        grid=(n // tile, N_EXPERTS),
        in_specs=[
            pl.BlockSpec((tile, d), lambda i, e: (i, 0)),
            pl.BlockSpec((tile, 128), lambda i, e: (i, 0)),
            pl.BlockSpec((None, d, f), lambda i, e: (e, 0, 0)),
            pl.BlockSpec((None, d, f), lambda i, e: (e, 0, 0)),
            pl.BlockSpec((None, f, d), lambda i, e: (e, 0, 0)),
            pl.BlockSpec((1, d), lambda i, e: (0, 0)),
            pl.BlockSpec((1, d), lambda i, e: (0, 0)),
        ],
        out_specs=pl.BlockSpec((tile, d), lambda i, e: (i, 0)),
        out_shape=jax.ShapeDtypeStruct((n, d), F32),
        scratch_shapes=[pltpu.VMEM((tile, d), F32), pltpu.VMEM((tile, d), BF16)],
        compiler_params=pltpu.CompilerParams(
            dimension_semantics=("arbitrary", "arbitrary"), vmem_limit_bytes=VMEM_LIMIT),
        name="moe_dense",
    )(h1, gate, p["w_gate"], p["w_up"], p["w_down"], p["ln2_g"], p["ln2_b"])
    return out.reshape(npos, nseq, d)


def _pad_rows(w, rows):
    return jnp.concatenate([w, jnp.zeros((rows - w.shape[0],) + w.shape[1:], w.dtype)], axis=0)


def _router_params(gw, gb, ew, eb):
    d = gw.shape[0]
    rw = jnp.concatenate([gw.T, jnp.zeros((8 - N_GROUPS, d), gw.dtype), ew.T], axis=0).astype(BF16)
    rb = jnp.concatenate([gb, jnp.zeros((8 - N_GROUPS,), gb.dtype), eb])[:, None].astype(F32)
    return rw, rb


def kernel(x_prompt, x_sample, state_conv_a, state_conv_b, meta_tokens, sc_w_in, sc_conv_w, sc_w_out, cf_w_pw1, cf_b_pw1, cf_conv_w, cf_conv_b, cf_ln_g, cf_ln_b, cf_w_pw2, cf_b_pw2, ln1_g, ln1_b, ln2_g, ln2_b, rt_group_w, rt_group_b, rt_expert_w, rt_expert_b, moe_w_gate, moe_w_up, moe_w_down):
    bsz, seq, d = x_prompt.shape
    dec_b, dec_t, _ = x_sample.shape
    n_meta = meta_tokens.shape[0]
    depth = ln1_g.shape[0]
    alpha = float((2 * depth) ** 0.25)
    row = lambda v: v[None, :].astype(F32)

    layer = []
    for i in range(depth):
        rw, rb = _router_params(rt_group_w[i], rt_group_b[i], rt_expert_w[i], rt_expert_b[i])
        layer.append(dict(
            ln1_g=row(ln1_g[i]), ln1_b=row(ln1_b[i]), ln2_g=row(ln2_g[i]), ln2_b=row(ln2_b[i]), rw=rw, rb=rb,
            w_gate=moe_w_gate[i].astype(BF16), w_up=moe_w_up[i].astype(BF16), w_down=moe_w_down[i].astype(BF16)))
    pa = dict(layer[0], w_in=sc_w_in[0].astype(BF16), conv_w=_pad_rows(sc_conv_w[0], 8),
              w_out=sc_w_out[0].astype(BF16))
    pb = dict(layer[1], w_pw1=cf_w_pw1[0].astype(BF16), b_pw1=row(cf_b_pw1[0]),
              conv_w=_pad_rows(cf_conv_w[0], 32), conv_b=row(cf_conv_b[0]),
              cln_g=row(cf_ln_g[0]), cln_b=row(cf_ln_b[0]), w_pw2=cf_w_pw2[0].astype(BF16), b_pw2=row(cf_b_pw2[0]))

    xp = jnp.swapaxes(x_prompt, 0, 1)
    xs = jnp.swapaxes(x_sample, 0, 1)
    xm = jnp.broadcast_to(meta_tokens.astype(F32)[:, None, :], (n_meta, bsz, d))
    p_tile = 512 // bsz

    zero_a = jnp.zeros((SC_WIDTH - 1, bsz, d), F32)
    hm, gm, tail_am = _mixer("a", xm, zero_a, pa, pos_tile=n_meta, seq_tile=bsz, alpha=alpha)
    hp, gp, tail_ap = _mixer("a", xp, tail_am, pa, pos_tile=p_tile, seq_tile=bsz, alpha=alpha)
    hs, gs, tail_as = _mixer("a", xs, jnp.swapaxes(state_conv_a[0], 0, 1), pa,
                             pos_tile=dec_t, seq_tile=32, alpha=alpha)
    hm = _moe_dense(hm, gm, pa, tile=n_meta * bsz, alpha=alpha)
    hp = _moe_dense(hp, gp, pa, tile=1024, alpha=alpha)
    hs = _moe_dense(hs, gs, pa, tile=1024, alpha=alpha)

    zero_b = jnp.zeros((CF_WIDTH - 1, bsz, d), F32)
    _, _, tail_bm = _mixer("b", hm, zero_b, pb, pos_tile=n_meta, seq_tile=bsz, alpha=alpha)
    hp, gp, tail_bp = _mixer("b", hp, tail_bm, pb, pos_tile=p_tile, seq_tile=bsz, alpha=alpha)
    hs, gs, tail_bs = _mixer("b", hs, jnp.swapaxes(state_conv_b[0], 0, 1), pb,
                             pos_tile=dec_t, seq_tile=32, alpha=alpha)
    hp = _moe_dense(hp, gp, pb, tile=1024, alpha=alpha)
    hs = _moe_dense(hs, gs, pb, tile=1024, alpha=alpha)

    unswap = lambda t: jnp.swapaxes(t, 0, 1)
    return (unswap(hp), unswap(hs), unswap(tail_ap)[None], unswap(tail_bp)[None],
            unswap(tail_as)[None], unswap(tail_bs)[None])
```

```python
import functools

import jax
import jax.numpy as jnp
from jax import lax
from jax.experimental import pallas as pl
from jax.experimental.pallas import tpu as pltpu

F32 = jnp.float32
BF16 = jnp.bfloat16
I32 = jnp.int32

LN_EPS = 1e-5
N_GROUPS = 4
EXPERTS_PER_GROUP = 8
N_EXPERTS = N_GROUPS * EXPERTS_PER_GROUP
SC_WIDTH = 3
CF_WIDTH = 31
ROUTER_ROWS = 8 + N_EXPERTS
SUBLANES = 8
LANES = 128
CONV_ROWS = 16
VMEM_LIMIT = 56 * 1024 * 1024
MOE_BLOCK = 256
LOG_MOE_BLOCK = 8
RANK_TILE = 1024
BLOCK_LANES = 256


def _dot(a, b):
    return jnp.dot(a, b, preferred_element_type=F32)


def _layer_norm(z, g, b):
    mu = jnp.mean(z, axis=-1, keepdims=True)
    zc = z - mu
    var = jnp.mean(zc * zc, axis=-1, keepdims=True)
    return zc * lax.rsqrt(var + LN_EPS) * g + b


def _sigmoid(x):
    return 1.0 / (1.0 + jnp.exp(-x))


def _route(h1b, rw_ref, rb_ref):
    t = h1b.shape[0]
    logits = lax.dot_general(rw_ref[...], h1b, (((1,), (1,)), ((), ())),
                             preferred_element_type=F32) + rb_ref[...]
    g = [logits[i:i + 1, :] for i in range(N_GROUPS)]
    gmax = jnp.maximum(jnp.maximum(g[0], g[1]), jnp.maximum(g[2], g[3]))
    gidx = jnp.where(g[0] == gmax, 0, jnp.where(g[1] == gmax, 1, jnp.where(g[2] == gmax, 2, 3)))
    gsum = (jnp.exp(g[0] - gmax) + jnp.exp(g[1] - gmax)) + (jnp.exp(g[2] - gmax) + jnp.exp(g[3] - gmax))
    gp = 1.0 / gsum
    sel = logits[8 + 8 * (N_GROUPS - 1):8 + 8 * N_GROUPS, :]
    for gi in range(N_GROUPS - 2, -1, -1):
        sel = jnp.where(gidx == gi, logits[8 + 8 * gi:16 + 8 * gi, :], sel)
    row = lax.broadcasted_iota(I32, (EXPERTS_PER_GROUP, t), 0)
    m1 = jnp.max(sel, axis=0, keepdims=True)
    i1 = jnp.min(jnp.where(sel == m1, row, EXPERTS_PER_GROUP), axis=0, keepdims=True)
    sel2 = jnp.where(row == i1, -jnp.inf, sel)
    m2 = jnp.max(sel2, axis=0, keepdims=True)
    i2 = jnp.min(jnp.where(sel2 == m2, row, EXPERTS_PER_GROUP), axis=0, keepdims=True)
    d = jnp.exp(m2 - m1)
    w1 = 1.0 / (1.0 + d)
    w2 = d / (1.0 + d)
    return gidx * EXPERTS_PER_GROUP + i1, gidx * EXPERTS_PER_GROUP + i2, gp * w1, gp * w2


def _post_mixer(x, y, alpha, g_ref, b_ref, rw_ref, rb_ref, h1_ref, eid_ref, gate_ref):
    h1 = _layer_norm(alpha * x + y, g_ref[...], b_ref[...])
    h1_ref[...] = h1
    e1, e2, g1, g2 = _route(h1.astype(BF16), rw_ref, rb_ref)
    t = h1.shape[0]
    row8 = lax.broadcasted_iota(I32, (SUBLANES, t), 0)
    eid_ref[...] = jnp.where(row8 == 0, e1, jnp.where(row8 == 1, e2, -1))
    row = lax.broadcasted_iota(I32, (LANES, t), 0)
    gates = jnp.where(row == 0, g1, jnp.where(row == 1, g2, 0.0))
    gate_ref[...] = gates.T


def _load_history(ubuf, hist_ref, rows, hist_rows):
    j = pl.program_id(1)

    @pl.when(j == 0)
    def _():
        ubuf[0:hist_rows, :] = hist_ref[...]

    @pl.when(j > 0)
    def _():
        ubuf[0:hist_rows, :] = ubuf[rows:rows + hist_rows, :]


def _mixer_a_kernel(x_ref, hist_ref, win_ref, cw_ref, wout_ref, g_ref, b_ref, rw_ref, rb_ref,
                    h1_ref, eid_ref, gate_ref, tail_ref, ubuf, *, alpha, nseq):
    rows, d = x_ref.shape
    hist_rows = (SC_WIDTH - 1) * nseq
    _load_history(ubuf, hist_ref, rows, hist_rows)
    x = x_ref[...]
    xb = x.astype(BF16)
    ubuf[hist_rows:hist_rows + rows, :] = _dot(xb, win_ref[:, d:2 * d]) * _dot(xb, win_ref[:, 2 * d:3 * d])
    tail_ref[...] = ubuf[rows:rows + hist_rows, :]
    conv = cw_ref[0:1, :] * ubuf[0:rows, :]
    for k in range(1, SC_WIDTH):
        conv = conv + cw_ref[k:k + 1, :] * ubuf[k * nseq:k * nseq + rows, :]
    bg = _dot(xb, win_ref[:, 0:d])
    y = _dot((bg * conv).astype(BF16), wout_ref[...])
    _post_mixer(x, y, alpha, g_ref, b_ref, rw_ref, rb_ref, h1_ref, eid_ref, gate_ref)


def _mixer_b_kernel(x_ref, hist_ref, w1_ref, b1_ref, cw_ref, cb_ref, lg_ref, lb_ref, w2_ref, b2_ref,
                    g_ref, b_ref, rw_ref, rb_ref,
                    h1_ref, eid_ref, gate_ref, tail_ref, ubuf, cbuf, *, alpha, nseq):
    rows, d = x_ref.shape
    hist_rows = (CF_WIDTH - 1) * nseq
    _load_history(ubuf, hist_ref, rows, hist_rows)
    x = x_ref[...]
    xb = x.astype(BF16)
    a = _dot(xb, w1_ref[:, 0:d]) + b1_ref[:, 0:d]
    gl = _dot(xb, w1_ref[:, d:2 * d]) + b1_ref[:, d:2 * d]
    ubuf[hist_rows:hist_rows + rows, :] = a * _sigmoid(gl)
    tail_ref[...] = ubuf[rows:rows + hist_rows, :]

    def chunk(c, carry):
        r0 = pl.multiple_of(c * CONV_ROWS, CONV_ROWS)
        acc = jnp.broadcast_to(cb_ref[...], (CONV_ROWS, d))
        for k in range(CF_WIDTH):
            acc = acc + cw_ref[k:k + 1, :] * ubuf[pl.ds(pl.multiple_of(r0 + k * nseq, SUBLANES), CONV_ROWS), :]
        cbuf[pl.ds(r0, CONV_ROWS), :] = acc
        return carry

    lax.fori_loop(0, rows // CONV_ROWS, chunk, 0)
    cn = _layer_norm(cbuf[...], lg_ref[...], lb_ref[...])
    cn = cn * _sigmoid(cn)
    y = _dot(cn.astype(BF16), w2_ref[...]) + b2_ref[...]
    _post_mixer(x, y, alpha, g_ref, b_ref, rw_ref, rb_ref, h1_ref, eid_ref, gate_ref)


def _const_spec(shape):
    nd = len(shape)
    return pl.BlockSpec(shape, lambda *_: (0,) * nd, pipeline_mode=pl.Buffered(1))


def _mixer(kind, x2d, hist2d, p, *, pos_tile, seq_tile, nseq, alpha):
    n, d = x2d.shape
    width = SC_WIDTH if kind == "a" else CF_WIDTH
    rows = pos_tile * seq_tile
    hist_rows = (width - 1) * seq_tile
    seq_blocks = nseq // seq_tile
    steps = n // (rows * seq_blocks)
    if kind == "a":
        kern = functools.partial(_mixer_a_kernel, alpha=alpha, nseq=seq_tile)
        weights = (p["w_in"], p["conv_w"], p["w_out"])
        wspecs = [_const_spec((d, 3 * d)), _const_spec((8, d)), _const_spec((d, d))]
        scratch = [pltpu.VMEM((rows + hist_rows, d), F32)]
    else:
        kern = functools.partial(_mixer_b_kernel, alpha=alpha, nseq=seq_tile)
        weights = (p["w_pw1"], p["b_pw1"], p["conv_w"], p["conv_b"], p["cln_g"], p["cln_b"], p["w_pw2"], p["b_pw2"])
        wspecs = [_const_spec((d, 2 * d)), _const_spec((1, 2 * d)), _const_spec((32, d)), _const_spec((1, d)),
                  _const_spec((1, d)), _const_spec((1, d)), _const_spec((d, d)), _const_spec((1, d))]
        scratch = [pltpu.VMEM((rows + hist_rows, d), F32), pltpu.VMEM((rows, d), F32)]
    common = (p["ln1_g"], p["ln1_b"], p["rw"], p["rb"])
    cspecs = [_const_spec((1, d)), _const_spec((1, d)), _const_spec((ROUTER_ROWS, d)), _const_spec((ROUTER_ROWS, 1))]
    return pl.pallas_call(
        kern,
        grid=(seq_blocks, steps),
        in_specs=[
            pl.BlockSpec((rows, d), lambda s, j: (s * steps + j, 0)),
            pl.BlockSpec((hist_rows, d), lambda s, j: (s, 0)),
        ] + wspecs + cspecs,
        out_specs=[
            pl.BlockSpec((rows, d), lambda s, j: (s * steps + j, 0)),
            pl.BlockSpec((SUBLANES, rows), lambda s, j: (0, s * steps + j)),
            pl.BlockSpec((rows, LANES), lambda s, j: (s * steps + j, 0)),
            pl.BlockSpec((hist_rows, d), lambda s, j: (s, 0)),
        ],
        out_shape=[
            jax.ShapeDtypeStruct((n, d), F32),
            jax.ShapeDtypeStruct((SUBLANES, n), I32),
            jax.ShapeDtypeStruct((n, LANES), F32),
            jax.ShapeDtypeStruct((seq_blocks * hist_rows, d), F32),
        ],
        scratch_shapes=scratch,
        compiler_params=pltpu.CompilerParams(
            dimension_semantics=("arbitrary", "arbitrary"), vmem_limit_bytes=VMEM_LIMIT),
        name="mixer_" + kind,
    )(x2d, hist2d, *weights, *common)


def _rank_kernel(eid_ref, pos_ref, blk_ref, tri, tot, prior):
    phase, i = pl.program_id(0), pl.program_id(1)
    tr = eid_ref.shape[1]
    row = lax.broadcasted_iota(I32, (N_EXPERTS, tr), 0)
    oh1 = (row == eid_ref[0:1, :]).astype(F32)
    oh2 = (row == eid_ref[1:2, :]).astype(F32)
    c1 = jnp.sum(oh1, axis=1, keepdims=True)
    c2 = jnp.sum(oh2, axis=1, keepdims=True)
    pos_ref[...] = jnp.zeros_like(pos_ref)

    @pl.when(jnp.logical_and(phase == 0, i == 0))
    def _():
        tot[...] = jnp.zeros_like(tot)
        a = lax.broadcasted_iota(I32, tri.shape, 0)
        b = lax.broadcasted_iota(I32, tri.shape, 1)
        tri[...] = (a <= b).astype(BF16)

    @pl.when(phase == 0)
    def _():
        tot[...] += c1 + c2

    @pl.when(jnp.logical_and(phase == 1, i == 0))
    def _():
        cnt = tot[...].astype(I32)
        seg = ((cnt + (MOE_BLOCK - 1)) >> LOG_MOE_BLOCK) << LOG_MOE_BLOCK
        r128 = lax.broadcasted_iota(I32, cnt.shape, 0)
        base = jnp.zeros_like(cnt)
        for e in range(N_EXPERTS - 1):
            base = base + jnp.where(r128 > e, seg[e:e + 1, :], 0)
        prior[...] = base.astype(F32)
        end_blk = (base + seg) >> LOG_MOE_BLOCK
        lane = lax.broadcasted_iota(I32, (N_EXPERTS, BLOCK_LANES), 1)
        expert_of = jnp.sum((end_blk[:, 0:1] <= lane).astype(F32), axis=0, keepdims=True).astype(I32)
        active = (lane[0:1, :] < end_blk[N_EXPERTS - 1:N_EXPERTS, 0:1]).astype(I32)
        r8 = lax.broadcasted_iota(I32, (SUBLANES, BLOCK_LANES), 0)
        blk_ref[...] = jnp.where(r8 == 0, jnp.minimum(expert_of, N_EXPERTS - 1), jnp.where(r8 == 1, active, 0))

    @pl.when(phase == 1)
    def _():
        cs = _dot(jnp.concatenate([oh1, oh2], axis=0).astype(BF16), tri[...])
        p = prior[:, 0:1]
        pos1 = jnp.sum(oh1 * (p + cs[0:N_EXPERTS, :] - 1.0), axis=0, keepdims=True).astype(I32)
        pos2 = jnp.sum(oh2 * (p + c1 + cs[N_EXPERTS:2 * N_EXPERTS, :] - 1.0), axis=0, keepdims=True).astype(I32)
        prior[...] += c1 + c2
        for q in range(tr // LANES):
            pos_ref[q, 0:1, :] = pos1[:, q * LANES:(q + 1) * LANES]
            pos_ref[q, 1:2, :] = pos2[:, q * LANES:(q + 1) * LANES]


def _rank(eid):
    n = eid.shape[1]
    steps = n // RANK_TILE
    q = RANK_TILE // LANES
    return pl.pallas_call(
        _rank_kernel,
        grid=(2, steps),
        in_specs=[pl.BlockSpec((SUBLANES, RANK_TILE), lambda ph, i: (0, i))],
        out_specs=[
            pl.BlockSpec((q, SUBLANES, LANES), lambda ph, i: (i * ph, 0, 0)),
            pl.BlockSpec((SUBLANES, BLOCK_LANES), lambda ph, i: (0, 0)),
        ],
        out_shape=[
            jax.ShapeDtypeStruct((n // LANES, SUBLANES, LANES), I32),
            jax.ShapeDtypeStruct((SUBLANES, BLOCK_LANES), I32),
        ],
        scratch_shapes=[pltpu.VMEM((RANK_TILE, RANK_TILE), BF16), pltpu.VMEM((N_EXPERTS, LANES), F32),
                        pltpu.VMEM((N_EXPERTS, LANES), F32)],
        compiler_params=pltpu.CompilerParams(
            dimension_semantics=("arbitrary", "arbitrary"), vmem_limit_bytes=VMEM_LIMIT),
        name="rank",
    )(eid)


def _token_rows(ref, tok):
    return ref.at[pl.ds(pl.multiple_of(tok * SUBLANES, SUBLANES), SUBLANES), :]


def _pair_rows(pos_smem, n):
    return pos_smem[n >> 7, 0, n & (LANES - 1)], pos_smem[n >> 7, 1, n & (LANES - 1)]


def _dispatch_kernel(*refs, chunk_base, aliased):
    if aliased:
        pos_hbm, h_ref, _, xs_hbm, pos_smem, tokbuf, sem_pos, sem_row = refs
    else:
        pos_hbm, h_ref, xs_hbm, pos_smem, tokbuf, sem_pos, sem_row = refs
    t = h_ref.shape[0]
    nq = t // LANES
    cp = pltpu.make_async_copy(pos_hbm.at[pl.ds(chunk_base + pl.program_id(0) * nq, nq)], pos_smem, sem_pos)
    cp.start()
    for c in range(SUBLANES):
        tokbuf[pl.ds(c, t, stride=SUBLANES), :] = h_ref[:, c * LANES:(c + 1) * LANES]
    cp.wait()

    def issue(n, carry):
        p1, p2 = _pair_rows(pos_smem, n)
        pltpu.make_async_copy(_token_rows(tokbuf, n), _token_rows(xs_hbm, p1), sem_row).start()
        pltpu.make_async_copy(_token_rows(tokbuf, n), _token_rows(xs_hbm, p2), sem_row).start()
        return carry

    lax.fori_loop(0, t, issue, 0)

    def drain(n, carry):
        pltpu.make_async_copy(_token_rows(tokbuf, n), _token_rows(xs_hbm, 0), sem_row).wait()
        pltpu.make_async_copy(_token_rows(tokbuf, n), _token_rows(xs_hbm, 0), sem_row).wait()
        return carry

    lax.fori_loop(0, t, drain, 0)


def _dispatch(pos, h2d, xs, *, tile, chunk_base, sorted_rows):
    n, d = h2d.shape
    aliased = xs is not None
    kern = functools.partial(_dispatch_kernel, chunk_base=chunk_base, aliased=aliased)
    in_specs = [pl.BlockSpec(memory_space=pl.ANY), pl.BlockSpec((tile, d), lambda i: (i, 0))]
    args = [pos, h2d]
    if aliased:
        in_specs.append(pl.BlockSpec(memory_space=pl.ANY))
        args.append(xs)
    return pl.pallas_call(
        kern,
        grid=(n // tile,),
        in_specs=in_specs,
        out_specs=pl.BlockSpec(memory_space=pl.ANY),
        out_shape=jax.ShapeDtypeStruct((sorted_rows * SUBLANES, LANES), F32),
        scratch_shapes=[pltpu.SMEM((tile // LANES, SUBLANES, LANES), I32), pltpu.VMEM((tile * SUBLANES, LANES), F32),
                        pltpu.SemaphoreType.DMA, pltpu.SemaphoreType.DMA],
        input_output_aliases={2: 0} if aliased else {},
        compiler_params=pltpu.CompilerParams(
            dimension_semantics=("arbitrary",), vmem_limit_bytes=VMEM_LIMIT),
        name="dispatch",
    )(*args)


def _moe_kernel(blk_ref, xs_ref, wg_ref, wu_ref, wd_ref, ys_ref, xb):
    b = pl.program_id(0)
    tm = xb.shape[0]

    @pl.when(blk_ref[1, b] == 1)
    def _():
        for c in range(SUBLANES):
            xb[:, c * LANES:(c + 1) * LANES] = xs_ref[pl.ds(c, tm, stride=SUBLANES), :].astype(BF16)
        x = xb[...]
        hg = _dot(x, wg_ref[...].astype(BF16))
        hu = _dot(x, wu_ref[...].astype(BF16))
        hid = (hg * _sigmoid(hg) * hu).astype(BF16)
        y = _dot(hid, wd_ref[...].astype(BF16))
        for c in range(SUBLANES):
            ys_ref[pl.ds(c, tm, stride=SUBLANES), :] = y[:, c * LANES:(c + 1) * LANES]

    @pl.when(blk_ref[1, b] == 0)
    def _():
        ys_ref[...] = jnp.zeros_like(ys_ref)


def _moe(blk, xs, wg, wu, wd):
    nblocks = xs.shape[0] // (MOE_BLOCK * SUBLANES)
    _, d, f = wg.shape
    return pl.pallas_call(
        _moe_kernel,
        grid_spec=pltpu.PrefetchScalarGridSpec(
            num_scalar_prefetch=1,
            grid=(nblocks,),
            in_specs=[
                pl.BlockSpec((MOE_BLOCK * SUBLANES, LANES), lambda b, blk: (b, 0)),
                pl.BlockSpec((None, d, f), lambda b, blk: (blk[0, b], 0, 0)),
                pl.BlockSpec((None, d, f), lambda b, blk: (blk[0, b], 0, 0)),
                pl.BlockSpec((None, f, d), lambda b, blk: (blk[0, b], 0, 0)),
            ],
            out_specs=pl.BlockSpec((MOE_BLOCK * SUBLANES, LANES), lambda b, blk: (b, 0)),
            scratch_shapes=[pltpu.VMEM((MOE_BLOCK, d), BF16)],
        ),
        out_shape=jax.ShapeDtypeStruct(xs.shape, F32),
        compiler_params=pltpu.CompilerParams(
            dimension_semantics=("arbitrary",), vmem_limit_bytes=VMEM_LIMIT),
        name="moe",
    )(blk, xs, wg, wu, wd)


def _combine_kernel(pos_hbm, ys_hbm, h_ref, gate_ref, g_ref, b_ref, o_ref,
                    pos_smem, buf1, buf2, sem_pos, sem_row, *, chunk_base, alpha):
    t = h_ref.shape[0]
    nq = t // LANES
    cp = pltpu.make_async_copy(pos_hbm.at[pl.ds(chunk_base + pl.program_id(0) * nq, nq)], pos_smem, sem_pos)
    cp.start()
    cp.wait()

    def issue(n, carry):
        p1, p2 = _pair_rows(pos_smem, n)
        pltpu.make_async_copy(_token_rows(ys_hbm, p1), _token_rows(buf1, n), sem_row).start()
        pltpu.make_async_copy(_token_rows(ys_hbm, p2), _token_rows(buf2, n), sem_row).start()
        return carry

    lax.fori_loop(0, t, issue, 0)

    def drain(n, carry):
        pltpu.make_async_copy(_token_rows(ys_hbm, 0), _token_rows(buf1, n), sem_row).wait()
        pltpu.make_async_copy(_token_rows(ys_hbm, 0), _token_rows(buf2, n), sem_row).wait()
        return carry

    lax.fori_loop(0, t, drain, 0)
    y1 = jnp.concatenate([buf1[pl.ds(c, t, stride=SUBLANES), :] for c in range(SUBLANES)], axis=1)
    y2 = jnp.concatenate([buf2[pl.ds(c, t, stride=SUBLANES), :] for c in range(SUBLANES)], axis=1)
    gate = gate_ref[...]
    z = alpha * h_ref[...] + gate[:, 0:1] * y1 + gate[:, 1:2] * y2
    o_ref[...] = _layer_norm(z, g_ref[...], b_ref[...])


def _combine(pos, ys, h2d, gate, p, *, tile, chunk_base, alpha):
    n, d = h2d.shape
    kern = functools.partial(_combine_kernel, chunk_base=chunk_base, alpha=alpha)
    return pl.pallas_call(
        kern,
        grid=(n // tile,),
        in_specs=[
            pl.BlockSpec(memory_space=pl.ANY), pl.BlockSpec(memory_space=pl.ANY),
            pl.BlockSpec((tile, d), lambda i: (i, 0)),
            pl.BlockSpec((tile, LANES), lambda i: (i, 0)),
            _const_spec((1, d)), _const_spec((1, d)),
        ],
        out_specs=pl.BlockSpec((tile, d), lambda i: (i, 0)),
        out_shape=jax.ShapeDtypeStruct((n, d), F32),
        scratch_shapes=[pltpu.SMEM((tile // LANES, SUBLANES, LANES), I32),
                        pltpu.VMEM((tile * SUBLANES, LANES), F32), pltpu.VMEM((tile * SUBLANES, LANES), F32),
                        pltpu.SemaphoreType.DMA, pltpu.SemaphoreType.DMA],
        compiler_params=pltpu.CompilerParams(
            dimension_semantics=("arbitrary",), vmem_limit_bytes=VMEM_LIMIT),
        name="combine",
    )(pos, ys, h2d, gate, p["ln2_g"], p["ln2_b"])


def _sparse_moe(sets, p, *, alpha):
    n_tok = sum(h.shape[0] for h, _, _, _ in sets)
    n_pad = -(-n_tok // RANK_TILE) * RANK_TILE
    eid = jnp.concatenate([e for _, e, _, _ in sets] + [jnp.full((SUBLANES, n_pad - n_tok), -1, I32)], axis=1)
    pos, blk = _rank(eid)
    nblocks = -(-(2 * n_tok + N_EXPERTS * (MOE_BLOCK - 1)) // MOE_BLOCK)
    assert nblocks <= BLOCK_LANES
    xs, base = None, 0
    bases = []
    for h, _, _, tile in sets:
        xs = _dispatch(pos, h, xs, tile=tile, chunk_base=base, sorted_rows=nblocks * MOE_BLOCK)
        bases.append(base)
        base += h.shape[0] // LANES
    ys = _moe(blk, xs, p["w_gate"], p["w_up"], p["w_down"])
    return [_combine(pos, ys, h, g, p, tile=tile, chunk_base=cb, alpha=alpha)
            for (h, _, g, tile), cb in zip(sets, bases)]


def _pad_rows(w, rows):
    return jnp.concatenate([w, jnp.zeros((rows - w.shape[0],) + w.shape[1:], w.dtype)], axis=0)


def _router_params(gw, gb, ew, eb):
    d = gw.shape[0]
    rw = jnp.concatenate([gw.T, jnp.zeros((8 - N_GROUPS, d), gw.dtype), ew.T], axis=0).astype(BF16)
    rb = jnp.concatenate([gb, jnp.zeros((8 - N_GROUPS,), gb.dtype), eb])[:, None].astype(F32)
    return rw, rb


def _to_blocks(x, seq_tile):
    nseq, npos, d = x.shape
    return x.reshape(nseq // seq_tile, seq_tile, npos, d).swapaxes(1, 2).reshape(nseq * npos, d)


def _from_blocks(x2d, nseq, seq_tile):
    n, d = x2d.shape
    npos = n // nseq
    return x2d.reshape(nseq // seq_tile, npos, seq_tile, d).swapaxes(1, 2).reshape(nseq, npos, d)


def kernel(x_prompt, x_sample, state_conv_a, state_conv_b, meta_tokens, sc_w_in, sc_conv_w, sc_w_out, cf_w_pw1, cf_b_pw1, cf_conv_w, cf_conv_b, cf_ln_g, cf_ln_b, cf_w_pw2, cf_b_pw2, ln1_g, ln1_b, ln2_g, ln2_b, rt_group_w, rt_group_b, rt_expert_w, rt_expert_b, moe_w_gate, moe_w_up, moe_w_down):
    bsz, seq, d = x_prompt.shape
    dec_b, dec_t, _ = x_sample.shape
    n_meta = meta_tokens.shape[0]
    depth = ln1_g.shape[0]
    alpha = float((2 * depth) ** 0.25)
    row = lambda v: v[None, :].astype(F32)
    s_seq = 32
    p_pos = 512 // bsz

    layer = []
    for i in range(depth):
        rw, rb = _router_params(rt_group_w[i], rt_group_b[i], rt_expert_w[i], rt_expert_b[i])
        layer.append(dict(
            ln1_g=row(ln1_g[i]), ln1_b=row(ln1_b[i]), ln2_g=row(ln2_g[i]), ln2_b=row(ln2_b[i]), rw=rw, rb=rb,
            w_gate=moe_w_gate[i], w_up=moe_w_up[i], w_down=moe_w_down[i]))
    pa = dict(layer[0], w_in=sc_w_in[0].astype(BF16), conv_w=_pad_rows(sc_conv_w[0], 8),
              w_out=sc_w_out[0].astype(BF16))
    pb = dict(layer[1], w_pw1=cf_w_pw1[0].astype(BF16), b_pw1=row(cf_b_pw1[0]),
              conv_w=_pad_rows(cf_conv_w[0], 32), conv_b=row(cf_conv_b[0]),
              cln_g=row(cf_ln_g[0]), cln_b=row(cf_ln_b[0]), w_pw2=cf_w_pw2[0].astype(BF16), b_pw2=row(cf_b_pw2[0]))

    xp = _to_blocks(x_prompt, bsz)
    xs = _to_blocks(x_sample, s_seq)
    xm = jnp.broadcast_to(meta_tokens.astype(F32)[:, None, :], (n_meta, bsz, d)).reshape(n_meta * bsz, d)
    mix_p = dict(pos_tile=p_pos, seq_tile=bsz, nseq=bsz, alpha=alpha)
    mix_s = dict(pos_tile=dec_t, seq_tile=s_seq, nseq=dec_b, alpha=alpha)
    mix_m = dict(pos_tile=n_meta, seq_tile=bsz, nseq=bsz, alpha=alpha)

    zero_a = jnp.zeros(((SC_WIDTH - 1) * bsz, d), F32)
    hm, em, gm, tail_am = _mixer("a", xm, zero_a, pa, **mix_m)
    hp, ep, gp, tail_ap = _mixer("a", xp, tail_am, pa, **mix_p)
    hs, es, gs, tail_as = _mixer("a", xs, _to_blocks(state_conv_a[0], s_seq), pa, **mix_s)
    hp, hs, hm = _sparse_moe([(hp, ep, gp, 512), (hs, es, gs, 512), (hm, em, gm, n_meta * bsz)], pa, alpha=alpha)

    zero_b = jnp.zeros(((CF_WIDTH - 1) * bsz, d), F32)
    _, _, _, tail_bm = _mixer("b", hm, zero_b, pb, **mix_m)
    hp, ep, gp, tail_bp = _mixer("b", hp, tail_bm, pb, **mix_p)
    hs, es, gs, tail_bs = _mixer("b", hs, _to_blocks(state_conv_b[0], s_seq), pb, **mix_s)
    hp, hs = _sparse_moe([(hp, ep, gp, 512), (hs, es, gs, 512)], pb, alpha=alpha)

    return (_from_blocks(hp, bsz, bsz), _from_blocks(hs, dec_b, s_seq),
            _from_blocks(tail_ap, bsz, bsz)[None], _from_blocks(tail_bp, bsz, bsz)[None],
            _from_blocks(tail_as, dec_b, s_seq)[None], _from_blocks(tail_bs, dec_b, s_seq)[None])
```

```python
import functools

import jax
import jax.numpy as jnp
from jax import lax
from jax.experimental import pallas as pl
from jax.experimental.pallas import tpu as pltpu

F32 = jnp.float32
BF16 = jnp.bfloat16
I32 = jnp.int32

LN_EPS = 1e-5
N_GROUPS = 4
EXPERTS_PER_GROUP = 8
N_EXPERTS = N_GROUPS * EXPERTS_PER_GROUP
SC_WIDTH = 3
CF_WIDTH = 31
ROUTER_ROWS = 8 + N_EXPERTS
SUBLANES = 8
LANES = 128
ROW_ALIGN = 16
CONV_ROWS = 16
VMEM_LIMIT = 56 * 1024 * 1024
MOE_BLOCK = 256
TOK_TILE = 1024
LOCAL_ROWS = 2 * TOK_TILE + N_EXPERTS * ROW_ALIGN
POS_ROWS = TOK_TILE // LANES
RUN_START_ROW, RUN_CHUNKS_ROW, RUN_DEST_ROW = 2 * POS_ROWS, 2 * POS_ROWS + 1, 2 * POS_ROWS + 2
PAD_START_ROW, PAD_CHUNKS_ROW = 2 * POS_ROWS + 3, 2 * POS_ROWS + 4
TABLE_ROWS = 2 * POS_ROWS + SUBLANES
BLOCK_LANES = 256


def _dot(a, b):
    return jnp.dot(a, b, preferred_element_type=F32)


def _layer_norm(z, g, b):
    mu = jnp.mean(z, axis=-1, keepdims=True)
    zc = z - mu
    var = jnp.mean(zc * zc, axis=-1, keepdims=True)
    return zc * lax.rsqrt(var + LN_EPS) * g + b


def _sigmoid(x):
    return 1.0 / (1.0 + jnp.exp(-x))


def _route(h1b, rw_ref, rb_ref):
    t = h1b.shape[0]
    logits = lax.dot_general(rw_ref[...], h1b, (((1,), (1,)), ((), ())),
                             preferred_element_type=F32) + rb_ref[...]
    g = [logits[i:i + 1, :] for i in range(N_GROUPS)]
    gmax = jnp.maximum(jnp.maximum(g[0], g[1]), jnp.maximum(g[2], g[3]))
    gidx = jnp.where(g[0] == gmax, 0, jnp.where(g[1] == gmax, 1, jnp.where(g[2] == gmax, 2, 3)))
    gsum = (jnp.exp(g[0] - gmax) + jnp.exp(g[1] - gmax)) + (jnp.exp(g[2] - gmax) + jnp.exp(g[3] - gmax))
    gp = 1.0 / gsum
    sel = logits[8 + 8 * (N_GROUPS - 1):8 + 8 * N_GROUPS, :]
    for gi in range(N_GROUPS - 2, -1, -1):
        sel = jnp.where(gidx == gi, logits[8 + 8 * gi:16 + 8 * gi, :], sel)
    row = lax.broadcasted_iota(I32, (EXPERTS_PER_GROUP, t), 0)
    m1 = jnp.max(sel, axis=0, keepdims=True)
    i1 = jnp.min(jnp.where(sel == m1, row, EXPERTS_PER_GROUP), axis=0, keepdims=True)
    sel2 = jnp.where(row == i1, -jnp.inf, sel)
    m2 = jnp.max(sel2, axis=0, keepdims=True)
    i2 = jnp.min(jnp.where(sel2 == m2, row, EXPERTS_PER_GROUP), axis=0, keepdims=True)
    d = jnp.exp(m2 - m1)
    w1 = 1.0 / (1.0 + d)
    w2 = d / (1.0 + d)
    return gidx * EXPERTS_PER_GROUP + i1, gidx * EXPERTS_PER_GROUP + i2, gp * w1, gp * w2


def _post_mixer(x, y, alpha, g_ref, b_ref, rw_ref, rb_ref, h1_ref, eid_ref, gate_ref):
    h1 = _layer_norm(alpha * x + y, g_ref[...], b_ref[...])
    h1_ref[...] = h1
    e1, e2, g1, g2 = _route(h1.astype(BF16), rw_ref, rb_ref)
    row8 = lax.broadcasted_iota(I32, (SUBLANES, h1.shape[0]), 0)
    eid_ref[...] = jnp.where(row8 == 0, e1, jnp.where(row8 == 1, e2, -1))
    gate_ref[...] = jnp.where(row8 == 0, g1, jnp.where(row8 == 1, g2, 0.0))


def _load_history(ubuf, hist_ref, rows, hist_rows):
    j = pl.program_id(1)

    @pl.when(j == 0)
    def _():
        ubuf[0:hist_rows, :] = hist_ref[...]

    @pl.when(j > 0)
    def _():
        ubuf[0:hist_rows, :] = ubuf[rows:rows + hist_rows, :]


def _mixer_a_kernel(x_ref, hist_ref, win_ref, cw_ref, wout_ref, g_ref, b_ref, rw_ref, rb_ref,
                    h1_ref, eid_ref, gate_ref, tail_ref, ubuf, *, alpha, nseq):
    rows, d = x_ref.shape
    hist_rows = (SC_WIDTH - 1) * nseq
    _load_history(ubuf, hist_ref, rows, hist_rows)
    x = x_ref[...]
    xb = x.astype(BF16)
    ubuf[hist_rows:hist_rows + rows, :] = _dot(xb, win_ref[:, d:2 * d]) * _dot(xb, win_ref[:, 2 * d:3 * d])
    tail_ref[...] = ubuf[rows:rows + hist_rows, :]
    conv = cw_ref[0:1, :] * ubuf[0:rows, :]
    for k in range(1, SC_WIDTH):
        conv = conv + cw_ref[k:k + 1, :] * ubuf[k * nseq:k * nseq + rows, :]
    bg = _dot(xb, win_ref[:, 0:d])
    y = _dot((bg * conv).astype(BF16), wout_ref[...])
    _post_mixer(x, y, alpha, g_ref, b_ref, rw_ref, rb_ref, h1_ref, eid_ref, gate_ref)


def _mixer_b_kernel(x_ref, hist_ref, w1_ref, b1_ref, cw_ref, cb_ref, lg_ref, lb_ref, w2_ref, b2_ref,
                    g_ref, b_ref, rw_ref, rb_ref,
                    h1_ref, eid_ref, gate_ref, tail_ref, ubuf, cbuf, *, alpha, nseq):
    rows, d = x_ref.shape
    hist_rows = (CF_WIDTH - 1) * nseq
    _load_history(ubuf, hist_ref, rows, hist_rows)
    x = x_ref[...]
    xb = x.astype(BF16)
    a = _dot(xb, w1_ref[:, 0:d]) + b1_ref[:, 0:d]
    gl = _dot(xb, w1_ref[:, d:2 * d]) + b1_ref[:, d:2 * d]
    ubuf[hist_rows:hist_rows + rows, :] = a * _sigmoid(gl)
    tail_ref[...] = ubuf[rows:rows + hist_rows, :]
    halves = CONV_ROWS // SUBLANES

    def chunk(c, carry):
        r0 = pl.multiple_of(c * CONV_ROWS, CONV_ROWS)
        acc = [jnp.broadcast_to(cb_ref[...], (SUBLANES, d)) for _ in range(halves)]
        for k in range(CF_WIDTH):
            w = cw_ref[k * SUBLANES:(k + 1) * SUBLANES, :]
            for h in range(halves):
                start = pl.multiple_of(r0 + h * SUBLANES + k * nseq, SUBLANES)
                acc[h] = acc[h] + w * ubuf[pl.ds(start, SUBLANES), :]
        for h in range(halves):
            cbuf[pl.ds(pl.multiple_of(r0 + h * SUBLANES, SUBLANES), SUBLANES), :] = acc[h]
        return carry

    lax.fori_loop(0, rows // CONV_ROWS, chunk, 0)
    cn = _layer_norm(cbuf[...], lg_ref[...], lb_ref[...])
    cn = cn * _sigmoid(cn)
    y = _dot(cn.astype(BF16), w2_ref[...]) + b2_ref[...]
    _post_mixer(x, y, alpha, g_ref, b_ref, rw_ref, rb_ref, h1_ref, eid_ref, gate_ref)


def _const_spec(shape):
    nd = len(shape)
    return pl.BlockSpec(shape, lambda *_: (0,) * nd, pipeline_mode=pl.Buffered(1))


def _mixer(kind, x2d, hist2d, p, *, pos_tile, seq_tile, nseq, alpha):
    n, d = x2d.shape
    width = SC_WIDTH if kind == "a" else CF_WIDTH
    rows = pos_tile * seq_tile
    hist_rows = (width - 1) * seq_tile
    seq_blocks = nseq // seq_tile
    steps = n // (rows * seq_blocks)
    if kind == "a":
        kern = functools.partial(_mixer_a_kernel, alpha=alpha, nseq=seq_tile)
        weights = (p["w_in"], p["conv_w"], p["w_out"])
        wspecs = [_const_spec((d, 3 * d)), _const_spec((8, d)), _const_spec((d, d))]
        scratch = [pltpu.VMEM((rows + hist_rows, d), F32)]
    else:
        kern = functools.partial(_mixer_b_kernel, alpha=alpha, nseq=seq_tile)
        weights = (p["w_pw1"], p["b_pw1"], p["conv_w"], p["conv_b"], p["cln_g"], p["cln_b"], p["w_pw2"], p["b_pw2"])
        wspecs = [_const_spec((d, 2 * d)), _const_spec((1, 2 * d)), _const_spec((CF_WIDTH * SUBLANES, d)),
                  _const_spec((1, d)), _const_spec((1, d)), _const_spec((1, d)), _const_spec((d, d)),
                  _const_spec((1, d))]
        scratch = [pltpu.VMEM((rows + hist_rows, d), F32), pltpu.VMEM((rows, d), F32)]
    common = (p["ln1_g"], p["ln1_b"], p["rw"], p["rb"])
    cspecs = [_const_spec((1, d)), _const_spec((1, d)), _const_spec((ROUTER_ROWS, d)), _const_spec((ROUTER_ROWS, 1))]
    return pl.pallas_call(
        kern,
        grid=(seq_blocks, steps),
        in_specs=[
            pl.BlockSpec((rows, d), lambda s, j: (s * steps + j, 0)),
            pl.BlockSpec((hist_rows, d), lambda s, j: (s, 0)),
        ] + wspecs + cspecs,
        out_specs=[
            pl.BlockSpec((rows, d), lambda s, j: (s * steps + j, 0)),
            pl.BlockSpec((SUBLANES, rows), lambda s, j: (0, s * steps + j)),
            pl.BlockSpec((SUBLANES, rows), lambda s, j: (0, s * steps + j)),
            pl.BlockSpec((hist_rows, d), lambda s, j: (s, 0)),
        ],
        out_shape=[
            jax.ShapeDtypeStruct((n, d), F32),
            jax.ShapeDtypeStruct((SUBLANES, n), I32),
            jax.ShapeDtypeStruct((SUBLANES, n), F32),
            jax.ShapeDtypeStruct((seq_blocks * hist_rows, d), F32),
        ],
        scratch_shapes=scratch,
        compiler_params=pltpu.CompilerParams(
            dimension_semantics=("arbitrary", "arbitrary"), vmem_limit_bytes=VMEM_LIMIT),
        name="mixer_" + kind,
    )(x2d, hist2d, *weights, *common)


def _rank_kernel(eid_ref, tab_ref, blk_ref, tri, tot, prior, pad_start, pad_chunks, *, total_rows):
    phase, i = pl.program_id(0), pl.program_id(1)
    t = eid_ref.shape[1]
    row = lax.broadcasted_iota(I32, (N_EXPERTS, t), 0)
    oh1 = (row == eid_ref[0:1, :]).astype(F32)
    oh2 = (row == eid_ref[1:2, :]).astype(F32)
    c1 = jnp.sum(oh1, axis=1, keepdims=True)
    c2 = jnp.sum(oh2, axis=1, keepdims=True)
    chunks = jnp.ceil((c1 + c2) * (1.0 / ROW_ALIGN))
    tab_ref[...] = jnp.zeros_like(tab_ref)

    @pl.when(jnp.logical_and(phase == 0, i == 0))
    def _():
        tot[...] = jnp.zeros_like(tot)
        a = lax.broadcasted_iota(I32, tri.shape, 0)
        b = lax.broadcasted_iota(I32, tri.shape, 1)
        tri[...] = (a <= b).astype(BF16)

    @pl.when(phase == 0)
    def _():
        tot[...] += chunks * ROW_ALIGN

    @pl.when(jnp.logical_and(phase == 1, i == 0))
    def _():
        cnt = tot[...].astype(I32)
        seg = ((cnt + (MOE_BLOCK - 1)) // MOE_BLOCK) * MOE_BLOCK
        r128 = lax.broadcasted_iota(I32, cnt.shape, 0)
        base = jnp.zeros_like(cnt)
        for e in range(N_EXPERTS - 1):
            base = base + jnp.where(r128 > e, seg[e:e + 1, :], 0)
        prior[...] = base.astype(F32)
        pad_start[...] = (base + cnt).astype(F32)
        pad_end = jnp.where(r128 == N_EXPERTS - 1, total_rows, base + seg)
        pad_chunks[...] = ((pad_end - base - cnt) // ROW_ALIGN).astype(F32)
        end_blk = (base + seg) // MOE_BLOCK
        lane = lax.broadcasted_iota(I32, (N_EXPERTS, BLOCK_LANES), 1)
        expert_of = jnp.sum((end_blk[:, 0:1] <= lane).astype(F32), axis=0, keepdims=True).astype(I32)
        active = (lane[0:1, :] < end_blk[N_EXPERTS - 1:N_EXPERTS, 0:1]).astype(I32)
        r8 = lax.broadcasted_iota(I32, (SUBLANES, BLOCK_LANES), 0)
        blk_ref[...] = jnp.where(r8 == 0, jnp.minimum(expert_of, N_EXPERTS - 1), jnp.where(r8 == 1, active, 0))

    @pl.when(phase == 1)
    def _():
        cs = _dot(jnp.concatenate([oh1, oh2], axis=0).astype(BF16), tri[...])
        ea = lax.broadcasted_iota(I32, (N_EXPERTS, N_EXPERTS), 0)
        eb = lax.broadcasted_iota(I32, (N_EXPERTS, N_EXPERTS), 1)
        chunks_b = jnp.broadcast_to(chunks, (N_EXPERTS, LANES))
        start = _dot((eb < ea).astype(BF16), chunks_b.astype(BF16)) * ROW_ALIGN
        s0 = start[:, 0:1]
        pos1 = jnp.sum(oh1 * (s0 + cs[0:N_EXPERTS, :] - 1.0), axis=0, keepdims=True).astype(I32)
        pos2 = jnp.sum(oh2 * (s0 + c1 + cs[N_EXPERTS:2 * N_EXPERTS, :] - 1.0), axis=0, keepdims=True).astype(I32)
        for q in range(POS_ROWS):
            tab_ref[q:q + 1, :] = pos1[:, q * LANES:(q + 1) * LANES]
            tab_ref[POS_ROWS + q:POS_ROWS + q + 1, :] = pos2[:, q * LANES:(q + 1) * LANES]
        diag = lax.broadcasted_iota(I32, (N_EXPERTS, LANES), 0) == lax.broadcasted_iota(I32, (N_EXPERTS, LANES), 1)
        to_lanes = lambda col: jnp.sum(jnp.where(diag, col, 0.0), axis=0, keepdims=True).astype(I32)
        tab_ref[RUN_START_ROW:RUN_START_ROW + 1, :] = to_lanes(start)
        tab_ref[RUN_CHUNKS_ROW:RUN_CHUNKS_ROW + 1, :] = to_lanes(chunks_b)
        tab_ref[RUN_DEST_ROW:RUN_DEST_ROW + 1, :] = to_lanes(prior[...])
        tab_ref[PAD_START_ROW:PAD_START_ROW + 1, :] = to_lanes(pad_start[...])
        tab_ref[PAD_CHUNKS_ROW:PAD_CHUNKS_ROW + 1, :] = to_lanes(pad_chunks[...])
        prior[...] += chunks * ROW_ALIGN


def _rank(eid, total_rows):
    n = eid.shape[1]
    tiles = n // TOK_TILE
    return pl.pallas_call(
        functools.partial(_rank_kernel, total_rows=total_rows),
        grid=(2, tiles),
        in_specs=[pl.BlockSpec((SUBLANES, TOK_TILE), lambda ph, i: (0, i))],
        out_specs=[
            pl.BlockSpec((None, TABLE_ROWS, LANES), lambda ph, i: (i * ph, 0, 0)),
            pl.BlockSpec((SUBLANES, BLOCK_LANES), lambda ph, i: (0, 0)),
        ],
        out_shape=[
            jax.ShapeDtypeStruct((tiles, TABLE_ROWS, LANES), I32),
            jax.ShapeDtypeStruct((SUBLANES, BLOCK_LANES), I32),
        ],
        scratch_shapes=[pltpu.VMEM((TOK_TILE, TOK_TILE), BF16)] + [pltpu.VMEM((N_EXPERTS, LANES), F32)] * 4,
        compiler_params=pltpu.CompilerParams(
            dimension_semantics=("arbitrary", "arbitrary"), vmem_limit_bytes=VMEM_LIMIT),
        name="rank",
    )(eid)


def _token_rows(ref, tok):
    return ref.at[pl.ds(pl.multiple_of(tok * SUBLANES, SUBLANES), SUBLANES), :]


def _pair_rows(tab, n):
    return tab[n >> 7, n & (LANES - 1)], tab[POS_ROWS + (n >> 7), n & (LANES - 1)]


def _for_each_chunk(tab, fn, rows=(RUN_START_ROW, RUN_CHUNKS_ROW, RUN_DEST_ROW)):
    start_row, chunks_row, dest_row = rows

    def per_expert(e, carry):
        start, dest = tab[start_row, e], tab[dest_row, e]

        def per_chunk(q, c2):
            fn(pl.multiple_of(start + q * ROW_ALIGN, ROW_ALIGN), pl.multiple_of(dest + q * ROW_ALIGN, ROW_ALIGN))
            return c2

        lax.fori_loop(0, tab[chunks_row, e], per_chunk, 0)
        return carry

    lax.fori_loop(0, N_EXPERTS, per_expert, 0)


def _dispatch_kernel(*refs, tile_base, aliased, fill_pads):
    if aliased:
        tab_hbm, h_ref, _, xs_hbm, tab, tokbuf, loc, xl, zrows, sem_tab, sem_row = refs
    else:
        tab_hbm, h_ref, xs_hbm, tab, tokbuf, loc, xl, zrows, sem_tab, sem_row = refs
    t = h_ref.shape[0]
    cp = pltpu.make_async_copy(tab_hbm.at[tile_base + pl.program_id(0)], tab, sem_tab)
    cp.start()
    for c in range(SUBLANES):
        tokbuf[pl.ds(c, t, stride=SUBLANES), :] = h_ref[:, c * LANES:(c + 1) * LANES]
    loc[...] = jnp.zeros_like(loc)
    cp.wait()

    def place(n, carry):
        l1, l2 = _pair_rows(tab, n)
        v = _token_rows(tokbuf, n)[...]
        _token_rows(loc, l1)[...] = v
        _token_rows(loc, l2)[...] = v
        return carry

    lax.fori_loop(0, t, place, 0)
    for c in range(SUBLANES):
        xl[:, c * LANES:(c + 1) * LANES] = loc[pl.ds(c, LOCAL_ROWS, stride=SUBLANES), :].astype(BF16)

    def chunk_copy(local_row, global_row):
        return pltpu.make_async_copy(xl.at[pl.ds(local_row, ROW_ALIGN), :],
                                     xs_hbm.at[pl.ds(global_row, ROW_ALIGN), :], sem_row)

    _for_each_chunk(tab, lambda l, g: chunk_copy(l, g).start())
    _for_each_chunk(tab, lambda l, g: chunk_copy(l, g).wait())

    if fill_pads:
        @pl.when(pl.program_id(0) == pl.num_programs(0) - 1)
        def _():
            zrows[...] = jnp.zeros_like(zrows)
            zero_copy = lambda _, g: pltpu.make_async_copy(zrows, xs_hbm.at[pl.ds(g, ROW_ALIGN), :], sem_row)
            pad_rows = (PAD_START_ROW, PAD_CHUNKS_ROW, PAD_START_ROW)
            _for_each_chunk(tab, lambda l, g: zero_copy(l, g).start(), pad_rows)
            _for_each_chunk(tab, lambda l, g: zero_copy(l, g).wait(), pad_rows)


def _dispatch(tabs, h2d, xs, *, tile, tile_base, sorted_rows, fill_pads):
    n, d = h2d.shape
    aliased = xs is not None
    kern = functools.partial(_dispatch_kernel, tile_base=tile_base, aliased=aliased, fill_pads=fill_pads)
    in_specs = [pl.BlockSpec(memory_space=pl.ANY), pl.BlockSpec((tile, d), lambda i: (i, 0))]
    args = [tabs, h2d]
    if aliased:
        in_specs.append(pl.BlockSpec(memory_space=pl.ANY))
        args.append(xs)
    return pl.pallas_call(
        kern,
        grid=(n // tile,),
        in_specs=in_specs,
        out_specs=pl.BlockSpec(memory_space=pl.ANY),
        out_shape=jax.ShapeDtypeStruct((sorted_rows, d), BF16),
        scratch_shapes=[pltpu.SMEM((TABLE_ROWS, LANES), I32), pltpu.VMEM((tile * SUBLANES, LANES), F32),
                        pltpu.VMEM((LOCAL_ROWS * SUBLANES, LANES), F32), pltpu.VMEM((LOCAL_ROWS, d), BF16),
                        pltpu.VMEM((ROW_ALIGN, d), BF16), pltpu.SemaphoreType.DMA, pltpu.SemaphoreType.DMA],
        input_output_aliases={2: 0} if aliased else {},
        compiler_params=pltpu.CompilerParams(
            dimension_semantics=("arbitrary",), vmem_limit_bytes=VMEM_LIMIT),
        name="dispatch",
    )(*args)


def _combine_kernel(tab_hbm, gate_hbm, ys_hbm, h_ref, g_ref, b_ref, o_ref,
                    tab, gates, yl, ytok, ztok, sem_tab, sem_gate, sem_row, *, tile_base, alpha):
    t = h_ref.shape[0]
    i = pl.program_id(0)
    cp_tab = pltpu.make_async_copy(tab_hbm.at[tile_base + i], tab, sem_tab)
    cp_gate = pltpu.make_async_copy(gate_hbm.at[:, pl.ds(pl.multiple_of(i * t, LANES), t)], gates, sem_gate)
    cp_tab.start()
    cp_gate.start()
    yl[...] = jnp.zeros_like(yl)
    cp_tab.wait()

    def chunk_copy(local_row, global_row):
        return pltpu.make_async_copy(ys_hbm.at[pl.ds(global_row, ROW_ALIGN), :],
                                     yl.at[pl.ds(local_row, ROW_ALIGN), :], sem_row)

    _for_each_chunk(tab, lambda l, g: chunk_copy(l, g).start())
    _for_each_chunk(tab, lambda l, g: chunk_copy(l, g).wait())
    for c in range(SUBLANES):
        ytok[pl.ds(c, LOCAL_ROWS, stride=SUBLANES), :] = yl[:, c * LANES:(c + 1) * LANES].astype(F32)
    cp_gate.wait()

    def gather(n, carry):
        l1, l2 = _pair_rows(tab, n)
        _token_rows(ztok, n)[...] = gates[0, n] * _token_rows(ytok, l1)[...] + gates[1, n] * _token_rows(ytok, l2)[...]
        return carry

    lax.fori_loop(0, t, gather, 0)
    f = jnp.concatenate([ztok[pl.ds(c, t, stride=SUBLANES), :] for c in range(SUBLANES)], axis=1)
    o_ref[...] = _layer_norm(alpha * h_ref[...] + f, g_ref[...], b_ref[...])


def _combine(tabs, gate, ys, h2d, p, *, tile, tile_base, alpha):
    n, d = h2d.shape
    kern = functools.partial(_combine_kernel, tile_base=tile_base, alpha=alpha)
    return pl.pallas_call(
        kern,
        grid=(n // tile,),
        in_specs=[
            pl.BlockSpec(memory_space=pl.ANY), pl.BlockSpec(memory_space=pl.ANY), pl.BlockSpec(memory_space=pl.ANY),
            pl.BlockSpec((tile, d), lambda i: (i, 0)),
            _const_spec((1, d)), _const_spec((1, d)),
        ],
        out_specs=pl.BlockSpec((tile, d), lambda i: (i, 0)),
        out_shape=jax.ShapeDtypeStruct((n, d), F32),
        scratch_shapes=[pltpu.SMEM((TABLE_ROWS, LANES), I32), pltpu.SMEM((SUBLANES, tile), F32),
                        pltpu.VMEM((LOCAL_ROWS, d), BF16), pltpu.VMEM((LOCAL_ROWS * SUBLANES, LANES), F32),
                        pltpu.VMEM((tile * SUBLANES, LANES), F32),
                        pltpu.SemaphoreType.DMA, pltpu.SemaphoreType.DMA, pltpu.SemaphoreType.DMA],
        compiler_params=pltpu.CompilerParams(
            dimension_semantics=("arbitrary",), vmem_limit_bytes=VMEM_LIMIT),
        name="combine",
    )(tabs, gate, ys, h2d, p["ln2_g"], p["ln2_b"])


def _moe_kernel(blk_ref, x_ref, wg_ref, wu_ref, wd_ref, y_ref):
    b = pl.program_id(0)

    @pl.when(blk_ref[1, b] == 1)
    def _():
        x = x_ref[...]
        hg = _dot(x, wg_ref[...].astype(BF16))
        hu = _dot(x, wu_ref[...].astype(BF16))
        hid = (hg * _sigmoid(hg) * hu).astype(BF16)
        y_ref[...] = _dot(hid, wd_ref[...].astype(BF16)).astype(BF16)

    @pl.when(blk_ref[1, b] == 0)
    def _():
        y_ref[...] = jnp.zeros_like(y_ref)


def _moe(blk, xs, wg, wu, wd, layer):
    rows, d = xs.shape
    nblocks = rows // MOE_BLOCK
    f = wg.shape[-1]
    block = lambda b, blk: (jnp.where(blk[1, b] == 1, b, nblocks - 1), 0)
    return pl.pallas_call(
        _moe_kernel,
        grid_spec=pltpu.PrefetchScalarGridSpec(
            num_scalar_prefetch=1,
            grid=(nblocks,),
            in_specs=[
                pl.BlockSpec((MOE_BLOCK, d), block),
                pl.BlockSpec((None, None, d, f), lambda b, blk: (layer, blk[0, b], 0, 0)),
                pl.BlockSpec((None, None, d, f), lambda b, blk: (layer, blk[0, b], 0, 0)),
                pl.BlockSpec((None, None, f, d), lambda b, blk: (layer, blk[0, b], 0, 0)),
            ],
            out_specs=pl.BlockSpec((MOE_BLOCK, d), lambda b, blk: (b, 0)),
        ),
        out_shape=jax.ShapeDtypeStruct((rows, d), BF16),
        compiler_params=pltpu.CompilerParams(
            dimension_semantics=("arbitrary",), vmem_limit_bytes=VMEM_LIMIT),
        name="moe",
    )(blk, xs, wg, wu, wd)


def _sparse_moe(sets, p, *, layer, alpha):
    eids, bases, tiles = [], [], 0
    for h, e, _ in sets:
        n = h.shape[0]
        n_pad = -(-n // TOK_TILE) * TOK_TILE
        eids.append(jnp.concatenate([e, jnp.full((SUBLANES, n_pad - n), -1, I32)], axis=1))
        bases.append(tiles)
        tiles += n_pad // TOK_TILE
    n_tok = sum(h.shape[0] for h, _, _ in sets)
    run_pad = tiles * N_EXPERTS * (ROW_ALIGN - 1)
    nblocks = -(-(2 * n_tok + run_pad + N_EXPERTS * (MOE_BLOCK - 1)) // MOE_BLOCK)
    assert nblocks <= BLOCK_LANES
    sorted_rows = nblocks * MOE_BLOCK
    tabs, blk = _rank(jnp.concatenate(eids, axis=1), sorted_rows)
    xs = None
    for k, ((h, _, _), tb) in enumerate(zip(sets, bases)):
        xs = _dispatch(tabs, h, xs, tile=min(TOK_TILE, h.shape[0]), tile_base=tb, sorted_rows=sorted_rows,
                       fill_pads=k == len(sets) - 1)
    ys = _moe(blk, xs, p["w_gate"], p["w_up"], p["w_down"], layer)
    return [_combine(tabs, g, ys, h, p, tile=min(TOK_TILE, h.shape[0]), tile_base=tb, alpha=alpha)
            for (h, _, g), tb in zip(sets, bases)]


def _pad_rows(w, rows):
    return jnp.concatenate([w, jnp.zeros((rows - w.shape[0],) + w.shape[1:], w.dtype)], axis=0)


def _router_params(gw, gb, ew, eb):
    d = gw.shape[0]
    rw = jnp.concatenate([gw.T, jnp.zeros((8 - N_GROUPS, d), gw.dtype), ew.T], axis=0).astype(BF16)
    rb = jnp.concatenate([gb, jnp.zeros((8 - N_GROUPS,), gb.dtype), eb])[:, None].astype(F32)
    return rw, rb


def _to_blocks(x, seq_tile):
    nseq, npos, d = x.shape
    return x.reshape(nseq // seq_tile, seq_tile, npos, d).swapaxes(1, 2).reshape(nseq * npos, d)


def _from_blocks(x2d, nseq, seq_tile):
    n, d = x2d.shape
    npos = n // nseq
    return x2d.reshape(nseq // seq_tile, npos, seq_tile, d).swapaxes(1, 2).reshape(nseq, npos, d)


def kernel(x_prompt, x_sample, state_conv_a, state_conv_b, meta_tokens, sc_w_in, sc_conv_w, sc_w_out, cf_w_pw1, cf_b_pw1, cf_conv_w, cf_conv_b, cf_ln_g, cf_ln_b, cf_w_pw2, cf_b_pw2, ln1_g, ln1_b, ln2_g, ln2_b, rt_group_w, rt_group_b, rt_expert_w, rt_expert_b, moe_w_gate, moe_w_up, moe_w_down):
    bsz, seq, d = x_prompt.shape
    dec_b, dec_t, _ = x_sample.shape
    n_meta = meta_tokens.shape[0]
    depth = ln1_g.shape[0]
    alpha = float((2 * depth) ** 0.25)
    row = lambda v: v[None, :].astype(F32)
    s_seq = 32
    p_pos = 512 // bsz

    layer = []
    for i in range(depth):
        rw, rb = _router_params(rt_group_w[i], rt_group_b[i], rt_expert_w[i], rt_expert_b[i])
        layer.append(dict(
            ln1_g=row(ln1_g[i]), ln1_b=row(ln1_b[i]), ln2_g=row(ln2_g[i]), ln2_b=row(ln2_b[i]), rw=rw, rb=rb,
            w_gate=moe_w_gate, w_up=moe_w_up, w_down=moe_w_down))
    pa = dict(layer[0], w_in=sc_w_in[0].astype(BF16), conv_w=_pad_rows(sc_conv_w[0], 8),
              w_out=sc_w_out[0].astype(BF16))
    pb = dict(layer[1], w_pw1=cf_w_pw1[0].astype(BF16), b_pw1=row(cf_b_pw1[0]),
              conv_w=jnp.repeat(cf_conv_w[0].astype(F32), SUBLANES, axis=0), conv_b=row(cf_conv_b[0]),
              cln_g=row(cf_ln_g[0]), cln_b=row(cf_ln_b[0]), w_pw2=cf_w_pw2[0].astype(BF16), b_pw2=row(cf_b_pw2[0]))

    xp = _to_blocks(x_prompt, bsz)
    xs = _to_blocks(x_sample, s_seq)
    xm = jnp.broadcast_to(meta_tokens.astype(F32)[:, None, :], (n_meta, bsz, d)).reshape(n_meta * bsz, d)
    mix_p = dict(pos_tile=p_pos, seq_tile=bsz, nseq=bsz, alpha=alpha)
    mix_s = dict(pos_tile=dec_t, seq_tile=s_seq, nseq=dec_b, alpha=alpha)
    mix_m = dict(pos_tile=n_meta, seq_tile=bsz, nseq=bsz, alpha=alpha)

    zero_a = jnp.zeros(((SC_WIDTH - 1) * bsz, d), F32)
    hm, em, gm, tail_am = _mixer("a", xm, zero_a, pa, **mix_m)
    hp, ep, gp, tail_ap = _mixer("a", xp, tail_am, pa, **mix_p)
    hs, es, gs, tail_as = _mixer("a", xs, _to_blocks(state_conv_a[0], s_seq), pa, **mix_s)
    hp, hs, hm = _sparse_moe([(hp, ep, gp), (hs, es, gs), (hm, em, gm)], pa, layer=0, alpha=alpha)

    zero_b = jnp.zeros(((CF_WIDTH - 1) * bsz, d), F32)
    _, _, _, tail_bm = _mixer("b", hm, zero_b, pb, **mix_m)
    hp, ep, gp, tail_bp = _mixer("b", hp, tail_bm, pb, **mix_p)
    hs, es, gs, tail_bs = _mixer("b", hs, _to_blocks(state_conv_b[0], s_seq), pb, **mix_s)
    hp, hs = _sparse_moe([(hp, ep, gp), (hs, es, gs)], pb, layer=1, alpha=alpha)

    return (_from_blocks(hp, bsz, bsz), _from_blocks(hs, dec_b, s_seq),
            _from_blocks(tail_ap, bsz, bsz)[None], _from_blocks(tail_bp, bsz, bsz)[None],
            _from_blocks(tail_as, dec_b, s_seq)[None], _from_blocks(tail_bs, dec_b, s_seq)[None])
```

```python
import functools

import jax
import jax.numpy as jnp
from jax import lax
from jax.experimental import pallas as pl
from jax.experimental.pallas import tpu as pltpu

F32 = jnp.float32
BF16 = jnp.bfloat16
I32 = jnp.int32

LN_EPS = 1e-5
N_GROUPS = 4
EXPERTS_PER_GROUP = 8
N_EXPERTS = N_GROUPS * EXPERTS_PER_GROUP
SC_WIDTH = 3
CF_WIDTH = 31
ROUTER_ROWS = 8 + N_EXPERTS
SUBLANES = 8
LANES = 128
ROW_ALIGN = 16
BIG_CHUNK_ROWS = 64
CONV_ROWS = 16
VMEM_LIMIT = 56 * 1024 * 1024
MOE_BLOCK = 256
TOK_TILE = 1024
LOCAL_ROWS = 2 * TOK_TILE + N_EXPERTS * ROW_ALIGN
POS_ROWS = TOK_TILE // LANES
RUN_START_ROW, RUN_CHUNKS_ROW, RUN_DEST_ROW = 2 * POS_ROWS, 2 * POS_ROWS + 1, 2 * POS_ROWS + 2
PAD_START_ROW, PAD_CHUNKS_ROW = 2 * POS_ROWS + 3, 2 * POS_ROWS + 4
TABLE_ROWS = 2 * POS_ROWS + SUBLANES
BLOCK_LANES = 256


def _dot(a, b):
    return jnp.dot(a, b, preferred_element_type=F32)


def _layer_norm(z, g, b):
    mu = jnp.mean(z, axis=-1, keepdims=True)
    zc = z - mu
    var = jnp.mean(zc * zc, axis=-1, keepdims=True)
    return zc * lax.rsqrt(var + LN_EPS) * g + b


def _sigmoid(x):
    return 1.0 / (1.0 + jnp.exp(-x))


def _route(h1b, rw_ref, rb_ref):
    t = h1b.shape[0]
    logits = lax.dot_general(rw_ref[...], h1b, (((1,), (1,)), ((), ())),
                             preferred_element_type=F32) + rb_ref[...]
    g = [logits[i:i + 1, :] for i in range(N_GROUPS)]
    gmax = jnp.maximum(jnp.maximum(g[0], g[1]), jnp.maximum(g[2], g[3]))
    gidx = jnp.where(g[0] == gmax, 0, jnp.where(g[1] == gmax, 1, jnp.where(g[2] == gmax, 2, 3)))
    gsum = (jnp.exp(g[0] - gmax) + jnp.exp(g[1] - gmax)) + (jnp.exp(g[2] - gmax) + jnp.exp(g[3] - gmax))
    gp = 1.0 / gsum
    sel = logits[8 + 8 * (N_GROUPS - 1):8 + 8 * N_GROUPS, :]
    for gi in range(N_GROUPS - 2, -1, -1):
        sel = jnp.where(gidx == gi, logits[8 + 8 * gi:16 + 8 * gi, :], sel)
    row = lax.broadcasted_iota(I32, (EXPERTS_PER_GROUP, t), 0)
    m1 = jnp.max(sel, axis=0, keepdims=True)
    i1 = jnp.min(jnp.where(sel == m1, row, EXPERTS_PER_GROUP), axis=0, keepdims=True)
    sel2 = jnp.where(row == i1, -jnp.inf, sel)
    m2 = jnp.max(sel2, axis=0, keepdims=True)
    i2 = jnp.min(jnp.where(sel2 == m2, row, EXPERTS_PER_GROUP), axis=0, keepdims=True)
    d = jnp.exp(m2 - m1)
    w1 = 1.0 / (1.0 + d)
    w2 = d / (1.0 + d)
    return gidx * EXPERTS_PER_GROUP + i1, gidx * EXPERTS_PER_GROUP + i2, gp * w1, gp * w2


def _post_mixer(x, y, alpha, g_ref, b_ref, rw_ref, rb_ref, h1_ref, eid_ref, gate_ref):
    h1 = _layer_norm(alpha * x + y, g_ref[...], b_ref[...])
    h1_ref[...] = h1
    e1, e2, g1, g2 = _route(h1.astype(BF16), rw_ref, rb_ref)
    row8 = lax.broadcasted_iota(I32, (SUBLANES, h1.shape[0]), 0)
    eid_ref[...] = jnp.where(row8 == 0, e1, jnp.where(row8 == 1, e2, -1))
    gate_ref[...] = jnp.where(row8 == 0, g1, jnp.where(row8 == 1, g2, 0.0))


def _load_history(ubuf, hist_ref, rows, hist_rows):
    j = pl.program_id(1)

    @pl.when(j == 0)
    def _():
        ubuf[0:hist_rows, :] = hist_ref[...]

    @pl.when(j > 0)
    def _():
        ubuf[0:hist_rows, :] = ubuf[rows:rows + hist_rows, :]


def _mixer_a_kernel(x_ref, hist_ref, win_ref, cw_ref, wout_ref, g_ref, b_ref, rw_ref, rb_ref,
                    h1_ref, eid_ref, gate_ref, tail_ref, ubuf, *, alpha, nseq):
    rows, d = x_ref.shape
    hist_rows = (SC_WIDTH - 1) * nseq
    _load_history(ubuf, hist_ref, rows, hist_rows)
    x = x_ref[...]
    xb = x.astype(BF16)
    ubuf[hist_rows:hist_rows + rows, :] = _dot(xb, win_ref[:, d:2 * d]) * _dot(xb, win_ref[:, 2 * d:3 * d])
    tail_ref[...] = ubuf[rows:rows + hist_rows, :]
    conv = cw_ref[0:1, :] * ubuf[0:rows, :]
    for k in range(1, SC_WIDTH):
        conv = conv + cw_ref[k:k + 1, :] * ubuf[k * nseq:k * nseq + rows, :]
    bg = _dot(xb, win_ref[:, 0:d])
    y = _dot((bg * conv).astype(BF16), wout_ref[...])
    _post_mixer(x, y, alpha, g_ref, b_ref, rw_ref, rb_ref, h1_ref, eid_ref, gate_ref)


def _mixer_b_kernel(x_ref, hist_ref, w1_ref, b1_ref, cw_ref, cb_ref, lg_ref, lb_ref, w2_ref, b2_ref,
                    g_ref, b_ref, rw_ref, rb_ref,
                    h1_ref, eid_ref, gate_ref, tail_ref, ubuf, cbuf, *, alpha, nseq):
    rows, d = x_ref.shape
    hist_rows = (CF_WIDTH - 1) * nseq
    _load_history(ubuf, hist_ref, rows, hist_rows)
    x = x_ref[...]
    xb = x.astype(BF16)
    a = _dot(xb, w1_ref[:, 0:d]) + b1_ref[:, 0:d]
    gl = _dot(xb, w1_ref[:, d:2 * d]) + b1_ref[:, d:2 * d]
    ubuf[hist_rows:hist_rows + rows, :] = a * _sigmoid(gl)
    tail_ref[...] = ubuf[rows:rows + hist_rows, :]
    halves = CONV_ROWS // SUBLANES

    def chunk(c, carry):
        r0 = pl.multiple_of(c * CONV_ROWS, CONV_ROWS)
        acc = [jnp.broadcast_to(cb_ref[...], (SUBLANES, d)) for _ in range(halves)]
        for k in range(CF_WIDTH):
            w = cw_ref[k * SUBLANES:(k + 1) * SUBLANES, :]
            for h in range(halves):
                start = pl.multiple_of(r0 + h * SUBLANES + k * nseq, SUBLANES)
                acc[h] = acc[h] + w * ubuf[pl.ds(start, SUBLANES), :]
        for h in range(halves):
            cbuf[pl.ds(pl.multiple_of(r0 + h * SUBLANES, SUBLANES), SUBLANES), :] = acc[h]
        return carry

    lax.fori_loop(0, rows // CONV_ROWS, chunk, 0)
    cn = _layer_norm(cbuf[...], lg_ref[...], lb_ref[...])
    cn = cn * _sigmoid(cn)
    y = _dot(cn.astype(BF16), w2_ref[...]) + b2_ref[...]
    _post_mixer(x, y, alpha, g_ref, b_ref, rw_ref, rb_ref, h1_ref, eid_ref, gate_ref)


def _const_spec(shape):
    nd = len(shape)
    return pl.BlockSpec(shape, lambda *_: (0,) * nd, pipeline_mode=pl.Buffered(1))


def _mixer(kind, x2d, hist2d, p, *, pos_tile, seq_tile, nseq, alpha):
    n, d = x2d.shape
    width = SC_WIDTH if kind == "a" else CF_WIDTH
    rows = pos_tile * seq_tile
    hist_rows = (width - 1) * seq_tile
    seq_blocks = nseq // seq_tile
    steps = n // (rows * seq_blocks)
    if kind == "a":
        kern = functools.partial(_mixer_a_kernel, alpha=alpha, nseq=seq_tile)
        weights = (p["w_in"], p["conv_w"], p["w_out"])
        wspecs = [_const_spec((d, 3 * d)), _const_spec((8, d)), _const_spec((d, d))]
        scratch = [pltpu.VMEM((rows + hist_rows, d), F32)]
    else:
        kern = functools.partial(_mixer_b_kernel, alpha=alpha, nseq=seq_tile)
        weights = (p["w_pw1"], p["b_pw1"], p["conv_w"], p["conv_b"], p["cln_g"], p["cln_b"], p["w_pw2"], p["b_pw2"])
        wspecs = [_const_spec((d, 2 * d)), _const_spec((1, 2 * d)), _const_spec((CF_WIDTH * SUBLANES, d)),
                  _const_spec((1, d)), _const_spec((1, d)), _const_spec((1, d)), _const_spec((d, d)),
                  _const_spec((1, d))]
        scratch = [pltpu.VMEM((rows + hist_rows, d), F32), pltpu.VMEM((rows, d), F32)]
    common = (p["ln1_g"], p["ln1_b"], p["rw"], p["rb"])
    cspecs = [_const_spec((1, d)), _const_spec((1, d)), _const_spec((ROUTER_ROWS, d)), _const_spec((ROUTER_ROWS, 1))]
    return pl.pallas_call(
        kern,
        grid=(seq_blocks, steps),
        in_specs=[
            pl.BlockSpec((rows, d), lambda s, j: (s * steps + j, 0)),
            pl.BlockSpec((hist_rows, d), lambda s, j: (s, 0)),
        ] + wspecs + cspecs,
        out_specs=[
            pl.BlockSpec((rows, d), lambda s, j: (s * steps + j, 0)),
            pl.BlockSpec((SUBLANES, rows), lambda s, j: (0, s * steps + j)),
            pl.BlockSpec((SUBLANES, rows), lambda s, j: (0, s * steps + j)),
            pl.BlockSpec((hist_rows, d), lambda s, j: (s, 0)),
        ],
        out_shape=[
            jax.ShapeDtypeStruct((n, d), F32),
            jax.ShapeDtypeStruct((SUBLANES, n), I32),
            jax.ShapeDtypeStruct((SUBLANES, n), F32),
            jax.ShapeDtypeStruct((seq_blocks * hist_rows, d), F32),
        ],
        scratch_shapes=scratch,
        compiler_params=pltpu.CompilerParams(
            dimension_semantics=("arbitrary", "arbitrary"), vmem_limit_bytes=VMEM_LIMIT),
        name="mixer_" + kind,
    )(x2d, hist2d, *weights, *common)


def _rank_kernel(eid_ref, tab_ref, blk_ref, tri, tot, prior, pad_start, pad_chunks, *, total_rows):
    phase, i = pl.program_id(0), pl.program_id(1)
    t = eid_ref.shape[1]
    row = lax.broadcasted_iota(I32, (N_EXPERTS, t), 0)
    oh1 = (row == eid_ref[0:1, :]).astype(F32)
    oh2 = (row == eid_ref[1:2, :]).astype(F32)
    c1 = jnp.sum(oh1, axis=1, keepdims=True)
    c2 = jnp.sum(oh2, axis=1, keepdims=True)
    chunks = jnp.ceil((c1 + c2) * (1.0 / ROW_ALIGN))
    tab_ref[...] = jnp.zeros_like(tab_ref)

    @pl.when(jnp.logical_and(phase == 0, i == 0))
    def _():
        tot[...] = jnp.zeros_like(tot)
        a = lax.broadcasted_iota(I32, tri.shape, 0)
        b = lax.broadcasted_iota(I32, tri.shape, 1)
        tri[...] = (a <= b).astype(BF16)

    @pl.when(phase == 0)
    def _():
        tot[...] += chunks * ROW_ALIGN

    @pl.when(jnp.logical_and(phase == 1, i == 0))
    def _():
        cnt = tot[...].astype(I32)
        seg = ((cnt + (MOE_BLOCK - 1)) // MOE_BLOCK) * MOE_BLOCK
        r128 = lax.broadcasted_iota(I32, cnt.shape, 0)
        base = jnp.zeros_like(cnt)
        for e in range(N_EXPERTS - 1):
            base = base + jnp.where(r128 > e, seg[e:e + 1, :], 0)
        prior[...] = base.astype(F32)
        pad_start[...] = (base + cnt).astype(F32)
        pad_end = jnp.where(r128 == N_EXPERTS - 1, total_rows, base + seg)
        pad_chunks[...] = ((pad_end - base - cnt) // ROW_ALIGN).astype(F32)
        end_blk = (base + seg) // MOE_BLOCK
        lane = lax.broadcasted_iota(I32, (N_EXPERTS, BLOCK_LANES), 1)
        expert_of = jnp.sum((end_blk[:, 0:1] <= lane).astype(F32), axis=0, keepdims=True).astype(I32)
        active = (lane[0:1, :] < end_blk[N_EXPERTS - 1:N_EXPERTS, 0:1]).astype(I32)
        r8 = lax.broadcasted_iota(I32, (SUBLANES, BLOCK_LANES), 0)
        blk_ref[...] = jnp.where(r8 == 0, jnp.minimum(expert_of, N_EXPERTS - 1), jnp.where(r8 == 1, active, 0))

    @pl.when(phase == 1)
    def _():
        cs = _dot(jnp.concatenate([oh1, oh2], axis=0).astype(BF16), tri[...])
        ea = lax.broadcasted_iota(I32, (N_EXPERTS, N_EXPERTS), 0)
        eb = lax.broadcasted_iota(I32, (N_EXPERTS, N_EXPERTS), 1)
        chunks_b = jnp.broadcast_to(chunks, (N_EXPERTS, LANES))
        start = _dot((eb < ea).astype(BF16), chunks_b.astype(BF16)) * ROW_ALIGN
        s0 = start[:, 0:1]
        pos1 = jnp.sum(oh1 * (s0 + cs[0:N_EXPERTS, :] - 1.0), axis=0, keepdims=True).astype(I32)
        pos2 = jnp.sum(oh2 * (s0 + c1 + cs[N_EXPERTS:2 * N_EXPERTS, :] - 1.0), axis=0, keepdims=True).astype(I32)
        for q in range(POS_ROWS):
            tab_ref[q:q + 1, :] = pos1[:, q * LANES:(q + 1) * LANES]
            tab_ref[POS_ROWS + q:POS_ROWS + q + 1, :] = pos2[:, q * LANES:(q + 1) * LANES]
        diag = lax.broadcasted_iota(I32, (N_EXPERTS, LANES), 0) == lax.broadcasted_iota(I32, (N_EXPERTS, LANES), 1)
        to_lanes = lambda col: jnp.sum(jnp.where(diag, col, 0.0), axis=0, keepdims=True).astype(I32)
        tab_ref[RUN_START_ROW:RUN_START_ROW + 1, :] = to_lanes(start)
        tab_ref[RUN_CHUNKS_ROW:RUN_CHUNKS_ROW + 1, :] = to_lanes(chunks_b)
        tab_ref[RUN_DEST_ROW:RUN_DEST_ROW + 1, :] = to_lanes(prior[...])
        tab_ref[PAD_START_ROW:PAD_START_ROW + 1, :] = to_lanes(pad_start[...])
        tab_ref[PAD_CHUNKS_ROW:PAD_CHUNKS_ROW + 1, :] = to_lanes(pad_chunks[...])
        prior[...] += chunks * ROW_ALIGN


def _rank(eid, total_rows):
    n = eid.shape[1]
    tiles = n // TOK_TILE
    return pl.pallas_call(
        functools.partial(_rank_kernel, total_rows=total_rows),
        grid=(2, tiles),
        in_specs=[pl.BlockSpec((SUBLANES, TOK_TILE), lambda ph, i: (0, i))],
        out_specs=[
            pl.BlockSpec((None, TABLE_ROWS, LANES), lambda ph, i: (i * ph, 0, 0)),
            pl.BlockSpec((SUBLANES, BLOCK_LANES), lambda ph, i: (0, 0)),
        ],
        out_shape=[
            jax.ShapeDtypeStruct((tiles, TABLE_ROWS, LANES), I32),
            jax.ShapeDtypeStruct((SUBLANES, BLOCK_LANES), I32),
        ],
        scratch_shapes=[pltpu.VMEM((TOK_TILE, TOK_TILE), BF16)] + [pltpu.VMEM((N_EXPERTS, LANES), F32)] * 4,
        compiler_params=pltpu.CompilerParams(
            dimension_semantics=("arbitrary", "arbitrary"), vmem_limit_bytes=VMEM_LIMIT),
        name="rank",
    )(eid)


def _token_rows(ref, tok):
    return ref.at[pl.ds(pl.multiple_of(tok * SUBLANES, SUBLANES), SUBLANES), :]


def _for_each_chunk(tab, fn, rows=(RUN_START_ROW, RUN_CHUNKS_ROW, RUN_DEST_ROW)):
    start_row, chunks_row, dest_row = rows
    per_big = BIG_CHUNK_ROWS // ROW_ALIGN

    def per_expert(e, carry):
        start, dest, chunks = tab[start_row, e], tab[dest_row, e], tab[chunks_row, e]
        nbig = chunks // per_big

        def piece(nrows):
            def body(q, c2):
                fn(pl.multiple_of(start + q * nrows, ROW_ALIGN), pl.multiple_of(dest + q * nrows, ROW_ALIGN), nrows)
                return c2
            return body

        lax.fori_loop(0, nbig, piece(BIG_CHUNK_ROWS), 0)
        lax.fori_loop(nbig * per_big, chunks, piece(ROW_ALIGN), 0)
        return carry

    lax.fori_loop(0, N_EXPERTS, per_expert, 0)


def _dispatch_kernel(*refs, tile_base, aliased, fill_pads):
    if aliased:
        tab_hbm, h_ref, _, xs_hbm, tab, tokbuf, loc, xl, zrows, sem_tab, sem_row = refs
    else:
        tab_hbm, h_ref, xs_hbm, tab, tokbuf, loc, xl, zrows, sem_tab, sem_row = refs
    t = h_ref.shape[0]
    cp = pltpu.make_async_copy(tab_hbm.at[tile_base + pl.program_id(0)], tab, sem_tab)
    cp.start()
    for c in range(SUBLANES):
        tokbuf[pl.ds(c, t, stride=SUBLANES), :] = h_ref[:, c * LANES:(c + 1) * LANES]

    @pl.when(pl.program_id(0) == 0)
    def _():
        loc[...] = jnp.zeros_like(loc)

    cp.wait()

    def place(q, carry):
        for lane in range(LANES):
            v = _token_rows(tokbuf, q * LANES + lane)[...]
            _token_rows(loc, tab[q, lane])[...] = v
            _token_rows(loc, tab[POS_ROWS + q, lane])[...] = v
        return carry

    lax.fori_loop(0, t // LANES, place, 0)
    for c in range(SUBLANES):
        xl[:, c * LANES:(c + 1) * LANES] = loc[pl.ds(c, LOCAL_ROWS, stride=SUBLANES), :].astype(BF16)

    def chunk_copy(local_row, global_row, nrows):
        return pltpu.make_async_copy(xl.at[pl.ds(local_row, nrows), :],
                                     xs_hbm.at[pl.ds(global_row, nrows), :], sem_row)

    _for_each_chunk(tab, lambda *a: chunk_copy(*a).start())
    _for_each_chunk(tab, lambda *a: chunk_copy(*a).wait())

    if fill_pads:
        @pl.when(pl.program_id(0) == pl.num_programs(0) - 1)
        def _():
            zrows[...] = jnp.zeros_like(zrows)
            zero_copy = lambda _, g, nrows: pltpu.make_async_copy(
                zrows.at[pl.ds(0, nrows), :], xs_hbm.at[pl.ds(g, nrows), :], sem_row)
            pad_rows = (PAD_START_ROW, PAD_CHUNKS_ROW, PAD_START_ROW)
            _for_each_chunk(tab, lambda *a: zero_copy(*a).start(), pad_rows)
            _for_each_chunk(tab, lambda *a: zero_copy(*a).wait(), pad_rows)


def _dispatch(tabs, h2d, xs, *, tile, tile_base, sorted_rows, fill_pads):
    n, d = h2d.shape
    aliased = xs is not None
    kern = functools.partial(_dispatch_kernel, tile_base=tile_base, aliased=aliased, fill_pads=fill_pads)
    in_specs = [pl.BlockSpec(memory_space=pl.ANY), pl.BlockSpec((tile, d), lambda i: (i, 0))]
    args = [tabs, h2d]
    if aliased:
        in_specs.append(pl.BlockSpec(memory_space=pl.ANY))
        args.append(xs)
    return pl.pallas_call(
        kern,
        grid=(n // tile,),
        in_specs=in_specs,
        out_specs=pl.BlockSpec(memory_space=pl.ANY),
        out_shape=jax.ShapeDtypeStruct((sorted_rows, d), BF16),
        scratch_shapes=[pltpu.SMEM((TABLE_ROWS, LANES), I32), pltpu.VMEM((tile * SUBLANES, LANES), F32),
                        pltpu.VMEM((LOCAL_ROWS * SUBLANES, LANES), F32), pltpu.VMEM((LOCAL_ROWS, d), BF16),
                        pltpu.VMEM((BIG_CHUNK_ROWS, d), BF16), pltpu.SemaphoreType.DMA, pltpu.SemaphoreType.DMA],
        input_output_aliases={2: 0} if aliased else {},
        compiler_params=pltpu.CompilerParams(
            dimension_semantics=("arbitrary",), vmem_limit_bytes=VMEM_LIMIT),
        name="dispatch",
    )(*args)


def _combine_kernel(tab_hbm, gate_hbm, ys_hbm, h_ref, g_ref, b_ref, o_ref,
                    tab, gates, yl, ytok, ztok, sem_tab, sem_gate, sem_row, *, tile_base, alpha):
    t = h_ref.shape[0]
    i = pl.program_id(0)
    cp_tab = pltpu.make_async_copy(tab_hbm.at[tile_base + i], tab, sem_tab)
    cp_gate = pltpu.make_async_copy(gate_hbm.at[:, pl.ds(pl.multiple_of(i * t, LANES), t)], gates, sem_gate)
    cp_tab.start()
    cp_gate.start()

    @pl.when(i == 0)
    def _():
        yl[...] = jnp.zeros_like(yl)

    cp_tab.wait()

    def chunk_copy(local_row, global_row, nrows):
        return pltpu.make_async_copy(ys_hbm.at[pl.ds(global_row, nrows), :],
                                     yl.at[pl.ds(local_row, nrows), :], sem_row)

    _for_each_chunk(tab, lambda *a: chunk_copy(*a).start())
    _for_each_chunk(tab, lambda *a: chunk_copy(*a).wait())
    for c in range(SUBLANES):
        ytok[pl.ds(c, LOCAL_ROWS, stride=SUBLANES), :] = yl[:, c * LANES:(c + 1) * LANES].astype(F32)
    cp_gate.wait()

    def gather(q, carry):
        for lane in range(LANES):
            n = q * LANES + lane
            _token_rows(ztok, n)[...] = (gates[0, n] * _token_rows(ytok, tab[q, lane])[...]
                                         + gates[1, n] * _token_rows(ytok, tab[POS_ROWS + q, lane])[...])
        return carry

    lax.fori_loop(0, t // LANES, gather, 0)
    f = jnp.concatenate([ztok[pl.ds(c, t, stride=SUBLANES), :] for c in range(SUBLANES)], axis=1)
    o_ref[...] = _layer_norm(alpha * h_ref[...] + f, g_ref[...], b_ref[...])


def _combine(tabs, gate, ys, h2d, p, *, tile, tile_base, alpha):
    n, d = h2d.shape
    kern = functools.partial(_combine_kernel, tile_base=tile_base, alpha=alpha)
    return pl.pallas_call(
        kern,
        grid=(n // tile,),
        in_specs=[
            pl.BlockSpec(memory_space=pl.ANY), pl.BlockSpec(memory_space=pl.ANY), pl.BlockSpec(memory_space=pl.ANY),
            pl.BlockSpec((tile, d), lambda i: (i, 0)),
            _const_spec((1, d)), _const_spec((1, d)),
        ],
        out_specs=pl.BlockSpec((tile, d), lambda i: (i, 0)),
        out_shape=jax.ShapeDtypeStruct((n, d), F32),
        scratch_shapes=[pltpu.SMEM((TABLE_ROWS, LANES), I32), pltpu.SMEM((SUBLANES, tile), F32),
                        pltpu.VMEM((LOCAL_ROWS, d), BF16), pltpu.VMEM((LOCAL_ROWS * SUBLANES, LANES), F32),
                        pltpu.VMEM((tile * SUBLANES, LANES), F32),
                        pltpu.SemaphoreType.DMA, pltpu.SemaphoreType.DMA, pltpu.SemaphoreType.DMA],
        compiler_params=pltpu.CompilerParams(
            dimension_semantics=("arbitrary",), vmem_limit_bytes=VMEM_LIMIT),
        name="combine",
    )(tabs, gate, ys, h2d, p["ln2_g"], p["ln2_b"])


def _moe_kernel(blk_ref, x_ref, wg_ref, wu_ref, wd_ref, y_ref, wgb, wub, wdb):
    b = pl.program_id(0)

    @pl.when(jnp.logical_or(b == 0, blk_ref[0, b] != blk_ref[0, jnp.maximum(b - 1, 0)]))
    def _():
        wgb[...] = wg_ref[...].astype(BF16)
        wub[...] = wu_ref[...].astype(BF16)
        wdb[...] = wd_ref[...].astype(BF16)

    @pl.when(blk_ref[1, b] == 1)
    def _():
        x = x_ref[...]
        hg = _dot(x, wgb[...])
        hu = _dot(x, wub[...])
        hid = (hg * _sigmoid(hg) * hu).astype(BF16)
        y_ref[...] = _dot(hid, wdb[...]).astype(BF16)

    @pl.when(blk_ref[1, b] == 0)
    def _():
        y_ref[...] = jnp.zeros_like(y_ref)


def _moe(blk, xs, wg, wu, wd, layer):
    rows, d = xs.shape
    nblocks = rows // MOE_BLOCK
    f = wg.shape[-1]
    block = lambda b, blk: (jnp.where(blk[1, b] == 1, b, nblocks - 1), 0)
    return pl.pallas_call(
        _moe_kernel,
        grid_spec=pltpu.PrefetchScalarGridSpec(
            num_scalar_prefetch=1,
            grid=(nblocks,),
            in_specs=[
                pl.BlockSpec((MOE_BLOCK, d), block),
                pl.BlockSpec((None, None, d, f), lambda b, blk: (layer, blk[0, b], 0, 0)),
                pl.BlockSpec((None, None, d, f), lambda b, blk: (layer, blk[0, b], 0, 0)),
                pl.BlockSpec((None, None, f, d), lambda b, blk: (layer, blk[0, b], 0, 0)),
            ],
            out_specs=pl.BlockSpec((MOE_BLOCK, d), lambda b, blk: (b, 0)),
            scratch_shapes=[pltpu.VMEM((d, f), BF16), pltpu.VMEM((d, f), BF16), pltpu.VMEM((f, d), BF16)],
        ),
        out_shape=jax.ShapeDtypeStruct((rows, d), BF16),
        compiler_params=pltpu.CompilerParams(
            dimension_semantics=("arbitrary",), vmem_limit_bytes=VMEM_LIMIT),
        name="moe",
    )(blk, xs, wg, wu, wd)


def _sparse_moe(sets, p, *, layer, alpha):
    eids, bases, tiles = [], [], 0
    for h, e, _ in sets:
        n = h.shape[0]
        n_pad = -(-n // TOK_TILE) * TOK_TILE
        eids.append(jnp.concatenate([e, jnp.full((SUBLANES, n_pad - n), -1, I32)], axis=1))
        bases.append(tiles)
        tiles += n_pad // TOK_TILE
    n_tok = sum(h.shape[0] for h, _, _ in sets)
    run_pad = tiles * N_EXPERTS * (ROW_ALIGN - 1)
    nblocks = -(-(2 * n_tok + run_pad + N_EXPERTS * (MOE_BLOCK - 1)) // MOE_BLOCK)
    assert nblocks <= BLOCK_LANES
    sorted_rows = nblocks * MOE_BLOCK
    tabs, blk = _rank(jnp.concatenate(eids, axis=1), sorted_rows)
    xs = None
    for k, ((h, _, _), tb) in enumerate(zip(sets, bases)):
        xs = _dispatch(tabs, h, xs, tile=min(TOK_TILE, h.shape[0]), tile_base=tb, sorted_rows=sorted_rows,
                       fill_pads=k == len(sets) - 1)
    ys = _moe(blk, xs, p["w_gate"], p["w_up"], p["w_down"], layer)
    return [_combine(tabs, g, ys, h, p, tile=min(TOK_TILE, h.shape[0]), tile_base=tb, alpha=alpha)
            for (h, _, g), tb in zip(sets, bases)]


def _pad_rows(w, rows):
    return jnp.concatenate([w, jnp.zeros((rows - w.shape[0],) + w.shape[1:], w.dtype)], axis=0)


def _router_params(gw, gb, ew, eb):
    d = gw.shape[0]
    rw = jnp.concatenate([gw.T, jnp.zeros((8 - N_GROUPS, d), gw.dtype), ew.T], axis=0).astype(BF16)
    rb = jnp.concatenate([gb, jnp.zeros((8 - N_GROUPS,), gb.dtype), eb])[:, None].astype(F32)
    return rw, rb


def _to_blocks(x, seq_tile):
    nseq, npos, d = x.shape
    return x.reshape(nseq // seq_tile, seq_tile, npos, d).swapaxes(1, 2).reshape(nseq * npos, d)


def _from_blocks(x2d, nseq, seq_tile):
    n, d = x2d.shape
    npos = n // nseq
    return x2d.reshape(nseq // seq_tile, npos, seq_tile, d).swapaxes(1, 2).reshape(nseq, npos, d)


def kernel(x_prompt, x_sample, state_conv_a, state_conv_b, meta_tokens, sc_w_in, sc_conv_w, sc_w_out, cf_w_pw1, cf_b_pw1, cf_conv_w, cf_conv_b, cf_ln_g, cf_ln_b, cf_w_pw2, cf_b_pw2, ln1_g, ln1_b, ln2_g, ln2_b, rt_group_w, rt_group_b, rt_expert_w, rt_expert_b, moe_w_gate, moe_w_up, moe_w_down):
    bsz, seq, d = x_prompt.shape
    dec_b, dec_t, _ = x_sample.shape
    n_meta = meta_tokens.shape[0]
    depth = ln1_g.shape[0]
    alpha = float((2 * depth) ** 0.25)
    row = lambda v: v[None, :].astype(F32)
    s_seq = 32
    p_pos = 512 // bsz

    layer = []
    for i in range(depth):
        rw, rb = _router_params(rt_group_w[i], rt_group_b[i], rt_expert_w[i], rt_expert_b[i])
        layer.append(dict(
            ln1_g=row(ln1_g[i]), ln1_b=row(ln1_b[i]), ln2_g=row(ln2_g[i]), ln2_b=row(ln2_b[i]), rw=rw, rb=rb,
            w_gate=moe_w_gate, w_up=moe_w_up, w_down=moe_w_down))
    pa = dict(layer[0], w_in=sc_w_in[0].astype(BF16), conv_w=_pad_rows(sc_conv_w[0], 8),
              w_out=sc_w_out[0].astype(BF16))
    pb = dict(layer[1], w_pw1=cf_w_pw1[0].astype(BF16), b_pw1=row(cf_b_pw1[0]),
              conv_w=jnp.repeat(cf_conv_w[0].astype(F32), SUBLANES, axis=0), conv_b=row(cf_conv_b[0]),
              cln_g=row(cf_ln_g[0]), cln_b=row(cf_ln_b[0]), w_pw2=cf_w_pw2[0].astype(BF16), b_pw2=row(cf_b_pw2[0]))

    xp = _to_blocks(x_prompt, bsz)
    xs = _to_blocks(x_sample, s_seq)
    xm = jnp.broadcast_to(meta_tokens.astype(F32)[:, None, :], (n_meta, bsz, d)).reshape(n_meta * bsz, d)
    mix_p = dict(pos_tile=p_pos, seq_tile=bsz, nseq=bsz, alpha=alpha)
    mix_s = dict(pos_tile=dec_t, seq_tile=s_seq, nseq=dec_b, alpha=alpha)
    mix_m = dict(pos_tile=n_meta, seq_tile=bsz, nseq=bsz, alpha=alpha)

    zero_a = jnp.zeros(((SC_WIDTH - 1) * bsz, d), F32)
    hm, em, gm, tail_am = _mixer("a", xm, zero_a, pa, **mix_m)
    hp, ep, gp, tail_ap = _mixer("a", xp, tail_am, pa, **mix_p)
    hs, es, gs, tail_as = _mixer("a", xs, _to_blocks(state_conv_a[0], s_seq), pa, **mix_s)
    hp, hs, hm = _sparse_moe([(hp, ep, gp), (hs, es, gs), (hm, em, gm)], pa, layer=0, alpha=alpha)

    zero_b = jnp.zeros(((CF_WIDTH - 1) * bsz, d), F32)
    _, _, _, tail_bm = _mixer("b", hm, zero_b, pb, **mix_m)
    hp, ep, gp, tail_bp = _mixer("b", hp, tail_bm, pb, **mix_p)
    hs, es, gs, tail_bs = _mixer("b", hs, _to_blocks(state_conv_b[0], s_seq), pb, **mix_s)
    hp, hs = _sparse_moe([(hp, ep, gp), (hs, es, gs)], pb, layer=1, alpha=alpha)

    return (_from_blocks(hp, bsz, bsz), _from_blocks(hs, dec_b, s_seq),
            _from_blocks(tail_ap, bsz, bsz)[None], _from_blocks(tail_bp, bsz, bsz)[None],
            _from_blocks(tail_as, dec_b, s_seq)[None], _from_blocks(tail_bs, dec_b, s_seq)[None])
```

```python
import functools

import jax
import jax.numpy as jnp
from jax import lax
from jax.experimental import pallas as pl
from jax.experimental.pallas import tpu as pltpu

F32 = jnp.float32
BF16 = jnp.bfloat16
I32 = jnp.int32

LN_EPS = 1e-5
N_GROUPS = 4
EXPERTS_PER_GROUP = 8
N_EXPERTS = N_GROUPS * EXPERTS_PER_GROUP
SC_WIDTH = 3
CF_WIDTH = 31
ROUTER_ROWS = 8 + N_EXPERTS
SUBLANES = 8
LANES = 128
ROW_ALIGN = 16
BIG_CHUNK_ROWS = 64
CONV_ROWS = 16
VMEM_LIMIT = 56 * 1024 * 1024
MOE_BLOCK = 512
TOK_TILE = 1024
LOCAL_ROWS = 2 * TOK_TILE + N_EXPERTS * ROW_ALIGN
POS_ROWS = TOK_TILE // LANES
RUN_START_ROW, RUN_CHUNKS_ROW, RUN_DEST_ROW = 2 * POS_ROWS, 2 * POS_ROWS + 1, 2 * POS_ROWS + 2
PAD_START_ROW, PAD_CHUNKS_ROW = 2 * POS_ROWS + 3, 2 * POS_ROWS + 4
TABLE_ROWS = 2 * POS_ROWS + SUBLANES
BLOCK_LANES = 256


def _dot(a, b):
    return jnp.dot(a, b, preferred_element_type=F32)


def _layer_norm(z, g, b):
    mu = jnp.mean(z, axis=-1, keepdims=True)
    zc = z - mu
    var = jnp.mean(zc * zc, axis=-1, keepdims=True)
    return zc * lax.rsqrt(var + LN_EPS) * g + b


def _sigmoid(x):
    return 1.0 / (1.0 + jnp.exp(-x))


def _route(h1b, rw_ref, rb_ref):
    t = h1b.shape[0]
    logits = lax.dot_general(rw_ref[...], h1b, (((1,), (1,)), ((), ())),
                             preferred_element_type=F32) + rb_ref[...]
    g = [logits[i:i + 1, :] for i in range(N_GROUPS)]
    gmax = jnp.maximum(jnp.maximum(g[0], g[1]), jnp.maximum(g[2], g[3]))
    gidx = jnp.where(g[0] == gmax, 0, jnp.where(g[1] == gmax, 1, jnp.where(g[2] == gmax, 2, 3)))
    gsum = (jnp.exp(g[0] - gmax) + jnp.exp(g[1] - gmax)) + (jnp.exp(g[2] - gmax) + jnp.exp(g[3] - gmax))
    gp = 1.0 / gsum
    sel = logits[8 + 8 * (N_GROUPS - 1):8 + 8 * N_GROUPS, :]
    for gi in range(N_GROUPS - 2, -1, -1):
        sel = jnp.where(gidx == gi, logits[8 + 8 * gi:16 + 8 * gi, :], sel)
    row = lax.broadcasted_iota(I32, (EXPERTS_PER_GROUP, t), 0)
    m1 = jnp.max(sel, axis=0, keepdims=True)
    i1 = jnp.min(jnp.where(sel == m1, row, EXPERTS_PER_GROUP), axis=0, keepdims=True)
    sel2 = jnp.where(row == i1, -jnp.inf, sel)
    m2 = jnp.max(sel2, axis=0, keepdims=True)
    i2 = jnp.min(jnp.where(sel2 == m2, row, EXPERTS_PER_GROUP), axis=0, keepdims=True)
    d = jnp.exp(m2 - m1)
    w1 = 1.0 / (1.0 + d)
    w2 = d / (1.0 + d)
    return gidx * EXPERTS_PER_GROUP + i1, gidx * EXPERTS_PER_GROUP + i2, gp * w1, gp * w2


def _post_mixer(x, y, alpha, g_ref, b_ref, rw_ref, rb_ref, h1_ref, eid_ref, gate_ref):
    h1 = _layer_norm(alpha * x + y, g_ref[...], b_ref[...])
    h1_ref[...] = h1
    e1, e2, g1, g2 = _route(h1.astype(BF16), rw_ref, rb_ref)
    row8 = lax.broadcasted_iota(I32, (SUBLANES, h1.shape[0]), 0)
    eid_ref[...] = jnp.where(row8 == 0, e1, jnp.where(row8 == 1, e2, -1))
    gate_ref[...] = jnp.where(row8 == 0, g1, jnp.where(row8 == 1, g2, 0.0))


def _load_history(ubuf, hist_ref, rows, hist_rows):
    j = pl.program_id(1)

    @pl.when(j == 0)
    def _():
        ubuf[0:hist_rows, :] = hist_ref[...]

    @pl.when(j > 0)
    def _():
        ubuf[0:hist_rows, :] = ubuf[rows:rows + hist_rows, :]


def _position_major(x_ref, xin):
    nseq, npos, d = x_ref.shape
    for c in range(d // LANES):
        for s in range(nseq):
            xin[c, pl.ds(s, npos, stride=nseq), :] = x_ref[s, :, c * LANES:(c + 1) * LANES]
    return jnp.concatenate([xin[c] for c in range(d // LANES)], axis=1)


def _by_sequence(rows, o_ref, zout):
    nseq, npos, d = o_ref.shape
    for c in range(d // LANES):
        zout[c] = rows[:, c * LANES:(c + 1) * LANES]
    for c in range(d // LANES):
        for s in range(nseq):
            o_ref[s, :, c * LANES:(c + 1) * LANES] = zout[c, pl.ds(s, npos, stride=nseq), :]


def _mixer_a_kernel(x_ref, hist_ref, win_ref, cw_ref, wout_ref, g_ref, b_ref, rw_ref, rb_ref,
                    h1_ref, eid_ref, gate_ref, tail_ref, ubuf, *maybe_xin, alpha, nseq):
    rows, d = h1_ref.shape
    hist_rows = (SC_WIDTH - 1) * nseq
    _load_history(ubuf, hist_ref, rows, hist_rows)
    x = _position_major(x_ref, *maybe_xin) if maybe_xin else x_ref[...]
    xb = x.astype(BF16)
    ubuf[hist_rows:hist_rows + rows, :] = _dot(xb, win_ref[:, d:2 * d]) * _dot(xb, win_ref[:, 2 * d:3 * d])
    tail_ref[...] = ubuf[rows:rows + hist_rows, :]
    conv = cw_ref[0:1, :] * ubuf[0:rows, :]
    for k in range(1, SC_WIDTH):
        conv = conv + cw_ref[k:k + 1, :] * ubuf[k * nseq:k * nseq + rows, :]
    bg = _dot(xb, win_ref[:, 0:d])
    y = _dot((bg * conv).astype(BF16), wout_ref[...])
    _post_mixer(x, y, alpha, g_ref, b_ref, rw_ref, rb_ref, h1_ref, eid_ref, gate_ref)


def _mixer_b_kernel(x_ref, hist_ref, w1_ref, b1_ref, cw_ref, cb_ref, lg_ref, lb_ref, w2_ref, b2_ref,
                    g_ref, b_ref, rw_ref, rb_ref,
                    h1_ref, eid_ref, gate_ref, tail_ref, ubuf, cbuf, *, alpha, nseq):
    rows, d = x_ref.shape
    hist_rows = (CF_WIDTH - 1) * nseq
    _load_history(ubuf, hist_ref, rows, hist_rows)
    x = x_ref[...]
    xb = x.astype(BF16)
    a = _dot(xb, w1_ref[:, 0:d]) + b1_ref[:, 0:d]
    gl = _dot(xb, w1_ref[:, d:2 * d]) + b1_ref[:, d:2 * d]
    ubuf[hist_rows:hist_rows + rows, :] = a * _sigmoid(gl)
    tail_ref[...] = ubuf[rows:rows + hist_rows, :]
    halves = CONV_ROWS // SUBLANES

    def chunk(c, carry):
        r0 = pl.multiple_of(c * CONV_ROWS, CONV_ROWS)
        acc = [jnp.broadcast_to(cb_ref[...], (SUBLANES, d)) for _ in range(halves)]
        for k in range(CF_WIDTH):
            w = cw_ref[k * SUBLANES:(k + 1) * SUBLANES, :]
            for h in range(halves):
                start = pl.multiple_of(r0 + h * SUBLANES + k * nseq, SUBLANES)
                acc[h] = acc[h] + w * ubuf[pl.ds(start, SUBLANES), :]
        for h in range(halves):
            cbuf[pl.ds(pl.multiple_of(r0 + h * SUBLANES, SUBLANES), SUBLANES), :] = acc[h]
        return carry

    lax.fori_loop(0, rows // CONV_ROWS, chunk, 0)
    cn = _layer_norm(cbuf[...], lg_ref[...], lb_ref[...])
    cn = cn * _sigmoid(cn)
    y = _dot(cn.astype(BF16), w2_ref[...]) + b2_ref[...]
    _post_mixer(x, y, alpha, g_ref, b_ref, rw_ref, rb_ref, h1_ref, eid_ref, gate_ref)


def _const_spec(shape):
    nd = len(shape)
    return pl.BlockSpec(shape, lambda *_: (0,) * nd, pipeline_mode=pl.Buffered(1))


def _mixer(kind, x2d, hist2d, p, *, pos_tile, seq_tile, nseq, alpha, by_sequence=False):
    d = x2d.shape[-1]
    n = x2d.size // d
    width = SC_WIDTH if kind == "a" else CF_WIDTH
    rows = pos_tile * seq_tile
    hist_rows = (width - 1) * seq_tile
    seq_blocks = nseq // seq_tile
    steps = n // (rows * seq_blocks)
    if kind == "a":
        kern = functools.partial(_mixer_a_kernel, alpha=alpha, nseq=seq_tile)
        weights = (p["w_in"], p["conv_w"], p["w_out"])
        wspecs = [_const_spec((d, 3 * d)), _const_spec((8, d)), _const_spec((d, d))]
        scratch = [pltpu.VMEM((rows + hist_rows, d), F32)]
        if by_sequence:
            scratch.append(pltpu.VMEM((d // LANES, rows, LANES), F32))
    else:
        assert not by_sequence
        kern = functools.partial(_mixer_b_kernel, alpha=alpha, nseq=seq_tile)
        weights = (p["w_pw1"], p["b_pw1"], p["conv_w"], p["conv_b"], p["cln_g"], p["cln_b"], p["w_pw2"], p["b_pw2"])
        wspecs = [_const_spec((d, 2 * d)), _const_spec((1, 2 * d)), _const_spec((CF_WIDTH * SUBLANES, d)),
                  _const_spec((1, d)), _const_spec((1, d)), _const_spec((1, d)), _const_spec((d, d)),
                  _const_spec((1, d))]
        scratch = [pltpu.VMEM((rows + hist_rows, d), F32), pltpu.VMEM((rows, d), F32)]
    common = (p["ln1_g"], p["ln1_b"], p["rw"], p["rb"])
    cspecs = [_const_spec((1, d)), _const_spec((1, d)), _const_spec((ROUTER_ROWS, d)), _const_spec((ROUTER_ROWS, 1))]
    if by_sequence:
        assert seq_blocks == 1
        x_spec = pl.BlockSpec((nseq, pos_tile, d), lambda s, j: (0, j, 0))
    else:
        x_spec = pl.BlockSpec((rows, d), lambda s, j: (s * steps + j, 0))
    return pl.pallas_call(
        kern,
        grid=(seq_blocks, steps),
        in_specs=[x_spec, pl.BlockSpec((hist_rows, d), lambda s, j: (s, 0))] + wspecs + cspecs,
        out_specs=[
            pl.BlockSpec((rows, d), lambda s, j: (s * steps + j, 0)),
            pl.BlockSpec((SUBLANES, rows), lambda s, j: (0, s * steps + j)),
            pl.BlockSpec((SUBLANES, rows), lambda s, j: (0, s * steps + j)),
            pl.BlockSpec((hist_rows, d), lambda s, j: (s, 0)),
        ],
        out_shape=[
            jax.ShapeDtypeStruct((n, d), F32),
            jax.ShapeDtypeStruct((SUBLANES, n), I32),
            jax.ShapeDtypeStruct((SUBLANES, n), F32),
            jax.ShapeDtypeStruct((seq_blocks * hist_rows, d), F32),
        ],
        scratch_shapes=scratch,
        compiler_params=pltpu.CompilerParams(
            dimension_semantics=("arbitrary", "arbitrary"), vmem_limit_bytes=VMEM_LIMIT),
        name="mixer_" + kind,
    )(x2d, hist2d, *weights, *common)


def _rank_kernel(eid_ref, tab_ref, blk_ref, tri, tot, prior, pad_start, pad_chunks, *, total_rows):
    phase, i = pl.program_id(0), pl.program_id(1)
    t = eid_ref.shape[1]
    row = lax.broadcasted_iota(I32, (N_EXPERTS, t), 0)
    oh1 = (row == eid_ref[0:1, :]).astype(F32)
    oh2 = (row == eid_ref[1:2, :]).astype(F32)
    c1 = jnp.sum(oh1, axis=1, keepdims=True)
    c2 = jnp.sum(oh2, axis=1, keepdims=True)
    chunks = jnp.ceil((c1 + c2) * (1.0 / ROW_ALIGN))
    tab_ref[...] = jnp.zeros_like(tab_ref)

    @pl.when(jnp.logical_and(phase == 0, i == 0))
    def _():
        tot[...] = jnp.zeros_like(tot)
        a = lax.broadcasted_iota(I32, tri.shape, 0)
        b = lax.broadcasted_iota(I32, tri.shape, 1)
        tri[...] = (a <= b).astype(BF16)

    @pl.when(phase == 0)
    def _():
        tot[...] += chunks * ROW_ALIGN

    @pl.when(jnp.logical_and(phase == 1, i == 0))
    def _():
        cnt = tot[...].astype(I32)
        seg = ((cnt + (MOE_BLOCK - 1)) // MOE_BLOCK) * MOE_BLOCK
        r128 = lax.broadcasted_iota(I32, cnt.shape, 0)
        base = jnp.zeros_like(cnt)
        for e in range(N_EXPERTS - 1):
            base = base + jnp.where(r128 > e, seg[e:e + 1, :], 0)
        prior[...] = base.astype(F32)
        pad_start[...] = (base + cnt).astype(F32)
        pad_end = jnp.where(r128 == N_EXPERTS - 1, total_rows, base + seg)
        pad_chunks[...] = ((pad_end - base - cnt) // ROW_ALIGN).astype(F32)
        end_blk = (base + seg) // MOE_BLOCK
        lane = lax.broadcasted_iota(I32, (N_EXPERTS, BLOCK_LANES), 1)
        expert_of = jnp.sum((end_blk[:, 0:1] <= lane).astype(F32), axis=0, keepdims=True).astype(I32)
        active = (lane[0:1, :] < end_blk[N_EXPERTS - 1:N_EXPERTS, 0:1]).astype(I32)
        r8 = lax.broadcasted_iota(I32, (SUBLANES, BLOCK_LANES), 0)
        blk_ref[...] = jnp.where(r8 == 0, jnp.minimum(expert_of, N_EXPERTS - 1), jnp.where(r8 == 1, active, 0))

    @pl.when(phase == 1)
    def _():
        cs = _dot(jnp.concatenate([oh1, oh2], axis=0).astype(BF16), tri[...])
        ea = lax.broadcasted_iota(I32, (N_EXPERTS, N_EXPERTS), 0)
        eb = lax.broadcasted_iota(I32, (N_EXPERTS, N_EXPERTS), 1)
        chunks_b = jnp.broadcast_to(chunks, (N_EXPERTS, LANES))
        start = _dot((eb < ea).astype(BF16), chunks_b.astype(BF16)) * ROW_ALIGN
        s0 = start[:, 0:1]
        pos1 = jnp.sum(oh1 * (s0 + cs[0:N_EXPERTS, :] - 1.0), axis=0, keepdims=True).astype(I32)
        pos2 = jnp.sum(oh2 * (s0 + c1 + cs[N_EXPERTS:2 * N_EXPERTS, :] - 1.0), axis=0, keepdims=True).astype(I32)
        for q in range(POS_ROWS):
            tab_ref[q:q + 1, :] = pos1[:, q * LANES:(q + 1) * LANES]
            tab_ref[POS_ROWS + q:POS_ROWS + q + 1, :] = pos2[:, q * LANES:(q + 1) * LANES]
        diag = lax.broadcasted_iota(I32, (N_EXPERTS, LANES), 0) == lax.broadcasted_iota(I32, (N_EXPERTS, LANES), 1)
        to_lanes = lambda col: jnp.sum(jnp.where(diag, col, 0.0), axis=0, keepdims=True).astype(I32)
        tab_ref[RUN_START_ROW:RUN_START_ROW + 1, :] = to_lanes(start)
        tab_ref[RUN_CHUNKS_ROW:RUN_CHUNKS_ROW + 1, :] = to_lanes(chunks_b)
        tab_ref[RUN_DEST_ROW:RUN_DEST_ROW + 1, :] = to_lanes(prior[...])
        tab_ref[PAD_START_ROW:PAD_START_ROW + 1, :] = to_lanes(pad_start[...])
        tab_ref[PAD_CHUNKS_ROW:PAD_CHUNKS_ROW + 1, :] = to_lanes(pad_chunks[...])
        prior[...] += chunks * ROW_ALIGN


def _rank(eid, total_rows):
    n = eid.shape[1]
    tiles = n // TOK_TILE
    return pl.pallas_call(
        functools.partial(_rank_kernel, total_rows=total_rows),
        grid=(2, tiles),
        in_specs=[pl.BlockSpec((SUBLANES, TOK_TILE), lambda ph, i: (0, i))],
        out_specs=[
            pl.BlockSpec((None, TABLE_ROWS, LANES), lambda ph, i: (i * ph, 0, 0)),
            pl.BlockSpec((SUBLANES, BLOCK_LANES), lambda ph, i: (0, 0)),
        ],
        out_shape=[
            jax.ShapeDtypeStruct((tiles, TABLE_ROWS, LANES), I32),
            jax.ShapeDtypeStruct((SUBLANES, BLOCK_LANES), I32),
        ],
        scratch_shapes=[pltpu.VMEM((TOK_TILE, TOK_TILE), BF16)] + [pltpu.VMEM((N_EXPERTS, LANES), F32)] * 4,
        compiler_params=pltpu.CompilerParams(
            dimension_semantics=("arbitrary", "arbitrary"), vmem_limit_bytes=VMEM_LIMIT),
        name="rank",
    )(eid)


def _token_rows(ref, tok):
    return ref.at[pl.ds(pl.multiple_of(tok * SUBLANES, SUBLANES), SUBLANES), :]


def _for_each_chunk(tab, fn, rows=(RUN_START_ROW, RUN_CHUNKS_ROW, RUN_DEST_ROW)):
    start_row, chunks_row, dest_row = rows
    per_big = BIG_CHUNK_ROWS // ROW_ALIGN

    def per_expert(e, carry):
        start, dest, chunks = tab[start_row, e], tab[dest_row, e], tab[chunks_row, e]
        nbig = chunks // per_big

        def piece(nrows):
            def body(q, c2):
                fn(pl.multiple_of(start + q * nrows, ROW_ALIGN), pl.multiple_of(dest + q * nrows, ROW_ALIGN), nrows)
                return c2
            return body

        lax.fori_loop(0, nbig, piece(BIG_CHUNK_ROWS), 0)
        lax.fori_loop(nbig * per_big, chunks, piece(ROW_ALIGN), 0)
        return carry

    lax.fori_loop(0, N_EXPERTS, per_expert, 0)


def _dispatch_kernel(*refs, tile_base, aliased, fill_pads):
    if aliased:
        tab_hbm, h_ref, _, xs_hbm, tab, tokbuf, loc, xl, zrows, sem_tab, sem_row = refs
    else:
        tab_hbm, h_ref, xs_hbm, tab, tokbuf, loc, xl, zrows, sem_tab, sem_row = refs
    t = h_ref.shape[0]
    cp = pltpu.make_async_copy(tab_hbm.at[tile_base + pl.program_id(0)], tab, sem_tab)
    cp.start()
    for c in range(SUBLANES):
        tokbuf[pl.ds(c, t, stride=SUBLANES), :] = h_ref[:, c * LANES:(c + 1) * LANES]

    @pl.when(pl.program_id(0) == 0)
    def _():
        loc[...] = jnp.zeros_like(loc)

    cp.wait()

    def place(q, carry):
        for lane in range(LANES):
            v = _token_rows(tokbuf, q * LANES + lane)[...]
            _token_rows(loc, tab[q, lane])[...] = v
            _token_rows(loc, tab[POS_ROWS + q, lane])[...] = v
        return carry

    lax.fori_loop(0, t // LANES, place, 0)
    for c in range(SUBLANES):
        xl[:, c * LANES:(c + 1) * LANES] = loc[pl.ds(c, LOCAL_ROWS, stride=SUBLANES), :].astype(BF16)

    def chunk_copy(local_row, global_row, nrows):
        return pltpu.make_async_copy(xl.at[pl.ds(local_row, nrows), :],
                                     xs_hbm.at[pl.ds(global_row, nrows), :], sem_row)

    _for_each_chunk(tab, lambda *a: chunk_copy(*a).start())
    _for_each_chunk(tab, lambda *a: chunk_copy(*a).wait())

    if fill_pads:
        @pl.when(pl.program_id(0) == pl.num_programs(0) - 1)
        def _():
            zrows[...] = jnp.zeros_like(zrows)
            zero_copy = lambda _, g, nrows: pltpu.make_async_copy(
                zrows.at[pl.ds(0, nrows), :], xs_hbm.at[pl.ds(g, nrows), :], sem_row)
            pad_rows = (PAD_START_ROW, PAD_CHUNKS_ROW, PAD_START_ROW)
            _for_each_chunk(tab, lambda *a: zero_copy(*a).start(), pad_rows)
            _for_each_chunk(tab, lambda *a: zero_copy(*a).wait(), pad_rows)


def _dispatch(tabs, h2d, xs, *, tile, tile_base, sorted_rows, fill_pads):
    n, d = h2d.shape
    aliased = xs is not None
    kern = functools.partial(_dispatch_kernel, tile_base=tile_base, aliased=aliased, fill_pads=fill_pads)
    in_specs = [pl.BlockSpec(memory_space=pl.ANY), pl.BlockSpec((tile, d), lambda i: (i, 0))]
    args = [tabs, h2d]
    if aliased:
        in_specs.append(pl.BlockSpec(memory_space=pl.ANY))
        args.append(xs)
    return pl.pallas_call(
        kern,
        grid=(n // tile,),
        in_specs=in_specs,
        out_specs=pl.BlockSpec(memory_space=pl.ANY),
        out_shape=jax.ShapeDtypeStruct((sorted_rows, d), BF16),
        scratch_shapes=[pltpu.SMEM((TABLE_ROWS, LANES), I32), pltpu.VMEM((tile * SUBLANES, LANES), F32),
                        pltpu.VMEM((LOCAL_ROWS * SUBLANES, LANES), F32), pltpu.VMEM((LOCAL_ROWS, d), BF16),
                        pltpu.VMEM((BIG_CHUNK_ROWS, d), BF16), pltpu.SemaphoreType.DMA, pltpu.SemaphoreType.DMA],
        input_output_aliases={2: 0} if aliased else {},
        compiler_params=pltpu.CompilerParams(
            dimension_semantics=("arbitrary",), vmem_limit_bytes=VMEM_LIMIT),
        name="dispatch",
    )(*args)


def _combine_kernel(tab_hbm, gate_hbm, ys_hbm, h_ref, g_ref, b_ref, o_ref,
                    tab, gates, yl, ytok, ztok, sem_tab, sem_gate, sem_row, *maybe_zout, tile_base, alpha):
    t = h_ref.shape[0]
    i = pl.program_id(0)
    cp_tab = pltpu.make_async_copy(tab_hbm.at[tile_base + i], tab, sem_tab)
    cp_gate = pltpu.make_async_copy(gate_hbm.at[:, pl.ds(pl.multiple_of(i * t, LANES), t)], gates, sem_gate)
    cp_tab.start()
    cp_gate.start()

    @pl.when(i == 0)
    def _():
        yl[...] = jnp.zeros_like(yl)

    cp_tab.wait()

    def chunk_copy(local_row, global_row, nrows):
        return pltpu.make_async_copy(ys_hbm.at[pl.ds(global_row, nrows), :],
                                     yl.at[pl.ds(local_row, nrows), :], sem_row)

    _for_each_chunk(tab, lambda *a: chunk_copy(*a).start())
    _for_each_chunk(tab, lambda *a: chunk_copy(*a).wait())
    for c in range(SUBLANES):
        ytok[pl.ds(c, LOCAL_ROWS, stride=SUBLANES), :] = yl[:, c * LANES:(c + 1) * LANES].astype(F32)
    cp_gate.wait()

    def gather(q, carry):
        for lane in range(LANES):
            n = q * LANES + lane
            _token_rows(ztok, n)[...] = (gates[0, n] * _token_rows(ytok, tab[q, lane])[...]
                                         + gates[1, n] * _token_rows(ytok, tab[POS_ROWS + q, lane])[...])
        return carry

    lax.fori_loop(0, t // LANES, gather, 0)
    f = jnp.concatenate([ztok[pl.ds(c, t, stride=SUBLANES), :] for c in range(SUBLANES)], axis=1)
    out = _layer_norm(alpha * h_ref[...] + f, g_ref[...], b_ref[...])
    if maybe_zout:
        _by_sequence(out, o_ref, *maybe_zout)
    else:
        o_ref[...] = out


def _combine(tabs, gate, ys, h2d, p, *, tile, tile_base, alpha, out_sequences=None):
    n, d = h2d.shape
    kern = functools.partial(_combine_kernel, tile_base=tile_base, alpha=alpha)
    scratch = [pltpu.SMEM((TABLE_ROWS, LANES), I32), pltpu.SMEM((SUBLANES, tile), F32),
               pltpu.VMEM((LOCAL_ROWS, d), BF16), pltpu.VMEM((LOCAL_ROWS * SUBLANES, LANES), F32),
               pltpu.VMEM((tile * SUBLANES, LANES), F32),
               pltpu.SemaphoreType.DMA, pltpu.SemaphoreType.DMA, pltpu.SemaphoreType.DMA]
    if out_sequences:
        npos = tile // out_sequences
        out_spec = pl.BlockSpec((out_sequences, npos, d), lambda i: (0, i, 0))
        out_shape = jax.ShapeDtypeStruct((out_sequences, n // out_sequences, d), F32)
        scratch.append(pltpu.VMEM((d // LANES, tile, LANES), F32))
    else:
        out_spec = pl.BlockSpec((tile, d), lambda i: (i, 0))
        out_shape = jax.ShapeDtypeStruct((n, d), F32)
    return pl.pallas_call(
        kern,
        grid=(n // tile,),
        in_specs=[
            pl.BlockSpec(memory_space=pl.ANY), pl.BlockSpec(memory_space=pl.ANY), pl.BlockSpec(memory_space=pl.ANY),
            pl.BlockSpec((tile, d), lambda i: (i, 0)),
            _const_spec((1, d)), _const_spec((1, d)),
        ],
        out_specs=out_spec,
        out_shape=out_shape,
        scratch_shapes=scratch,
        compiler_params=pltpu.CompilerParams(
            dimension_semantics=("arbitrary",), vmem_limit_bytes=VMEM_LIMIT),
        name="combine",
    )(tabs, gate, ys, h2d, p["ln2_g"], p["ln2_b"])


def _moe_kernel(blk_ref, x_ref, wg_ref, wu_ref, wd_ref, y_ref, wgb, wub, wdb):
    b = pl.program_id(0)

    @pl.when(jnp.logical_or(b == 0, blk_ref[0, b] != blk_ref[0, jnp.maximum(b - 1, 0)]))
    def _():
        wgb[...] = wg_ref[...].astype(BF16)
        wub[...] = wu_ref[...].astype(BF16)
        wdb[...] = wd_ref[...].astype(BF16)

    @pl.when(blk_ref[1, b] == 1)
    def _():
        x = x_ref[...]
        hg = _dot(x, wgb[...])
        hu = _dot(x, wub[...])
        hid = (hg * _sigmoid(hg) * hu).astype(BF16)
        y_ref[...] = _dot(hid, wdb[...]).astype(BF16)

    @pl.when(blk_ref[1, b] == 0)
    def _():
        y_ref[...] = jnp.zeros_like(y_ref)


def _moe(blk, xs, wg, wu, wd, layer):
    rows, d = xs.shape
    nblocks = rows // MOE_BLOCK
    f = wg.shape[-1]
    block = lambda b, blk: (jnp.where(blk[1, b] == 1, b, nblocks - 1), 0)
    return pl.pallas_call(
        _moe_kernel,
        grid_spec=pltpu.PrefetchScalarGridSpec(
            num_scalar_prefetch=1,
            grid=(nblocks,),
            in_specs=[
                pl.BlockSpec((MOE_BLOCK, d), block),
                pl.BlockSpec((None, None, d, f), lambda b, blk: (layer, blk[0, b], 0, 0)),
                pl.BlockSpec((None, None, d, f), lambda b, blk: (layer, blk[0, b], 0, 0)),
                pl.BlockSpec((None, None, f, d), lambda b, blk: (layer, blk[0, b], 0, 0)),
            ],
            out_specs=pl.BlockSpec((MOE_BLOCK, d), lambda b, blk: (b, 0)),
            scratch_shapes=[pltpu.VMEM((d, f), BF16), pltpu.VMEM((d, f), BF16), pltpu.VMEM((f, d), BF16)],
        ),
        out_shape=jax.ShapeDtypeStruct((rows, d), BF16),
        compiler_params=pltpu.CompilerParams(
            dimension_semantics=("arbitrary",), vmem_limit_bytes=VMEM_LIMIT),
        name="moe",
    )(blk, xs, wg, wu, wd)


def _sparse_moe(sets, p, *, layer, alpha, out_sequences=None):
    eids, bases, tiles = [], [], 0
    for h, e, _ in sets:
        n = h.shape[0]
        n_pad = -(-n // TOK_TILE) * TOK_TILE
        eids.append(jnp.concatenate([e, jnp.full((SUBLANES, n_pad - n), -1, I32)], axis=1))
        bases.append(tiles)
        tiles += n_pad // TOK_TILE
    n_tok = sum(h.shape[0] for h, _, _ in sets)
    run_pad = tiles * N_EXPERTS * (ROW_ALIGN - 1)
    nblocks = -(-(2 * n_tok + run_pad + N_EXPERTS * (MOE_BLOCK - 1)) // MOE_BLOCK)
    assert nblocks <= BLOCK_LANES
    sorted_rows = nblocks * MOE_BLOCK
    tabs, blk = _rank(jnp.concatenate(eids, axis=1), sorted_rows)
    xs = None
    for k, ((h, _, _), tb) in enumerate(zip(sets, bases)):
        xs = _dispatch(tabs, h, xs, tile=min(TOK_TILE, h.shape[0]), tile_base=tb, sorted_rows=sorted_rows,
                       fill_pads=k == len(sets) - 1)
    ys = _moe(blk, xs, p["w_gate"], p["w_up"], p["w_down"], layer)
    return [_combine(tabs, g, ys, h, p, tile=min(TOK_TILE, h.shape[0]), tile_base=tb, alpha=alpha,
                     out_sequences=out_sequences if k == 0 else None)
            for k, ((h, _, g), tb) in enumerate(zip(sets, bases))]


def _pad_rows(w, rows):
    return jnp.concatenate([w, jnp.zeros((rows - w.shape[0],) + w.shape[1:], w.dtype)], axis=0)


def _router_params(gw, gb, ew, eb):
    d = gw.shape[0]
    rw = jnp.concatenate([gw.T, jnp.zeros((8 - N_GROUPS, d), gw.dtype), ew.T], axis=0).astype(BF16)
    rb = jnp.concatenate([gb, jnp.zeros((8 - N_GROUPS,), gb.dtype), eb])[:, None].astype(F32)
    return rw, rb


def _to_blocks(x, seq_tile):
    nseq, npos, d = x.shape
    return x.reshape(nseq // seq_tile, seq_tile, npos, d).swapaxes(1, 2).reshape(nseq * npos, d)


def _from_blocks(x2d, nseq, seq_tile):
    n, d = x2d.shape
    npos = n // nseq
    return x2d.reshape(nseq // seq_tile, npos, seq_tile, d).swapaxes(1, 2).reshape(nseq, npos, d)


def kernel(x_prompt, x_sample, state_conv_a, state_conv_b, meta_tokens, sc_w_in, sc_conv_w, sc_w_out, cf_w_pw1, cf_b_pw1, cf_conv_w, cf_conv_b, cf_ln_g, cf_ln_b, cf_w_pw2, cf_b_pw2, ln1_g, ln1_b, ln2_g, ln2_b, rt_group_w, rt_group_b, rt_expert_w, rt_expert_b, moe_w_gate, moe_w_up, moe_w_down):
    bsz, seq, d = x_prompt.shape
    dec_b, dec_t, _ = x_sample.shape
    n_meta = meta_tokens.shape[0]
    depth = ln1_g.shape[0]
    alpha = float((2 * depth) ** 0.25)
    row = lambda v: v[None, :].astype(F32)
    s_seq = 32
    p_pos = 512 // bsz

    layer = []
    for i in range(depth):
        rw, rb = _router_params(rt_group_w[i], rt_group_b[i], rt_expert_w[i], rt_expert_b[i])
        layer.append(dict(
            ln1_g=row(ln1_g[i]), ln1_b=row(ln1_b[i]), ln2_g=row(ln2_g[i]), ln2_b=row(ln2_b[i]), rw=rw, rb=rb,
            w_gate=moe_w_gate, w_up=moe_w_up, w_down=moe_w_down))
    pa = dict(layer[0], w_in=sc_w_in[0].astype(BF16), conv_w=_pad_rows(sc_conv_w[0], 8),
              w_out=sc_w_out[0].astype(BF16))
    pb = dict(layer[1], w_pw1=cf_w_pw1[0].astype(BF16), b_pw1=row(cf_b_pw1[0]),
              conv_w=jnp.repeat(cf_conv_w[0].astype(F32), SUBLANES, axis=0), conv_b=row(cf_conv_b[0]),
              cln_g=row(cf_ln_g[0]), cln_b=row(cf_ln_b[0]), w_pw2=cf_w_pw2[0].astype(BF16), b_pw2=row(cf_b_pw2[0]))

    xs = _to_blocks(x_sample, s_seq)
    xm = jnp.broadcast_to(meta_tokens.astype(F32)[:, None, :], (n_meta, bsz, d)).reshape(n_meta * bsz, d)
    mix_p = dict(pos_tile=p_pos, seq_tile=bsz, nseq=bsz, alpha=alpha)
    mix_s = dict(pos_tile=dec_t, seq_tile=s_seq, nseq=dec_b, alpha=alpha)
    mix_m = dict(pos_tile=n_meta, seq_tile=bsz, nseq=bsz, alpha=alpha)

    zero_a = jnp.zeros(((SC_WIDTH - 1) * bsz, d), F32)
    hm, em, gm, tail_am = _mixer("a", xm, zero_a, pa, **mix_m)
    hp, ep, gp, tail_ap = _mixer("a", x_prompt, tail_am, pa, by_sequence=True, **mix_p)
    hs, es, gs, tail_as = _mixer("a", xs, _to_blocks(state_conv_a[0], s_seq), pa, **mix_s)
    hp, hs, hm = _sparse_moe([(hp, ep, gp), (hs, es, gs), (hm, em, gm)], pa, layer=0, alpha=alpha)

    zero_b = jnp.zeros(((CF_WIDTH - 1) * bsz, d), F32)
    _, _, _, tail_bm = _mixer("b", hm, zero_b, pb, **mix_m)
    hp, ep, gp, tail_bp = _mixer("b", hp, tail_bm, pb, **mix_p)
    hs, es, gs, tail_bs = _mixer("b", hs, _to_blocks(state_conv_b[0], s_seq), pb, **mix_s)
    y_prompt, hs = _sparse_moe([(hp, ep, gp), (hs, es, gs)], pb, layer=1, alpha=alpha, out_sequences=bsz)

    return (y_prompt, _from_blocks(hs, dec_b, s_seq),
            _from_blocks(tail_ap, bsz, bsz)[None], _from_blocks(tail_bp, bsz, bsz)[None],
            _from_blocks(tail_as, dec_b, s_seq)[None], _from_blocks(tail_bs, dec_b, s_seq)[None])
```

```python
import functools

import jax
import jax.numpy as jnp
from jax import lax
from jax.experimental import pallas as pl
from jax.experimental.pallas import tpu as pltpu

F32 = jnp.float32
BF16 = jnp.bfloat16
I32 = jnp.int32

LN_EPS = 1e-5
N_GROUPS = 4
EXPERTS_PER_GROUP = 8
N_EXPERTS = N_GROUPS * EXPERTS_PER_GROUP
SC_WIDTH = 3
CF_WIDTH = 31
ROUTER_ROWS = 8 + N_EXPERTS
SUBLANES = 8
LANES = 128
ROW_ALIGN = 16
BIG_CHUNK_ROWS = 64
CONV_ROWS = 16
VMEM_LIMIT = 56 * 1024 * 1024
MOE_BLOCK = 512
TOK_TILE = 1024
LOCAL_ROWS = 2 * TOK_TILE + N_EXPERTS * ROW_ALIGN
POS_ROWS = TOK_TILE // LANES
RUN_START_ROW, RUN_CHUNKS_ROW, RUN_DEST_ROW = 2 * POS_ROWS, 2 * POS_ROWS + 1, 2 * POS_ROWS + 2
PAD_START_ROW, PAD_CHUNKS_ROW = 2 * POS_ROWS + 3, 2 * POS_ROWS + 4
TABLE_ROWS = 2 * POS_ROWS + SUBLANES
BLOCK_LANES = 256


def _dot(a, b):
    return jnp.dot(a, b, preferred_element_type=F32)


def _layer_norm(z, g, b):
    mu = jnp.mean(z, axis=-1, keepdims=True)
    zc = z - mu
    var = jnp.mean(zc * zc, axis=-1, keepdims=True)
    return zc * lax.rsqrt(var + LN_EPS) * g + b


def _sigmoid(x):
    return 1.0 / (1.0 + jnp.exp(-x))


def _route(h1b, rw_ref, rb_ref):
    t = h1b.shape[0]
    logits = lax.dot_general(rw_ref[...], h1b, (((1,), (1,)), ((), ())),
                             preferred_element_type=F32) + rb_ref[...]
    g = [logits[i:i + 1, :] for i in range(N_GROUPS)]
    gmax = jnp.maximum(jnp.maximum(g[0], g[1]), jnp.maximum(g[2], g[3]))
    gidx = jnp.where(g[0] == gmax, 0, jnp.where(g[1] == gmax, 1, jnp.where(g[2] == gmax, 2, 3)))
    gsum = (jnp.exp(g[0] - gmax) + jnp.exp(g[1] - gmax)) + (jnp.exp(g[2] - gmax) + jnp.exp(g[3] - gmax))
    gp = 1.0 / gsum
    sel = logits[8 + 8 * (N_GROUPS - 1):8 + 8 * N_GROUPS, :]
    for gi in range(N_GROUPS - 2, -1, -1):
        sel = jnp.where(gidx == gi, logits[8 + 8 * gi:16 + 8 * gi, :], sel)
    row = lax.broadcasted_iota(I32, (EXPERTS_PER_GROUP, t), 0)
    m1 = jnp.max(sel, axis=0, keepdims=True)
    i1 = jnp.min(jnp.where(sel == m1, row, EXPERTS_PER_GROUP), axis=0, keepdims=True)
    sel2 = jnp.where(row == i1, -jnp.inf, sel)
    m2 = jnp.max(sel2, axis=0, keepdims=True)
    i2 = jnp.min(jnp.where(sel2 == m2, row, EXPERTS_PER_GROUP), axis=0, keepdims=True)
    d = jnp.exp(m2 - m1)
    w1 = 1.0 / (1.0 + d)
    w2 = d / (1.0 + d)
    return gidx * EXPERTS_PER_GROUP + i1, gidx * EXPERTS_PER_GROUP + i2, gp * w1, gp * w2


def _post_mixer(x, y, alpha, g_ref, b_ref, rw_ref, rb_ref, h1_ref, eid_ref, gate_ref):
    h1 = _layer_norm(alpha * x + y, g_ref[...], b_ref[...])
    h1_ref[...] = h1
    e1, e2, g1, g2 = _route(h1.astype(BF16), rw_ref, rb_ref)
    row8 = lax.broadcasted_iota(I32, (SUBLANES, h1.shape[0]), 0)
    eid_ref[...] = jnp.where(row8 == 0, e1, jnp.where(row8 == 1, e2, -1))
    gate_ref[...] = jnp.where(row8 == 0, g1, jnp.where(row8 == 1, g2, 0.0))


def _load_history(ubuf, hist_ref, rows, hist_rows):
    j = pl.program_id(1)

    @pl.when(j == 0)
    def _():
        ubuf[0:hist_rows, :] = hist_ref[...]

    @pl.when(j > 0)
    def _():
        ubuf[0:hist_rows, :] = ubuf[rows:rows + hist_rows, :]


def _position_major(x_ref, xin):
    nseq, npos, d = x_ref.shape
    for c in range(d // LANES):
        for s in range(nseq):
            xin[c, pl.ds(s, npos, stride=nseq), :] = x_ref[s, :, c * LANES:(c + 1) * LANES]
    return jnp.concatenate([xin[c] for c in range(d // LANES)], axis=1)


def _by_sequence(rows, o_ref, zout):
    nseq, npos, d = o_ref.shape
    for c in range(d // LANES):
        zout[c] = rows[:, c * LANES:(c + 1) * LANES]
    for c in range(d // LANES):
        for s in range(nseq):
            o_ref[s, :, c * LANES:(c + 1) * LANES] = zout[c, pl.ds(s, npos, stride=nseq), :]


def _mixer_a_kernel(x_ref, hist_ref, win_ref, cw_ref, wout_ref, g_ref, b_ref, rw_ref, rb_ref,
                    h1_ref, eid_ref, gate_ref, tail_ref, ubuf, *maybe_xin, alpha, nseq):
    rows, d = h1_ref.shape
    hist_rows = (SC_WIDTH - 1) * nseq
    _load_history(ubuf, hist_ref, rows, hist_rows)
    x = _position_major(x_ref, *maybe_xin) if maybe_xin else x_ref[...]
    xb = x.astype(BF16)
    ubuf[hist_rows:hist_rows + rows, :] = _dot(xb, win_ref[:, d:2 * d]) * _dot(xb, win_ref[:, 2 * d:3 * d])
    tail_ref[...] = ubuf[rows:rows + hist_rows, :]
    conv = cw_ref[0:1, :] * ubuf[0:rows, :]
    for k in range(1, SC_WIDTH):
        conv = conv + cw_ref[k:k + 1, :] * ubuf[k * nseq:k * nseq + rows, :]
    bg = _dot(xb, win_ref[:, 0:d])
    y = _dot((bg * conv).astype(BF16), wout_ref[...])
    _post_mixer(x, y, alpha, g_ref, b_ref, rw_ref, rb_ref, h1_ref, eid_ref, gate_ref)


def _mixer_b_kernel(x_ref, hist_ref, w1_ref, b1_ref, cw_ref, cb_ref, lg_ref, lb_ref, w2_ref, b2_ref,
                    g_ref, b_ref, rw_ref, rb_ref,
                    h1_ref, eid_ref, gate_ref, tail_ref, ubuf, cbuf, *, alpha, nseq):
    rows, d = x_ref.shape
    hist_rows = (CF_WIDTH - 1) * nseq
    _load_history(ubuf, hist_ref, rows, hist_rows)
    x = x_ref[...]
    xb = x.astype(BF16)
    a = _dot(xb, w1_ref[:, 0:d]) + b1_ref[:, 0:d]
    gl = _dot(xb, w1_ref[:, d:2 * d]) + b1_ref[:, d:2 * d]
    ubuf[hist_rows:hist_rows + rows, :] = a * _sigmoid(gl)
    tail_ref[...] = ubuf[rows:rows + hist_rows, :]
    halves = CONV_ROWS // SUBLANES

    def chunk(c, carry):
        r0 = pl.multiple_of(c * CONV_ROWS, CONV_ROWS)
        acc = [jnp.broadcast_to(cb_ref[...], (SUBLANES, d)) for _ in range(halves)]
        for k in range(CF_WIDTH):
            w = cw_ref[k * SUBLANES:(k + 1) * SUBLANES, :]
            for h in range(halves):
                start = pl.multiple_of(r0 + h * SUBLANES + k * nseq, SUBLANES)
                acc[h] = acc[h] + w * ubuf[pl.ds(start, SUBLANES), :]
        for h in range(halves):
            cbuf[pl.ds(pl.multiple_of(r0 + h * SUBLANES, SUBLANES), SUBLANES), :] = acc[h]
        return carry

    lax.fori_loop(0, rows // CONV_ROWS, chunk, 0)
    cn = _layer_norm(cbuf[...], lg_ref[...], lb_ref[...])
    cn = cn * _sigmoid(cn)
    y = _dot(cn.astype(BF16), w2_ref[...]) + b2_ref[...]
    _post_mixer(x, y, alpha, g_ref, b_ref, rw_ref, rb_ref, h1_ref, eid_ref, gate_ref)


def _const_spec(shape):
    nd = len(shape)
    return pl.BlockSpec(shape, lambda *_: (0,) * nd, pipeline_mode=pl.Buffered(1))


def _mixer(kind, x2d, hist2d, p, *, pos_tile, seq_tile, nseq, alpha, by_sequence=False):
    d = x2d.shape[-1]
    n = x2d.size // d
    width = SC_WIDTH if kind == "a" else CF_WIDTH
    rows = pos_tile * seq_tile
    hist_rows = (width - 1) * seq_tile
    seq_blocks = nseq // seq_tile
    steps = n // (rows * seq_blocks)
    if kind == "a":
        kern = functools.partial(_mixer_a_kernel, alpha=alpha, nseq=seq_tile)
        weights = (p["w_in"], p["conv_w"], p["w_out"])
        wspecs = [_const_spec((d, 3 * d)), _const_spec((8, d)), _const_spec((d, d))]
        scratch = [pltpu.VMEM((rows + hist_rows, d), F32)]
        if by_sequence:
            scratch.append(pltpu.VMEM((d // LANES, rows, LANES), F32))
    else:
        assert not by_sequence
        kern = functools.partial(_mixer_b_kernel, alpha=alpha, nseq=seq_tile)
        weights = (p["w_pw1"], p["b_pw1"], p["conv_w"], p["conv_b"], p["cln_g"], p["cln_b"], p["w_pw2"], p["b_pw2"])
        wspecs = [_const_spec((d, 2 * d)), _const_spec((1, 2 * d)), _const_spec((CF_WIDTH * SUBLANES, d)),
                  _const_spec((1, d)), _const_spec((1, d)), _const_spec((1, d)), _const_spec((d, d)),
                  _const_spec((1, d))]
        scratch = [pltpu.VMEM((rows + hist_rows, d), F32), pltpu.VMEM((rows, d), F32)]
    common = (p["ln1_g"], p["ln1_b"], p["rw"], p["rb"])
    cspecs = [_const_spec((1, d)), _const_spec((1, d)), _const_spec((ROUTER_ROWS, d)), _const_spec((ROUTER_ROWS, 1))]
    if by_sequence:
        assert seq_blocks == 1
        x_spec = pl.BlockSpec((nseq, pos_tile, d), lambda s, j: (0, j, 0))
    else:
        x_spec = pl.BlockSpec((rows, d), lambda s, j: (s * steps + j, 0))
    return pl.pallas_call(
        kern,
        grid=(seq_blocks, steps),
        in_specs=[x_spec, pl.BlockSpec((hist_rows, d), lambda s, j: (s, 0))] + wspecs + cspecs,
        out_specs=[
            pl.BlockSpec((rows, d), lambda s, j: (s * steps + j, 0)),
            pl.BlockSpec((SUBLANES, rows), lambda s, j: (0, s * steps + j)),
            pl.BlockSpec((SUBLANES, rows), lambda s, j: (0, s * steps + j)),
            pl.BlockSpec((hist_rows, d), lambda s, j: (s, 0)),
        ],
        out_shape=[
            jax.ShapeDtypeStruct((n, d), F32),
            jax.ShapeDtypeStruct((SUBLANES, n), I32),
            jax.ShapeDtypeStruct((SUBLANES, n), F32),
            jax.ShapeDtypeStruct((seq_blocks * hist_rows, d), F32),
        ],
        scratch_shapes=scratch,
        compiler_params=pltpu.CompilerParams(
            dimension_semantics=("arbitrary", "arbitrary"), vmem_limit_bytes=VMEM_LIMIT),
        name="mixer_" + kind,
    )(x2d, hist2d, *weights, *common)


def _rank_kernel(eid_ref, tab_ref, blk_ref, tri, tot, prior, pad_start, pad_chunks, *, total_rows):
    phase, i = pl.program_id(0), pl.program_id(1)
    t = eid_ref.shape[1]
    row = lax.broadcasted_iota(I32, (N_EXPERTS, t), 0)
    oh1 = (row == eid_ref[0:1, :]).astype(F32)
    oh2 = (row == eid_ref[1:2, :]).astype(F32)
    c1 = jnp.sum(oh1, axis=1, keepdims=True)
    c2 = jnp.sum(oh2, axis=1, keepdims=True)
    chunks = jnp.ceil((c1 + c2) * (1.0 / ROW_ALIGN))
    tab_ref[...] = jnp.zeros_like(tab_ref)

    @pl.when(jnp.logical_and(phase == 0, i == 0))
    def _():
        tot[...] = jnp.zeros_like(tot)
        a = lax.broadcasted_iota(I32, tri.shape, 0)
        b = lax.broadcasted_iota(I32, tri.shape, 1)
        tri[...] = (a <= b).astype(BF16)

    @pl.when(phase == 0)
    def _():
        tot[...] += chunks * ROW_ALIGN

    @pl.when(jnp.logical_and(phase == 1, i == 0))
    def _():
        cnt = tot[...].astype(I32)
        seg = ((cnt + (MOE_BLOCK - 1)) // MOE_BLOCK) * MOE_BLOCK
        r128 = lax.broadcasted_iota(I32, cnt.shape, 0)
        base = jnp.zeros_like(cnt)
        for e in range(N_EXPERTS - 1):
            base = base + jnp.where(r128 > e, seg[e:e + 1, :], 0)
        prior[...] = base.astype(F32)
        pad_start[...] = (base + cnt).astype(F32)
        pad_end = jnp.where(r128 == N_EXPERTS - 1, total_rows, base + seg)
        pad_chunks[...] = ((pad_end - base - cnt) // ROW_ALIGN).astype(F32)
        end_blk = (base + seg) // MOE_BLOCK
        lane = lax.broadcasted_iota(I32, (N_EXPERTS, BLOCK_LANES), 1)
        expert_of = jnp.sum((end_blk[:, 0:1] <= lane).astype(F32), axis=0, keepdims=True).astype(I32)
        active = (lane[0:1, :] < end_blk[N_EXPERTS - 1:N_EXPERTS, 0:1]).astype(I32)
        r8 = lax.broadcasted_iota(I32, (SUBLANES, BLOCK_LANES), 0)
        blk_ref[...] = jnp.where(r8 == 0, jnp.minimum(expert_of, N_EXPERTS - 1), jnp.where(r8 == 1, active, 0))

    @pl.when(phase == 1)
    def _():
        cs = _dot(jnp.concatenate([oh1, oh2], axis=0).astype(BF16), tri[...])
        ea = lax.broadcasted_iota(I32, (N_EXPERTS, N_EXPERTS), 0)
        eb = lax.broadcasted_iota(I32, (N_EXPERTS, N_EXPERTS), 1)
        chunks_b = jnp.broadcast_to(chunks, (N_EXPERTS, LANES))
        start = _dot((eb < ea).astype(BF16), chunks_b.astype(BF16)) * ROW_ALIGN
        s0 = start[:, 0:1]
        pos1 = jnp.sum(oh1 * (s0 + cs[0:N_EXPERTS, :] - 1.0), axis=0, keepdims=True).astype(I32)
        pos2 = jnp.sum(oh2 * (s0 + c1 + cs[N_EXPERTS:2 * N_EXPERTS, :] - 1.0), axis=0, keepdims=True).astype(I32)
        for q in range(POS_ROWS):
            tab_ref[q:q + 1, :] = pos1[:, q * LANES:(q + 1) * LANES]
            tab_ref[POS_ROWS + q:POS_ROWS + q + 1, :] = pos2[:, q * LANES:(q + 1) * LANES]
        diag = lax.broadcasted_iota(I32, (N_EXPERTS, LANES), 0) == lax.broadcasted_iota(I32, (N_EXPERTS, LANES), 1)
        to_lanes = lambda col: jnp.sum(jnp.where(diag, col, 0.0), axis=0, keepdims=True).astype(I32)
        tab_ref[RUN_START_ROW:RUN_START_ROW + 1, :] = to_lanes(start)
        tab_ref[RUN_CHUNKS_ROW:RUN_CHUNKS_ROW + 1, :] = to_lanes(chunks_b)
        tab_ref[RUN_DEST_ROW:RUN_DEST_ROW + 1, :] = to_lanes(prior[...])
        tab_ref[PAD_START_ROW:PAD_START_ROW + 1, :] = to_lanes(pad_start[...])
        tab_ref[PAD_CHUNKS_ROW:PAD_CHUNKS_ROW + 1, :] = to_lanes(pad_chunks[...])
        prior[...] += chunks * ROW_ALIGN


def _rank(eid, total_rows):
    n = eid.shape[1]
    tiles = n // TOK_TILE
    return pl.pallas_call(
        functools.partial(_rank_kernel, total_rows=total_rows),
        grid=(2, tiles),
        in_specs=[pl.BlockSpec((SUBLANES, TOK_TILE), lambda ph, i: (0, i))],
        out_specs=[
            pl.BlockSpec((None, TABLE_ROWS, LANES), lambda ph, i: (i * ph, 0, 0)),
            pl.BlockSpec((SUBLANES, BLOCK_LANES), lambda ph, i: (0, 0)),
        ],
        out_shape=[
            jax.ShapeDtypeStruct((tiles, TABLE_ROWS, LANES), I32),
            jax.ShapeDtypeStruct((SUBLANES, BLOCK_LANES), I32),
        ],
        scratch_shapes=[pltpu.VMEM((TOK_TILE, TOK_TILE), BF16)] + [pltpu.VMEM((N_EXPERTS, LANES), F32)] * 4,
        compiler_params=pltpu.CompilerParams(
            dimension_semantics=("arbitrary", "arbitrary"), vmem_limit_bytes=VMEM_LIMIT),
        name="rank",
    )(eid)


def _token_rows(ref, tok):
    return ref.at[pl.ds(pl.multiple_of(tok * SUBLANES, SUBLANES), SUBLANES), :]


def _for_each_chunk(tab, fn, rows=(RUN_START_ROW, RUN_CHUNKS_ROW, RUN_DEST_ROW)):
    start_row, chunks_row, dest_row = rows
    per_big = BIG_CHUNK_ROWS // ROW_ALIGN

    def per_expert(e, carry):
        start, dest, chunks = tab[start_row, e], tab[dest_row, e], tab[chunks_row, e]
        nbig = chunks // per_big

        def piece(nrows):
            def body(q, c2):
                fn(pl.multiple_of(start + q * nrows, ROW_ALIGN), pl.multiple_of(dest + q * nrows, ROW_ALIGN), nrows)
                return c2
            return body

        lax.fori_loop(0, nbig, piece(BIG_CHUNK_ROWS), 0)
        lax.fori_loop(nbig * per_big, chunks, piece(ROW_ALIGN), 0)
        return carry

    lax.fori_loop(0, N_EXPERTS, per_expert, 0)


def _dispatch_kernel(*refs, tile_base, aliased, fill_pads):
    if aliased:
        tab_hbm, h_ref, _, xs_hbm, tab, prev, tokbuf, loc, xl, zrows, sem_tab, sem_row = refs
    else:
        tab_hbm, h_ref, xs_hbm, tab, prev, tokbuf, loc, xl, zrows, sem_tab, sem_row = refs
    t = h_ref.shape[0]
    i = pl.program_id(0)
    cp = pltpu.make_async_copy(tab_hbm.at[tile_base + i], tab, sem_tab)
    cp.start()
    for c in range(SUBLANES):
        tokbuf[pl.ds(c, t, stride=SUBLANES), :] = h_ref[:, c * LANES:(c + 1) * LANES]

    @pl.when(i == 0)
    def _():
        loc[...] = jnp.zeros_like(loc)
        for e in range(N_EXPERTS):
            prev[0, e] = 0

    cp.wait()

    def place(q, carry):
        for lane in range(LANES):
            v = _token_rows(tokbuf, q * LANES + lane)[...]
            _token_rows(loc, tab[q, lane])[...] = v
            _token_rows(loc, tab[POS_ROWS + q, lane])[...] = v
        return carry

    lax.fori_loop(0, t // LANES, place, 0)

    def chunk_copy(local_row, global_row, nrows):
        return pltpu.make_async_copy(xl.at[pl.ds(local_row, nrows), :],
                                     xs_hbm.at[pl.ds(global_row, nrows), :], sem_row)

    prev_rows = (0, 1, 0)

    @pl.when(i > 0)
    def _():
        _for_each_chunk(prev, lambda *a: chunk_copy(*a).wait(), prev_rows)

    for c in range(SUBLANES):
        xl[:, c * LANES:(c + 1) * LANES] = loc[pl.ds(c, LOCAL_ROWS, stride=SUBLANES), :].astype(BF16)
    _for_each_chunk(tab, lambda *a: chunk_copy(*a).start())
    for e in range(N_EXPERTS):
        prev[1, e] = tab[RUN_CHUNKS_ROW, e]

    @pl.when(i == pl.num_programs(0) - 1)
    def _():
        _for_each_chunk(prev, lambda *a: chunk_copy(*a).wait(), prev_rows)

    if fill_pads:
        @pl.when(i == pl.num_programs(0) - 1)
        def _():
            zrows[...] = jnp.zeros_like(zrows)
            zero_copy = lambda _, g, nrows: pltpu.make_async_copy(
                zrows.at[pl.ds(0, nrows), :], xs_hbm.at[pl.ds(g, nrows), :], sem_row)
            pad_rows = (PAD_START_ROW, PAD_CHUNKS_ROW, PAD_START_ROW)
            _for_each_chunk(tab, lambda *a: zero_copy(*a).start(), pad_rows)
            _for_each_chunk(tab, lambda *a: zero_copy(*a).wait(), pad_rows)


def _dispatch(tabs, h2d, xs, *, tile, tile_base, sorted_rows, fill_pads):
    n, d = h2d.shape
    aliased = xs is not None
    kern = functools.partial(_dispatch_kernel, tile_base=tile_base, aliased=aliased, fill_pads=fill_pads)
    in_specs = [pl.BlockSpec(memory_space=pl.ANY), pl.BlockSpec((tile, d), lambda i: (i, 0))]
    args = [tabs, h2d]
    if aliased:
        in_specs.append(pl.BlockSpec(memory_space=pl.ANY))
        args.append(xs)
    return pl.pallas_call(
        kern,
        grid=(n // tile,),
        in_specs=in_specs,
        out_specs=pl.BlockSpec(memory_space=pl.ANY),
        out_shape=jax.ShapeDtypeStruct((sorted_rows, d), BF16),
        scratch_shapes=[pltpu.SMEM((TABLE_ROWS, LANES), I32), pltpu.SMEM((SUBLANES, LANES), I32),
                        pltpu.VMEM((tile * SUBLANES, LANES), F32),
                        pltpu.VMEM((LOCAL_ROWS * SUBLANES, LANES), F32), pltpu.VMEM((LOCAL_ROWS, d), BF16),
                        pltpu.VMEM((BIG_CHUNK_ROWS, d), BF16), pltpu.SemaphoreType.DMA, pltpu.SemaphoreType.DMA],
        input_output_aliases={2: 0} if aliased else {},
        compiler_params=pltpu.CompilerParams(
            dimension_semantics=("arbitrary",), vmem_limit_bytes=VMEM_LIMIT),
        name="dispatch",
    )(*args)


def _combine_kernel(tab_hbm, gate_hbm, ys_hbm, h_ref, g_ref, b_ref, o_ref,
                    tabs, gates, yl, ytok, ztok, sem_tab, sem_gate, sem_row, *maybe_zout, tile_base, alpha):
    t = h_ref.shape[0]
    i = pl.program_id(0)
    last = pl.num_programs(0) - 1
    slot = i % 2
    tab = tabs.at[slot]
    table_copy = lambda tile, s: pltpu.make_async_copy(tab_hbm.at[tile_base + tile], tabs.at[s], sem_tab.at[s])
    cp_gate = pltpu.make_async_copy(gate_hbm.at[:, pl.ds(pl.multiple_of(i * t, LANES), t)], gates, sem_gate)

    def chunk_copy(local_row, global_row, nrows):
        return pltpu.make_async_copy(ys_hbm.at[pl.ds(global_row, nrows), :],
                                     yl.at[pl.ds(local_row, nrows), :], sem_row)

    @pl.when(i == 0)
    def _():
        yl[...] = jnp.zeros_like(yl)
        table_copy(0, 0).start()
        table_copy(0, 0).wait()
        _for_each_chunk(tabs.at[0], lambda *a: chunk_copy(*a).start())

    @pl.when(i < last)
    def _():
        table_copy(i + 1, 1 - slot).start()

    cp_gate.start()
    _for_each_chunk(tab, lambda *a: chunk_copy(*a).wait())
    for c in range(SUBLANES):
        ytok[pl.ds(c, LOCAL_ROWS, stride=SUBLANES), :] = yl[:, c * LANES:(c + 1) * LANES].astype(F32)

    @pl.when(i < last)
    def _():
        table_copy(i + 1, 1 - slot).wait()
        _for_each_chunk(tabs.at[1 - slot], lambda *a: chunk_copy(*a).start())

    cp_gate.wait()

    def gather(q, carry):
        for lane in range(LANES):
            n = q * LANES + lane
            _token_rows(ztok, n)[...] = (gates[0, n] * _token_rows(ytok, tab[q, lane])[...]
                                         + gates[1, n] * _token_rows(ytok, tab[POS_ROWS + q, lane])[...])
        return carry

    lax.fori_loop(0, t // LANES, gather, 0)
    f = jnp.concatenate([ztok[pl.ds(c, t, stride=SUBLANES), :] for c in range(SUBLANES)], axis=1)
    out = _layer_norm(alpha * h_ref[...] + f, g_ref[...], b_ref[...])
    if maybe_zout:
        _by_sequence(out, o_ref, *maybe_zout)
    else:
        o_ref[...] = out


def _combine(tabs, gate, ys, h2d, p, *, tile, tile_base, alpha, out_sequences=None):
    n, d = h2d.shape
    kern = functools.partial(_combine_kernel, tile_base=tile_base, alpha=alpha)
    scratch = [pltpu.SMEM((2, TABLE_ROWS, LANES), I32), pltpu.SMEM((SUBLANES, tile), F32),
               pltpu.VMEM((LOCAL_ROWS, d), BF16), pltpu.VMEM((LOCAL_ROWS * SUBLANES, LANES), F32),
               pltpu.VMEM((tile * SUBLANES, LANES), F32),
               pltpu.SemaphoreType.DMA((2,)), pltpu.SemaphoreType.DMA, pltpu.SemaphoreType.DMA]
    if out_sequences:
        npos = tile // out_sequences
        out_spec = pl.BlockSpec((out_sequences, npos, d), lambda i: (0, i, 0))
        out_shape = jax.ShapeDtypeStruct((out_sequences, n // out_sequences, d), F32)
        scratch.append(pltpu.VMEM((d // LANES, tile, LANES), F32))
    else:
        out_spec = pl.BlockSpec((tile, d), lambda i: (i, 0))
        out_shape = jax.ShapeDtypeStruct((n, d), F32)
    return pl.pallas_call(
        kern,
        grid=(n // tile,),
        in_specs=[
            pl.BlockSpec(memory_space=pl.ANY), pl.BlockSpec(memory_space=pl.ANY), pl.BlockSpec(memory_space=pl.ANY),
            pl.BlockSpec((tile, d), lambda i: (i, 0)),
            _const_spec((1, d)), _const_spec((1, d)),
        ],
        out_specs=out_spec,
        out_shape=out_shape,
        scratch_shapes=scratch,
        compiler_params=pltpu.CompilerParams(
            dimension_semantics=("arbitrary",), vmem_limit_bytes=VMEM_LIMIT),
        name="combine",
    )(tabs, gate, ys, h2d, p["ln2_g"], p["ln2_b"])


def _moe_kernel(blk_ref, x_ref, wg_ref, wu_ref, wd_ref, y_ref, wgb, wub, wdb):
    b = pl.program_id(0)

    @pl.when(jnp.logical_or(b == 0, blk_ref[0, b] != blk_ref[0, jnp.maximum(b - 1, 0)]))
    def _():
        wgb[...] = wg_ref[...].astype(BF16)
        wub[...] = wu_ref[...].astype(BF16)
        wdb[...] = wd_ref[...].astype(BF16)

    @pl.when(blk_ref[1, b] == 1)
    def _():
        x = x_ref[...]
        hg = _dot(x, wgb[...])
        hu = _dot(x, wub[...])
        hid = (hg * _sigmoid(hg) * hu).astype(BF16)
        y_ref[...] = _dot(hid, wdb[...]).astype(BF16)

    @pl.when(blk_ref[1, b] == 0)
    def _():
        y_ref[...] = jnp.zeros_like(y_ref)


def _moe(blk, xs, wg, wu, wd, layer):
    rows, d = xs.shape
    nblocks = rows // MOE_BLOCK
    f = wg.shape[-1]
    block = lambda b, blk: (jnp.where(blk[1, b] == 1, b, nblocks - 1), 0)
    return pl.pallas_call(
        _moe_kernel,
        grid_spec=pltpu.PrefetchScalarGridSpec(
            num_scalar_prefetch=1,
            grid=(nblocks,),
            in_specs=[
                pl.BlockSpec((MOE_BLOCK, d), block),
                pl.BlockSpec((None, None, d, f), lambda b, blk: (layer, blk[0, b], 0, 0)),
                pl.BlockSpec((None, None, d, f), lambda b, blk: (layer, blk[0, b], 0, 0)),
                pl.BlockSpec((None, None, f, d), lambda b, blk: (layer, blk[0, b], 0, 0)),
            ],
            out_specs=pl.BlockSpec((MOE_BLOCK, d), lambda b, blk: (b, 0)),
            scratch_shapes=[pltpu.VMEM((d, f), BF16), pltpu.VMEM((d, f), BF16), pltpu.VMEM((f, d), BF16)],
        ),
        out_shape=jax.ShapeDtypeStruct((rows, d), BF16),
        compiler_params=pltpu.CompilerParams(
            dimension_semantics=("arbitrary",), vmem_limit_bytes=VMEM_LIMIT),
        name="moe",
    )(blk, xs, wg, wu, wd)


def _sparse_moe(sets, p, *, layer, alpha, out_sequences=None):
    eids, bases, tiles = [], [], 0
    for h, e, _ in sets:
        n = h.shape[0]
        n_pad = -(-n // TOK_TILE) * TOK_TILE
        eids.append(jnp.concatenate([e, jnp.full((SUBLANES, n_pad - n), -1, I32)], axis=1))
        bases.append(tiles)
        tiles += n_pad // TOK_TILE
    n_tok = sum(h.shape[0] for h, _, _ in sets)
    run_pad = tiles * N_EXPERTS * (ROW_ALIGN - 1)
    nblocks = -(-(2 * n_tok + run_pad + N_EXPERTS * (MOE_BLOCK - 1)) // MOE_BLOCK)
    assert nblocks <= BLOCK_LANES
    sorted_rows = nblocks * MOE_BLOCK
    tabs, blk = _rank(jnp.concatenate(eids, axis=1), sorted_rows)
    xs = None
    for k, ((h, _, _), tb) in enumerate(zip(sets, bases)):
        xs = _dispatch(tabs, h, xs, tile=min(TOK_TILE, h.shape[0]), tile_base=tb, sorted_rows=sorted_rows,
                       fill_pads=k == len(sets) - 1)
    ys = _moe(blk, xs, p["w_gate"], p["w_up"], p["w_down"], layer)
    return [_combine(tabs, g, ys, h, p, tile=min(TOK_TILE, h.shape[0]), tile_base=tb, alpha=alpha,
                     out_sequences=out_sequences if k == 0 else None)
            for k, ((h, _, g), tb) in enumerate(zip(sets, bases))]


def _pad_rows(w, rows):
    return jnp.concatenate([w, jnp.zeros((rows - w.shape[0],) + w.shape[1:], w.dtype)], axis=0)


def _router_params(gw, gb, ew, eb):
    d = gw.shape[0]
    rw = jnp.concatenate([gw.T, jnp.zeros((8 - N_GROUPS, d), gw.dtype), ew.T], axis=0).astype(BF16)
    rb = jnp.concatenate([gb, jnp.zeros((8 - N_GROUPS,), gb.dtype), eb])[:, None].astype(F32)
    return rw, rb


def _to_blocks(x, seq_tile):
    nseq, npos, d = x.shape
    return x.reshape(nseq // seq_tile, seq_tile, npos, d).swapaxes(1, 2).reshape(nseq * npos, d)


def _from_blocks(x2d, nseq, seq_tile):
    n, d = x2d.shape
    npos = n // nseq
    return x2d.reshape(nseq // seq_tile, npos, seq_tile, d).swapaxes(1, 2).reshape(nseq, npos, d)


def kernel(x_prompt, x_sample, state_conv_a, state_conv_b, meta_tokens, sc_w_in, sc_conv_w, sc_w_out, cf_w_pw1, cf_b_pw1, cf_conv_w, cf_conv_b, cf_ln_g, cf_ln_b, cf_w_pw2, cf_b_pw2, ln1_g, ln1_b, ln2_g, ln2_b, rt_group_w, rt_group_b, rt_expert_w, rt_expert_b, moe_w_gate, moe_w_up, moe_w_down):
    bsz, seq, d = x_prompt.shape
    dec_b, dec_t, _ = x_sample.shape
    n_meta = meta_tokens.shape[0]
    depth = ln1_g.shape[0]
    alpha = float((2 * depth) ** 0.25)
    row = lambda v: v[None, :].astype(F32)
    s_seq = 32
    p_pos = 512 // bsz

    layer = []
    for i in range(depth):
        rw, rb = _router_params(rt_group_w[i], rt_group_b[i], rt_expert_w[i], rt_expert_b[i])
        layer.append(dict(
            ln1_g=row(ln1_g[i]), ln1_b=row(ln1_b[i]), ln2_g=row(ln2_g[i]), ln2_b=row(ln2_b[i]), rw=rw, rb=rb,
            w_gate=moe_w_gate, w_up=moe_w_up, w_down=moe_w_down))
    pa = dict(layer[0], w_in=sc_w_in[0].astype(BF16), conv_w=_pad_rows(sc_conv_w[0], 8),
              w_out=sc_w_out[0].astype(BF16))
    pb = dict(layer[1], w_pw1=cf_w_pw1[0].astype(BF16), b_pw1=row(cf_b_pw1[0]),
              conv_w=jnp.repeat(cf_conv_w[0].astype(F32), SUBLANES, axis=0), conv_b=row(cf_conv_b[0]),
              cln_g=row(cf_ln_g[0]), cln_b=row(cf_ln_b[0]), w_pw2=cf_w_pw2[0].astype(BF16), b_pw2=row(cf_b_pw2[0]))

    xs = _to_blocks(x_sample, s_seq)
    xm = jnp.broadcast_to(meta_tokens.astype(F32)[:, None, :], (n_meta, bsz, d)).reshape(n_meta * bsz, d)
    mix_p = dict(pos_tile=p_pos, seq_tile=bsz, nseq=bsz, alpha=alpha)
    mix_s = dict(pos_tile=dec_t, seq_tile=s_seq, nseq=dec_b, alpha=alpha)
    mix_m = dict(pos_tile=n_meta, seq_tile=bsz, nseq=bsz, alpha=alpha)

    zero_a = jnp.zeros(((SC_WIDTH - 1) * bsz, d), F32)
    hm, em, gm, tail_am = _mixer("a", xm, zero_a, pa, **mix_m)
    hp, ep, gp, tail_ap = _mixer("a", x_prompt, tail_am, pa, by_sequence=True, **mix_p)
    hs, es, gs, tail_as = _mixer("a", xs, _to_blocks(state_conv_a[0], s_seq), pa, **mix_s)
    hp, hs, hm = _sparse_moe([(hp, ep, gp), (hs, es, gs), (hm, em, gm)], pa, layer=0, alpha=alpha)

    zero_b = jnp.zeros(((CF_WIDTH - 1) * bsz, d), F32)
    _, _, _, tail_bm = _mixer("b", hm, zero_b, pb, **mix_m)
    hp, ep, gp, tail_bp = _mixer("b", hp, tail_bm, pb, **mix_p)
    hs, es, gs, tail_bs = _mixer("b", hs, _to_blocks(state_conv_b[0], s_seq), pb, **mix_s)
    y_prompt, hs = _sparse_moe([(hp, ep, gp), (hs, es, gs)], pb, layer=1, alpha=alpha, out_sequences=bsz)

    return (y_prompt, _from_blocks(hs, dec_b, s_seq),
            _from_blocks(tail_ap, bsz, bsz)[None], _from_blocks(tail_bp, bsz, bsz)[None],
            _from_blocks(tail_as, dec_b, s_seq)[None], _from_blocks(tail_bs, dec_b, s_seq)[None])
```

```python
import functools

import jax
import jax.numpy as jnp
from jax import lax
from jax.experimental import pallas as pl
from jax.experimental.pallas import tpu as pltpu

F32 = jnp.float32
BF16 = jnp.bfloat16
I32 = jnp.int32

LN_EPS = 1e-5
N_GROUPS = 4
EXPERTS_PER_GROUP = 8
N_EXPERTS = N_GROUPS * EXPERTS_PER_GROUP
SC_WIDTH = 3
CF_WIDTH = 31
ROUTER_ROWS = 8 + N_EXPERTS
SUBLANES = 8
LANES = 128
ROW_ALIGN = 16
BIG_CHUNK_ROWS = 64
CONV_ROWS = 16
VMEM_LIMIT = 56 * 1024 * 1024
MOE_BLOCK = 512
TOK_TILE = 1024
LOCAL_ROWS = 2 * TOK_TILE + N_EXPERTS * ROW_ALIGN
POS_ROWS = TOK_TILE // LANES
RUN_START_ROW, RUN_CHUNKS_ROW, RUN_DEST_ROW = 2 * POS_ROWS, 2 * POS_ROWS + 1, 2 * POS_ROWS + 2
PAD_START_ROW, PAD_CHUNKS_ROW = 2 * POS_ROWS + 3, 2 * POS_ROWS + 4
TABLE_ROWS = 2 * POS_ROWS + SUBLANES
BLOCK_LANES = 256


def _dot(a, b):
    return jnp.dot(a, b, preferred_element_type=F32)


def _layer_norm(z, g, b):
    mu = jnp.mean(z, axis=-1, keepdims=True)
    zc = z - mu
    var = jnp.mean(zc * zc, axis=-1, keepdims=True)
    return zc * lax.rsqrt(var + LN_EPS) * g + b


def _sigmoid(x):
    return 1.0 / (1.0 + jnp.exp(-x))


def _route(h1b, rw_ref, rb_ref):
    t = h1b.shape[0]
    logits = lax.dot_general(rw_ref[...], h1b, (((1,), (1,)), ((), ())),
                             preferred_element_type=F32) + rb_ref[...]
    g = [logits[i:i + 1, :] for i in range(N_GROUPS)]
    gmax = jnp.maximum(jnp.maximum(g[0], g[1]), jnp.maximum(g[2], g[3]))
    gidx = jnp.where(g[0] == gmax, 0, jnp.where(g[1] == gmax, 1, jnp.where(g[2] == gmax, 2, 3)))
    gsum = (jnp.exp(g[0] - gmax) + jnp.exp(g[1] - gmax)) + (jnp.exp(g[2] - gmax) + jnp.exp(g[3] - gmax))
    gp = 1.0 / gsum
    sel = logits[8 + 8 * (N_GROUPS - 1):8 + 8 * N_GROUPS, :]
    for gi in range(N_GROUPS - 2, -1, -1):
        sel = jnp.where(gidx == gi, logits[8 + 8 * gi:16 + 8 * gi, :], sel)
    row = lax.broadcasted_iota(I32, (EXPERTS_PER_GROUP, t), 0)
    m1 = jnp.max(sel, axis=0, keepdims=True)
    i1 = jnp.min(jnp.where(sel == m1, row, EXPERTS_PER_GROUP), axis=0, keepdims=True)
    sel2 = jnp.where(row == i1, -jnp.inf, sel)
    m2 = jnp.max(sel2, axis=0, keepdims=True)
    i2 = jnp.min(jnp.where(sel2 == m2, row, EXPERTS_PER_GROUP), axis=0, keepdims=True)
    d = jnp.exp(m2 - m1)
    w1 = 1.0 / (1.0 + d)
    w2 = d / (1.0 + d)
    return gidx * EXPERTS_PER_GROUP + i1, gidx * EXPERTS_PER_GROUP + i2, gp * w1, gp * w2


def _post_mixer(x, y, alpha, g_ref, b_ref, rw_ref, rb_ref, h1_ref, eid_ref, gate_ref):
    h1 = _layer_norm(alpha * x + y, g_ref[...], b_ref[...])
    h1_ref[...] = h1
    e1, e2, g1, g2 = _route(h1.astype(BF16), rw_ref, rb_ref)
    row8 = lax.broadcasted_iota(I32, (SUBLANES, h1.shape[0]), 0)
    eid_ref[...] = jnp.where(row8 == 0, e1, jnp.where(row8 == 1, e2, -1))
    gate_ref[...] = jnp.where(row8 == 0, g1, jnp.where(row8 == 1, g2, 0.0))


def _load_history(ubuf, hist_ref, rows, hist_rows):
    j = pl.program_id(1)

    @pl.when(j == 0)
    def _():
        ubuf[0:hist_rows, :] = hist_ref[...]

    @pl.when(j > 0)
    def _():
        ubuf[0:hist_rows, :] = ubuf[rows:rows + hist_rows, :]


def _position_major(x_ref, xin):
    nseq, npos, d = x_ref.shape
    for c in range(d // LANES):
        for s in range(nseq):
            xin[c, pl.ds(s, npos, stride=nseq), :] = x_ref[s, :, c * LANES:(c + 1) * LANES]
    return jnp.concatenate([xin[c] for c in range(d // LANES)], axis=1)


def _by_sequence(rows, o_ref, zout):
    nseq, npos, d = o_ref.shape
    for c in range(d // LANES):
        zout[c] = rows[:, c * LANES:(c + 1) * LANES]
    for c in range(d // LANES):
        for s in range(nseq):
            o_ref[s, :, c * LANES:(c + 1) * LANES] = zout[c, pl.ds(s, npos, stride=nseq), :]


def _mixer_a_kernel(x_ref, hist_ref, win_ref, cw_ref, wout_ref, g_ref, b_ref, rw_ref, rb_ref,
                    h1_ref, eid_ref, gate_ref, tail_ref, ubuf, *maybe_xin, alpha, nseq):
    rows, d = h1_ref.shape
    hist_rows = (SC_WIDTH - 1) * nseq
    _load_history(ubuf, hist_ref, rows, hist_rows)
    x = _position_major(x_ref, *maybe_xin) if maybe_xin else x_ref[...]
    xb = x.astype(BF16)
    ubuf[hist_rows:hist_rows + rows, :] = _dot(xb, win_ref[:, d:2 * d]) * _dot(xb, win_ref[:, 2 * d:3 * d])
    tail_ref[...] = ubuf[rows:rows + hist_rows, :]
    conv = cw_ref[0:1, :] * ubuf[0:rows, :]
    for k in range(1, SC_WIDTH):
        conv = conv + cw_ref[k:k + 1, :] * ubuf[k * nseq:k * nseq + rows, :]
    bg = _dot(xb, win_ref[:, 0:d])
    y = _dot((bg * conv).astype(BF16), wout_ref[...])
    _post_mixer(x, y, alpha, g_ref, b_ref, rw_ref, rb_ref, h1_ref, eid_ref, gate_ref)


def _mixer_b_kernel(x_ref, hist_ref, w1_ref, b1_ref, cw_ref, cb_ref, lg_ref, lb_ref, w2_ref, b2_ref,
                    g_ref, b_ref, rw_ref, rb_ref,
                    h1_ref, eid_ref, gate_ref, tail_ref, ubuf, cbuf, *, alpha, nseq):
    rows, d = x_ref.shape
    hist_rows = (CF_WIDTH - 1) * nseq
    _load_history(ubuf, hist_ref, rows, hist_rows)
    x = x_ref[...]
    xb = x.astype(BF16)
    a = _dot(xb, w1_ref[:, 0:d]) + b1_ref[:, 0:d]
    gl = _dot(xb, w1_ref[:, d:2 * d]) + b1_ref[:, d:2 * d]
    ubuf[hist_rows:hist_rows + rows, :] = a * _sigmoid(gl)
    tail_ref[...] = ubuf[rows:rows + hist_rows, :]
    halves = CONV_ROWS // SUBLANES

    def chunk(c, carry):
        r0 = pl.multiple_of(c * CONV_ROWS, CONV_ROWS)
        acc = [jnp.broadcast_to(cb_ref[...], (SUBLANES, d)) for _ in range(halves)]
        for k in range(CF_WIDTH):
            w = cw_ref[k * SUBLANES:(k + 1) * SUBLANES, :]
            for h in range(halves):
                start = pl.multiple_of(r0 + h * SUBLANES + k * nseq, SUBLANES)
                acc[h] = acc[h] + w * ubuf[pl.ds(start, SUBLANES), :]
        for h in range(halves):
            cbuf[pl.ds(pl.multiple_of(r0 + h * SUBLANES, SUBLANES), SUBLANES), :] = acc[h]
        return carry

    lax.fori_loop(0, rows // CONV_ROWS, chunk, 0)
    cn = _layer_norm(cbuf[...], lg_ref[...], lb_ref[...])
    cn = cn * _sigmoid(cn)
    y = _dot(cn.astype(BF16), w2_ref[...]) + b2_ref[...]
    _post_mixer(x, y, alpha, g_ref, b_ref, rw_ref, rb_ref, h1_ref, eid_ref, gate_ref)


def _const_spec(shape):
    nd = len(shape)
    return pl.BlockSpec(shape, lambda *_: (0,) * nd, pipeline_mode=pl.Buffered(1))


def _mixer(kind, x2d, hist2d, p, *, pos_tile, seq_tile, nseq, alpha, by_sequence=False):
    d = x2d.shape[-1]
    n = x2d.size // d
    width = SC_WIDTH if kind == "a" else CF_WIDTH
    rows = pos_tile * seq_tile
    hist_rows = (width - 1) * seq_tile
    seq_blocks = nseq // seq_tile
    steps = n // (rows * seq_blocks)
    if kind == "a":
        kern = functools.partial(_mixer_a_kernel, alpha=alpha, nseq=seq_tile)
        weights = (p["w_in"], p["conv_w"], p["w_out"])
        wspecs = [_const_spec((d, 3 * d)), _const_spec((8, d)), _const_spec((d, d))]
        scratch = [pltpu.VMEM((rows + hist_rows, d), F32)]
        if by_sequence:
            scratch.append(pltpu.VMEM((d // LANES, rows, LANES), F32))
    else:
        assert not by_sequence
        kern = functools.partial(_mixer_b_kernel, alpha=alpha, nseq=seq_tile)
        weights = (p["w_pw1"], p["b_pw1"], p["conv_w"], p["conv_b"], p["cln_g"], p["cln_b"], p["w_pw2"], p["b_pw2"])
        wspecs = [_const_spec((d, 2 * d)), _const_spec((1, 2 * d)), _const_spec((CF_WIDTH * SUBLANES, d)),
                  _const_spec((1, d)), _const_spec((1, d)), _const_spec((1, d)), _const_spec((d, d)),
                  _const_spec((1, d))]
        scratch = [pltpu.VMEM((rows + hist_rows, d), F32), pltpu.VMEM((rows, d), F32)]
    common = (p["ln1_g"], p["ln1_b"], p["rw"], p["rb"])
    cspecs = [_const_spec((1, d)), _const_spec((1, d)), _const_spec((ROUTER_ROWS, d)), _const_spec((ROUTER_ROWS, 1))]
    if by_sequence:
        assert seq_blocks == 1
        x_spec = pl.BlockSpec((nseq, pos_tile, d), lambda s, j: (0, j, 0))
    else:
        x_spec = pl.BlockSpec((rows, d), lambda s, j: (s * steps + j, 0))
    return pl.pallas_call(
        kern,
        grid=(seq_blocks, steps),
        in_specs=[x_spec, pl.BlockSpec((hist_rows, d), lambda s, j: (s, 0))] + wspecs + cspecs,
        out_specs=[
            pl.BlockSpec((rows, d), lambda s, j: (s * steps + j, 0)),
            pl.BlockSpec((SUBLANES, rows), lambda s, j: (0, s * steps + j)),
            pl.BlockSpec((SUBLANES, rows), lambda s, j: (0, s * steps + j)),
            pl.BlockSpec((hist_rows, d), lambda s, j: (s, 0)),
        ],
        out_shape=[
            jax.ShapeDtypeStruct((n, d), F32),
            jax.ShapeDtypeStruct((SUBLANES, n), I32),
            jax.ShapeDtypeStruct((SUBLANES, n), F32),
            jax.ShapeDtypeStruct((seq_blocks * hist_rows, d), F32),
        ],
        scratch_shapes=scratch,
        compiler_params=pltpu.CompilerParams(
            dimension_semantics=("arbitrary", "arbitrary"), vmem_limit_bytes=VMEM_LIMIT),
        name="mixer_" + kind,
    )(x2d, hist2d, *weights, *common)


def _rank_kernel(eid_ref, tab_ref, blk_ref, tri, tot, prior, pad_start, pad_chunks, *, total_rows):
    phase, i = pl.program_id(0), pl.program_id(1)
    t = eid_ref.shape[1]
    row = lax.broadcasted_iota(I32, (N_EXPERTS, t), 0)
    oh1 = (row == eid_ref[0:1, :]).astype(F32)
    oh2 = (row == eid_ref[1:2, :]).astype(F32)
    c1 = jnp.sum(oh1, axis=1, keepdims=True)
    c2 = jnp.sum(oh2, axis=1, keepdims=True)
    chunks = jnp.ceil((c1 + c2) * (1.0 / ROW_ALIGN))
    tab_ref[...] = jnp.zeros_like(tab_ref)

    @pl.when(jnp.logical_and(phase == 0, i == 0))
    def _():
        tot[...] = jnp.zeros_like(tot)
        a = lax.broadcasted_iota(I32, tri.shape, 0)
        b = lax.broadcasted_iota(I32, tri.shape, 1)
        tri[...] = (a <= b).astype(BF16)

    @pl.when(phase == 0)
    def _():
        tot[...] += chunks * ROW_ALIGN

    @pl.when(jnp.logical_and(phase == 1, i == 0))
    def _():
        cnt = tot[...].astype(I32)
        seg = ((cnt + (MOE_BLOCK - 1)) // MOE_BLOCK) * MOE_BLOCK
        r128 = lax.broadcasted_iota(I32, cnt.shape, 0)
        base = jnp.zeros_like(cnt)
        for e in range(N_EXPERTS - 1):
            base = base + jnp.where(r128 > e, seg[e:e + 1, :], 0)
        prior[...] = base.astype(F32)
        pad_start[...] = (base + cnt).astype(F32)
        pad_end = jnp.where(r128 == N_EXPERTS - 1, total_rows, base + seg)
        pad_chunks[...] = ((pad_end - base - cnt) // ROW_ALIGN).astype(F32)
        end_blk = (base + seg) // MOE_BLOCK
        lane = lax.broadcasted_iota(I32, (N_EXPERTS, BLOCK_LANES), 1)
        expert_of = jnp.sum((end_blk[:, 0:1] <= lane).astype(F32), axis=0, keepdims=True).astype(I32)
        active = (lane[0:1, :] < end_blk[N_EXPERTS - 1:N_EXPERTS, 0:1]).astype(I32)
        r8 = lax.broadcasted_iota(I32, (SUBLANES, BLOCK_LANES), 0)
        blk_ref[...] = jnp.where(r8 == 0, jnp.minimum(expert_of, N_EXPERTS - 1), jnp.where(r8 == 1, active, 0))

    @pl.when(phase == 1)
    def _():
        ohs = jnp.concatenate([oh1, oh2], axis=0).astype(BF16)
        groups, run = [], jnp.zeros((2 * N_EXPERTS, 1), F32)
        for g in range(t // LANES):
            part = _dot(ohs[:, g * LANES:(g + 1) * LANES], tri[...])
            groups.append(part + run)
            run = run + part[:, LANES - 1:LANES]
        cs = jnp.concatenate(groups, axis=1)
        ea = lax.broadcasted_iota(I32, (N_EXPERTS, N_EXPERTS), 0)
        eb = lax.broadcasted_iota(I32, (N_EXPERTS, N_EXPERTS), 1)
        chunks_b = jnp.broadcast_to(chunks, (N_EXPERTS, LANES))
        start = _dot((eb < ea).astype(BF16), chunks_b.astype(BF16)) * ROW_ALIGN
        s0 = start[:, 0:1]
        pos1 = jnp.sum(oh1 * (s0 + cs[0:N_EXPERTS, :] - 1.0), axis=0, keepdims=True)
        pos2 = jnp.sum(oh2 * (s0 + c1 + cs[N_EXPERTS:2 * N_EXPERTS, :] - 1.0), axis=0, keepdims=True)
        pos1 = (pos1 * SUBLANES).astype(I32)
        pos2 = (pos2 * SUBLANES).astype(I32)
        for q in range(POS_ROWS):
            tab_ref[q:q + 1, :] = pos1[:, q * LANES:(q + 1) * LANES]
            tab_ref[POS_ROWS + q:POS_ROWS + q + 1, :] = pos2[:, q * LANES:(q + 1) * LANES]
        diag = lax.broadcasted_iota(I32, (N_EXPERTS, LANES), 0) == lax.broadcasted_iota(I32, (N_EXPERTS, LANES), 1)
        to_lanes = lambda col: jnp.sum(jnp.where(diag, col, 0.0), axis=0, keepdims=True).astype(I32)
        tab_ref[RUN_START_ROW:RUN_START_ROW + 1, :] = to_lanes(start)
        tab_ref[RUN_CHUNKS_ROW:RUN_CHUNKS_ROW + 1, :] = to_lanes(chunks_b)
        tab_ref[RUN_DEST_ROW:RUN_DEST_ROW + 1, :] = to_lanes(prior[...])
        tab_ref[PAD_START_ROW:PAD_START_ROW + 1, :] = to_lanes(pad_start[...])
        tab_ref[PAD_CHUNKS_ROW:PAD_CHUNKS_ROW + 1, :] = to_lanes(pad_chunks[...])
        prior[...] += chunks * ROW_ALIGN


def _rank(eid, total_rows):
    n = eid.shape[1]
    tiles = n // TOK_TILE
    return pl.pallas_call(
        functools.partial(_rank_kernel, total_rows=total_rows),
        grid=(2, tiles),
        in_specs=[pl.BlockSpec((SUBLANES, TOK_TILE), lambda ph, i: (0, i))],
        out_specs=[
            pl.BlockSpec((None, TABLE_ROWS, LANES), lambda ph, i: (i * ph, 0, 0)),
            pl.BlockSpec((SUBLANES, BLOCK_LANES), lambda ph, i: (0, 0)),
        ],
        out_shape=[
            jax.ShapeDtypeStruct((tiles, TABLE_ROWS, LANES), I32),
            jax.ShapeDtypeStruct((SUBLANES, BLOCK_LANES), I32),
        ],
        scratch_shapes=[pltpu.VMEM((LANES, LANES), BF16)] + [pltpu.VMEM((N_EXPERTS, LANES), F32)] * 4,
        compiler_params=pltpu.CompilerParams(
            dimension_semantics=("arbitrary", "arbitrary"), vmem_limit_bytes=VMEM_LIMIT),
        name="rank",
    )(eid)


def _tile_at(ref, row):
    return ref.at[pl.ds(pl.multiple_of(row, SUBLANES), SUBLANES), :]


def _token_rows(ref, tok):
    return _tile_at(ref, tok * SUBLANES)


def _for_each_chunk(tab, fn, rows=(RUN_START_ROW, RUN_CHUNKS_ROW, RUN_DEST_ROW)):
    start_row, chunks_row, dest_row = rows
    per_big = BIG_CHUNK_ROWS // ROW_ALIGN

    def per_expert(e, carry):
        start, dest, chunks = tab[start_row, e], tab[dest_row, e], tab[chunks_row, e]
        nbig = chunks // per_big

        def piece(nrows):
            def body(q, c2):
                fn(pl.multiple_of(start + q * nrows, ROW_ALIGN), pl.multiple_of(dest + q * nrows, ROW_ALIGN), nrows)
                return c2
            return body

        lax.fori_loop(0, nbig, piece(BIG_CHUNK_ROWS), 0)
        lax.fori_loop(nbig * per_big, chunks, piece(ROW_ALIGN), 0)
        return carry

    lax.fori_loop(0, N_EXPERTS, per_expert, 0)


def _dispatch_kernel(*refs, tile_base, aliased, fill_pads):
    if aliased:
        tab_hbm, h_ref, _, xs_hbm, tab, prev, tokbuf, loc, xl, zrows, sem_tab, sem_row = refs
    else:
        tab_hbm, h_ref, xs_hbm, tab, prev, tokbuf, loc, xl, zrows, sem_tab, sem_row = refs
    t = h_ref.shape[0]
    i = pl.program_id(0)
    cp = pltpu.make_async_copy(tab_hbm.at[tile_base + i], tab, sem_tab)
    cp.start()
    for c in range(SUBLANES):
        tokbuf[pl.ds(c, t, stride=SUBLANES), :] = h_ref[:, c * LANES:(c + 1) * LANES]

    @pl.when(i == 0)
    def _():
        loc[...] = jnp.zeros_like(loc)
        for e in range(N_EXPERTS):
            prev[0, e] = 0

    cp.wait()

    def place(q, carry):
        for lane in range(LANES):
            v = _token_rows(tokbuf, q * LANES + lane)[...]
            _tile_at(loc, tab[q, lane])[...] = v
            _tile_at(loc, tab[POS_ROWS + q, lane])[...] = v
        return carry

    lax.fori_loop(0, t // LANES, place, 0)

    def chunk_copy(local_row, global_row, nrows):
        return pltpu.make_async_copy(xl.at[pl.ds(local_row, nrows), :],
                                     xs_hbm.at[pl.ds(global_row, nrows), :], sem_row)

    prev_rows = (0, 1, 0)

    @pl.when(i > 0)
    def _():
        _for_each_chunk(prev, lambda *a: chunk_copy(*a).wait(), prev_rows)

    for c in range(SUBLANES):
        xl[:, c * LANES:(c + 1) * LANES] = loc[pl.ds(c, LOCAL_ROWS, stride=SUBLANES), :].astype(BF16)
    _for_each_chunk(tab, lambda *a: chunk_copy(*a).start())
    for e in range(N_EXPERTS):
        prev[1, e] = tab[RUN_CHUNKS_ROW, e]

    @pl.when(i == pl.num_programs(0) - 1)
    def _():
        _for_each_chunk(prev, lambda *a: chunk_copy(*a).wait(), prev_rows)

    if fill_pads:
        @pl.when(i == pl.num_programs(0) - 1)
        def _():
            zrows[...] = jnp.zeros_like(zrows)
            zero_copy = lambda _, g, nrows: pltpu.make_async_copy(
                zrows.at[pl.ds(0, nrows), :], xs_hbm.at[pl.ds(g, nrows), :], sem_row)
            pad_rows = (PAD_START_ROW, PAD_CHUNKS_ROW, PAD_START_ROW)
            _for_each_chunk(tab, lambda *a: zero_copy(*a).start(), pad_rows)
            _for_each_chunk(tab, lambda *a: zero_copy(*a).wait(), pad_rows)


def _dispatch(tabs, h2d, xs, *, tile, tile_base, sorted_rows, fill_pads):
    n, d = h2d.shape
    aliased = xs is not None
    kern = functools.partial(_dispatch_kernel, tile_base=tile_base, aliased=aliased, fill_pads=fill_pads)
    in_specs = [pl.BlockSpec(memory_space=pl.ANY), pl.BlockSpec((tile, d), lambda i: (i, 0))]
    args = [tabs, h2d]
    if aliased:
        in_specs.append(pl.BlockSpec(memory_space=pl.ANY))
        args.append(xs)
    return pl.pallas_call(
        kern,
        grid=(n // tile,),
        in_specs=in_specs,
        out_specs=pl.BlockSpec(memory_space=pl.ANY),
        out_shape=jax.ShapeDtypeStruct((sorted_rows, d), BF16),
        scratch_shapes=[pltpu.SMEM((TABLE_ROWS, LANES), I32), pltpu.SMEM((SUBLANES, LANES), I32),
                        pltpu.VMEM((tile * SUBLANES, LANES), F32),
                        pltpu.VMEM((LOCAL_ROWS * SUBLANES, LANES), F32), pltpu.VMEM((LOCAL_ROWS, d), BF16),
                        pltpu.VMEM((BIG_CHUNK_ROWS, d), BF16), pltpu.SemaphoreType.DMA, pltpu.SemaphoreType.DMA],
        input_output_aliases={2: 0} if aliased else {},
        compiler_params=pltpu.CompilerParams(
            dimension_semantics=("arbitrary",), vmem_limit_bytes=VMEM_LIMIT),
        name="dispatch",
    )(*args)


def _combine_kernel(tab_hbm, gate_hbm, ys_hbm, h_ref, g_ref, b_ref, o_ref,
                    tabs, gates, yl, ytok, ztok, sem_tab, sem_gate, sem_row, *maybe_zout, tile_base, alpha):
    t = h_ref.shape[0]
    i = pl.program_id(0)
    last = pl.num_programs(0) - 1
    slot = i % 2
    tab = tabs.at[slot]
    table_copy = lambda tile, s: pltpu.make_async_copy(tab_hbm.at[tile_base + tile], tabs.at[s], sem_tab.at[s])
    cp_gate = pltpu.make_async_copy(gate_hbm.at[:, pl.ds(pl.multiple_of(i * t, LANES), t)], gates, sem_gate)

    def chunk_copy(local_row, global_row, nrows):
        return pltpu.make_async_copy(ys_hbm.at[pl.ds(global_row, nrows), :],
                                     yl.at[pl.ds(local_row, nrows), :], sem_row)

    @pl.when(i == 0)
    def _():
        yl[...] = jnp.zeros_like(yl)
        table_copy(0, 0).start()
        table_copy(0, 0).wait()
        _for_each_chunk(tabs.at[0], lambda *a: chunk_copy(*a).start())

    @pl.when(i < last)
    def _():
        table_copy(i + 1, 1 - slot).start()

    cp_gate.start()
    _for_each_chunk(tab, lambda *a: chunk_copy(*a).wait())
    for c in range(SUBLANES):
        ytok[pl.ds(c, LOCAL_ROWS, stride=SUBLANES), :] = yl[:, c * LANES:(c + 1) * LANES].astype(F32)

    @pl.when(i < last)
    def _():
        table_copy(i + 1, 1 - slot).wait()
        _for_each_chunk(tabs.at[1 - slot], lambda *a: chunk_copy(*a).start())

    cp_gate.wait()

    def gather(q, carry):
        for lane in range(LANES):
            n = q * LANES + lane
            _token_rows(ztok, n)[...] = (gates[0, n] * _tile_at(ytok, tab[q, lane])[...]
                                         + gates[1, n] * _tile_at(ytok, tab[POS_ROWS + q, lane])[...])
        return carry

    lax.fori_loop(0, t // LANES, gather, 0)
    f = jnp.concatenate([ztok[pl.ds(c, t, stride=SUBLANES), :] for c in range(SUBLANES)], axis=1)
    out = _layer_norm(alpha * h_ref[...] + f, g_ref[...], b_ref[...])
    if maybe_zout:
        _by_sequence(out, o_ref, *maybe_zout)
    else:
        o_ref[...] = out


def _combine(tabs, gate, ys, h2d, p, *, tile, tile_base, alpha, out_sequences=None):
    n, d = h2d.shape
    kern = functools.partial(_combine_kernel, tile_base=tile_base, alpha=alpha)
    scratch = [pltpu.SMEM((2, TABLE_ROWS, LANES), I32), pltpu.SMEM((SUBLANES, tile), F32),
               pltpu.VMEM((LOCAL_ROWS, d), BF16), pltpu.VMEM((LOCAL_ROWS * SUBLANES, LANES), F32),
               pltpu.VMEM((tile * SUBLANES, LANES), F32),
               pltpu.SemaphoreType.DMA((2,)), pltpu.SemaphoreType.DMA, pltpu.SemaphoreType.DMA]
    if out_sequences:
        npos = tile // out_sequences
        out_spec = pl.BlockSpec((out_sequences, npos, d), lambda i: (0, i, 0))
        out_shape = jax.ShapeDtypeStruct((out_sequences, n // out_sequences, d), F32)
        scratch.append(pltpu.VMEM((d // LANES, tile, LANES), F32))
    else:
        out_spec = pl.BlockSpec((tile, d), lambda i: (i, 0))
        out_shape = jax.ShapeDtypeStruct((n, d), F32)
    return pl.pallas_call(
        kern,
        grid=(n // tile,),
        in_specs=[
            pl.BlockSpec(memory_space=pl.ANY), pl.BlockSpec(memory_space=pl.ANY), pl.BlockSpec(memory_space=pl.ANY),
            pl.BlockSpec((tile, d), lambda i: (i, 0)),
            _const_spec((1, d)), _const_spec((1, d)),
        ],
        out_specs=out_spec,
        out_shape=out_shape,
        scratch_shapes=scratch,
        compiler_params=pltpu.CompilerParams(
            dimension_semantics=("arbitrary",), vmem_limit_bytes=VMEM_LIMIT),
        name="combine",
    )(tabs, gate, ys, h2d, p["ln2_g"], p["ln2_b"])


def _moe_kernel(blk_ref, x_ref, wg_ref, wu_ref, wd_ref, y_ref, wgb, wub, wdb):
    b = pl.program_id(0)

    @pl.when(jnp.logical_or(b == 0, blk_ref[0, b] != blk_ref[0, jnp.maximum(b - 1, 0)]))
    def _():
        wgb[...] = wg_ref[...].astype(BF16)
        wub[...] = wu_ref[...].astype(BF16)
        wdb[...] = wd_ref[...].astype(BF16)

    @pl.when(blk_ref[1, b] == 1)
    def _():
        x = x_ref[...]
        hg = _dot(x, wgb[...])
        hu = _dot(x, wub[...])
        hid = (hg * _sigmoid(hg) * hu).astype(BF16)
        y_ref[...] = _dot(hid, wdb[...]).astype(BF16)

    @pl.when(blk_ref[1, b] == 0)
    def _():
        y_ref[...] = jnp.zeros_like(y_ref)


def _moe(blk, xs, wg, wu, wd, layer):
    rows, d = xs.shape
    nblocks = rows // MOE_BLOCK
    f = wg.shape[-1]
    block = lambda b, blk: (jnp.where(blk[1, b] == 1, b, nblocks - 1), 0)
    return pl.pallas_call(
        _moe_kernel,
        grid_spec=pltpu.PrefetchScalarGridSpec(
            num_scalar_prefetch=1,
            grid=(nblocks,),
            in_specs=[
                pl.BlockSpec((MOE_BLOCK, d), block),
                pl.BlockSpec((None, None, d, f), lambda b, blk: (layer, blk[0, b], 0, 0)),
                pl.BlockSpec((None, None, d, f), lambda b, blk: (layer, blk[0, b], 0, 0)),
                pl.BlockSpec((None, None, f, d), lambda b, blk: (layer, blk[0, b], 0, 0)),
            ],
            out_specs=pl.BlockSpec((MOE_BLOCK, d), lambda b, blk: (b, 0)),
            scratch_shapes=[pltpu.VMEM((d, f), BF16), pltpu.VMEM((d, f), BF16), pltpu.VMEM((f, d), BF16)],
        ),
        out_shape=jax.ShapeDtypeStruct((rows, d), BF16),
        compiler_params=pltpu.CompilerParams(
            dimension_semantics=("arbitrary",), vmem_limit_bytes=VMEM_LIMIT),
        name="moe",
    )(blk, xs, wg, wu, wd)


def _sparse_moe(sets, p, *, layer, alpha, out_sequences=None):
    eids, bases, tiles = [], [], 0
    for h, e, _ in sets:
        n = h.shape[0]
        n_pad = -(-n // TOK_TILE) * TOK_TILE
        eids.append(jnp.concatenate([e, jnp.full((SUBLANES, n_pad - n), -1, I32)], axis=1))
        bases.append(tiles)
        tiles += n_pad // TOK_TILE
    n_tok = sum(h.shape[0] for h, _, _ in sets)
    run_pad = tiles * N_EXPERTS * (ROW_ALIGN - 1)
    nblocks = -(-(2 * n_tok + run_pad + N_EXPERTS * (MOE_BLOCK - 1)) // MOE_BLOCK)
    assert nblocks <= BLOCK_LANES
    sorted_rows = nblocks * MOE_BLOCK
    tabs, blk = _rank(jnp.concatenate(eids, axis=1), sorted_rows)
    xs = None
    for k, ((h, _, _), tb) in enumerate(zip(sets, bases)):
        xs = _dispatch(tabs, h, xs, tile=min(TOK_TILE, h.shape[0]), tile_base=tb, sorted_rows=sorted_rows,
                       fill_pads=k == len(sets) - 1)
    ys = _moe(blk, xs, p["w_gate"], p["w_up"], p["w_down"], layer)
    return [_combine(tabs, g, ys, h, p, tile=min(TOK_TILE, h.shape[0]), tile_base=tb, alpha=alpha,
                     out_sequences=out_sequences if k == 0 else None)
            for k, ((h, _, g), tb) in enumerate(zip(sets, bases))]


def _pad_rows(w, rows):
    return jnp.concatenate([w, jnp.zeros((rows - w.shape[0],) + w.shape[1:], w.dtype)], axis=0)


def _router_params(gw, gb, ew, eb):
    d = gw.shape[0]
    rw = jnp.concatenate([gw.T, jnp.zeros((8 - N_GROUPS, d), gw.dtype), ew.T], axis=0).astype(BF16)
    rb = jnp.concatenate([gb, jnp.zeros((8 - N_GROUPS,), gb.dtype), eb])[:, None].astype(F32)
    return rw, rb


def _to_blocks(x, seq_tile):
    nseq, npos, d = x.shape
    return x.reshape(nseq // seq_tile, seq_tile, npos, d).swapaxes(1, 2).reshape(nseq * npos, d)


def _from_blocks(x2d, nseq, seq_tile):
    n, d = x2d.shape
    npos = n // nseq
    return x2d.reshape(nseq // seq_tile, npos, seq_tile, d).swapaxes(1, 2).reshape(nseq, npos, d)


def kernel(x_prompt, x_sample, state_conv_a, state_conv_b, meta_tokens, sc_w_in, sc_conv_w, sc_w_out, cf_w_pw1, cf_b_pw1, cf_conv_w, cf_conv_b, cf_ln_g, cf_ln_b, cf_w_pw2, cf_b_pw2, ln1_g, ln1_b, ln2_g, ln2_b, rt_group_w, rt_group_b, rt_expert_w, rt_expert_b, moe_w_gate, moe_w_up, moe_w_down):
    bsz, seq, d = x_prompt.shape
    dec_b, dec_t, _ = x_sample.shape
    n_meta = meta_tokens.shape[0]
    depth = ln1_g.shape[0]
    alpha = float((2 * depth) ** 0.25)
    row = lambda v: v[None, :].astype(F32)
    s_seq = 32
    p_pos = 512 // bsz

    layer = []
    for i in range(depth):
        rw, rb = _router_params(rt_group_w[i], rt_group_b[i], rt_expert_w[i], rt_expert_b[i])
        layer.append(dict(
            ln1_g=row(ln1_g[i]), ln1_b=row(ln1_b[i]), ln2_g=row(ln2_g[i]), ln2_b=row(ln2_b[i]), rw=rw, rb=rb,
            w_gate=moe_w_gate, w_up=moe_w_up, w_down=moe_w_down))
    pa = dict(layer[0], w_in=sc_w_in[0].astype(BF16), conv_w=_pad_rows(sc_conv_w[0], 8),
              w_out=sc_w_out[0].astype(BF16))
    pb = dict(layer[1], w_pw1=cf_w_pw1[0].astype(BF16), b_pw1=row(cf_b_pw1[0]),
              conv_w=jnp.repeat(cf_conv_w[0].astype(F32), SUBLANES, axis=0), conv_b=row(cf_conv_b[0]),
              cln_g=row(cf_ln_g[0]), cln_b=row(cf_ln_b[0]), w_pw2=cf_w_pw2[0].astype(BF16), b_pw2=row(cf_b_pw2[0]))

    xs = _to_blocks(x_sample, s_seq)
    xm = jnp.broadcast_to(meta_tokens.astype(F32)[:, None, :], (n_meta, bsz, d)).reshape(n_meta * bsz, d)
    mix_p = dict(pos_tile=p_pos, seq_tile=bsz, nseq=bsz, alpha=alpha)
    mix_s = dict(pos_tile=dec_t, seq_tile=s_seq, nseq=dec_b, alpha=alpha)
    mix_m = dict(pos_tile=n_meta, seq_tile=bsz, nseq=bsz, alpha=alpha)

    zero_a = jnp.zeros(((SC_WIDTH - 1) * bsz, d), F32)
    hm, em, gm, tail_am = _mixer("a", xm, zero_a, pa, **mix_m)
    hp, ep, gp, tail_ap = _mixer("a", x_prompt, tail_am, pa, by_sequence=True, **mix_p)
    hs, es, gs, tail_as = _mixer("a", xs, _to_blocks(state_conv_a[0], s_seq), pa, **mix_s)
    hp, hs, hm = _sparse_moe([(hp, ep, gp), (hs, es, gs), (hm, em, gm)], pa, layer=0, alpha=alpha)

    zero_b = jnp.zeros(((CF_WIDTH - 1) * bsz, d), F32)
    _, _, _, tail_bm = _mixer("b", hm, zero_b, pb, **mix_m)
    hp, ep, gp, tail_bp = _mixer("b", hp, tail_bm, pb, **mix_p)
    hs, es, gs, tail_bs = _mixer("b", hs, _to_blocks(state_conv_b[0], s_seq), pb, **mix_s)
    y_prompt, hs = _sparse_moe([(hp, ep, gp), (hs, es, gs)], pb, layer=1, alpha=alpha, out_sequences=bsz)

    return (y_prompt, _from_blocks(hs, dec_b, s_seq),
            _from_blocks(tail_ap, bsz, bsz)[None], _from_blocks(tail_bp, bsz, bsz)[None],
            _from_blocks(tail_as, dec_b, s_seq)[None], _from_blocks(tail_bs, dec_b, s_seq)[None])
```

```python
import functools

import jax
import jax.numpy as jnp
from jax import lax
from jax.experimental import pallas as pl
from jax.experimental.pallas import tpu as pltpu

F32 = jnp.float32
BF16 = jnp.bfloat16
I32 = jnp.int32

LN_EPS = 1e-5
N_GROUPS = 4
EXPERTS_PER_GROUP = 8
N_EXPERTS = N_GROUPS * EXPERTS_PER_GROUP
SC_WIDTH = 3
CF_WIDTH = 31
ROUTER_ROWS = 8 + N_EXPERTS
SUBLANES = 8
LANES = 128
ROW_ALIGN = 16
BIG_CHUNK_ROWS = 64
CONV_ROWS = 16
VMEM_LIMIT = 56 * 1024 * 1024
MOE_BLOCK = 512
TOK_TILE = 1024
LOCAL_ROWS = 2 * TOK_TILE + N_EXPERTS * ROW_ALIGN
POS_ROWS = TOK_TILE // LANES
RUN_START_ROW, RUN_CHUNKS_ROW, RUN_DEST_ROW = 2 * POS_ROWS, 2 * POS_ROWS + 1, 2 * POS_ROWS + 2
PAD_START_ROW, PAD_CHUNKS_ROW = 2 * POS_ROWS + 3, 2 * POS_ROWS + 4
TABLE_ROWS = 2 * POS_ROWS + SUBLANES
BLOCK_LANES = 256


def _dot(a, b):
    return jnp.dot(a, b, preferred_element_type=F32)


def _layer_norm(z, g, b):
    mu = jnp.mean(z, axis=-1, keepdims=True)
    zc = z - mu
    var = jnp.mean(zc * zc, axis=-1, keepdims=True)
    return zc * lax.rsqrt(var + LN_EPS) * g + b


def _sigmoid(x):
    return 1.0 / (1.0 + jnp.exp(-x))


def _route(h1b, rw_ref, rb_ref):
    t = h1b.shape[0]
    logits = lax.dot_general(rw_ref[...], h1b, (((1,), (1,)), ((), ())),
                             preferred_element_type=F32) + rb_ref[...]
    g = [logits[i:i + 1, :] for i in range(N_GROUPS)]
    gmax = jnp.maximum(jnp.maximum(g[0], g[1]), jnp.maximum(g[2], g[3]))
    gidx = jnp.where(g[0] == gmax, 0, jnp.where(g[1] == gmax, 1, jnp.where(g[2] == gmax, 2, 3)))
    gsum = (jnp.exp(g[0] - gmax) + jnp.exp(g[1] - gmax)) + (jnp.exp(g[2] - gmax) + jnp.exp(g[3] - gmax))
    gp = 1.0 / gsum
    sel = logits[8 + 8 * (N_GROUPS - 1):8 + 8 * N_GROUPS, :]
    for gi in range(N_GROUPS - 2, -1, -1):
        sel = jnp.where(gidx == gi, logits[8 + 8 * gi:16 + 8 * gi, :], sel)
    row = lax.broadcasted_iota(I32, (EXPERTS_PER_GROUP, t), 0)
    m1 = jnp.max(sel, axis=0, keepdims=True)
    i1 = jnp.min(jnp.where(sel == m1, row, EXPERTS_PER_GROUP), axis=0, keepdims=True)
    sel2 = jnp.where(row == i1, -jnp.inf, sel)
    m2 = jnp.max(sel2, axis=0, keepdims=True)
    i2 = jnp.min(jnp.where(sel2 == m2, row, EXPERTS_PER_GROUP), axis=0, keepdims=True)
    d = jnp.exp(m2 - m1)
    w1 = 1.0 / (1.0 + d)
    w2 = d / (1.0 + d)
    return gidx * EXPERTS_PER_GROUP + i1, gidx * EXPERTS_PER_GROUP + i2, gp * w1, gp * w2


def _post_mixer(x, y, alpha, g_ref, b_ref, rw_ref, rb_ref, h1_ref, tok_ref, eid_ref, gate_ref):
    h1 = _layer_norm(alpha * x + y, g_ref[...], b_ref[...])
    h1_ref[...] = h1
    for c in range(SUBLANES):
        tok_ref[pl.ds(c, h1.shape[0], stride=SUBLANES), :] = h1[:, c * LANES:(c + 1) * LANES]
    e1, e2, g1, g2 = _route(h1.astype(BF16), rw_ref, rb_ref)
    row8 = lax.broadcasted_iota(I32, (SUBLANES, h1.shape[0]), 0)
    eid_ref[...] = jnp.where(row8 == 0, e1, jnp.where(row8 == 1, e2, -1))
    gate_ref[...] = jnp.where(row8 == 0, g1, jnp.where(row8 == 1, g2, 0.0))


def _load_history(ubuf, hist_ref, rows, hist_rows):
    j = pl.program_id(1)

    @pl.when(j == 0)
    def _():
        ubuf[0:hist_rows, :] = hist_ref[...]

    @pl.when(j > 0)
    def _():
        ubuf[0:hist_rows, :] = ubuf[rows:rows + hist_rows, :]


def _position_major(x_ref, xin):
    nseq, npos, d = x_ref.shape
    for c in range(d // LANES):
        for s in range(nseq):
            xin[c, pl.ds(s, npos, stride=nseq), :] = x_ref[s, :, c * LANES:(c + 1) * LANES]
    return jnp.concatenate([xin[c] for c in range(d // LANES)], axis=1)


def _by_sequence(rows, o_ref, zout):
    nseq, npos, d = o_ref.shape
    for c in range(d // LANES):
        zout[c] = rows[:, c * LANES:(c + 1) * LANES]
    for c in range(d // LANES):
        for s in range(nseq):
            o_ref[s, :, c * LANES:(c + 1) * LANES] = zout[c, pl.ds(s, npos, stride=nseq), :]


def _mixer_a_kernel(x_ref, hist_ref, win_ref, cw_ref, wout_ref, g_ref, b_ref, rw_ref, rb_ref,
                    h1_ref, tok_ref, eid_ref, gate_ref, tail_ref, ubuf, *maybe_xin, alpha, nseq):
    rows, d = h1_ref.shape
    hist_rows = (SC_WIDTH - 1) * nseq
    _load_history(ubuf, hist_ref, rows, hist_rows)
    x = _position_major(x_ref, *maybe_xin) if maybe_xin else x_ref[...]
    xb = x.astype(BF16)
    ubuf[hist_rows:hist_rows + rows, :] = _dot(xb, win_ref[:, d:2 * d]) * _dot(xb, win_ref[:, 2 * d:3 * d])
    tail_ref[...] = ubuf[rows:rows + hist_rows, :]
    conv = cw_ref[0:1, :] * ubuf[0:rows, :]
    for k in range(1, SC_WIDTH):
        conv = conv + cw_ref[k:k + 1, :] * ubuf[k * nseq:k * nseq + rows, :]
    bg = _dot(xb, win_ref[:, 0:d])
    y = _dot((bg * conv).astype(BF16), wout_ref[...])
    _post_mixer(x, y, alpha, g_ref, b_ref, rw_ref, rb_ref, h1_ref, tok_ref, eid_ref, gate_ref)


def _mixer_b_kernel(x_ref, hist_ref, w1_ref, b1_ref, cw_ref, cb_ref, lg_ref, lb_ref, w2_ref, b2_ref,
                    g_ref, b_ref, rw_ref, rb_ref,
                    h1_ref, tok_ref, eid_ref, gate_ref, tail_ref, ubuf, cbuf, *, alpha, nseq):
    rows, d = x_ref.shape
    hist_rows = (CF_WIDTH - 1) * nseq
    _load_history(ubuf, hist_ref, rows, hist_rows)
    x = x_ref[...]
    xb = x.astype(BF16)
    a = _dot(xb, w1_ref[:, 0:d]) + b1_ref[:, 0:d]
    gl = _dot(xb, w1_ref[:, d:2 * d]) + b1_ref[:, d:2 * d]
    ubuf[hist_rows:hist_rows + rows, :] = a * _sigmoid(gl)
    tail_ref[...] = ubuf[rows:rows + hist_rows, :]
    halves = CONV_ROWS // SUBLANES

    def chunk(c, carry):
        r0 = pl.multiple_of(c * CONV_ROWS, CONV_ROWS)
        acc = [jnp.broadcast_to(cb_ref[...], (SUBLANES, d)) for _ in range(halves)]
        for k in range(CF_WIDTH):
            w = cw_ref[k * SUBLANES:(k + 1) * SUBLANES, :]
            for h in range(halves):
                start = pl.multiple_of(r0 + h * SUBLANES + k * nseq, SUBLANES)
                acc[h] = acc[h] + w * ubuf[pl.ds(start, SUBLANES), :]
        for h in range(halves):
            cbuf[pl.ds(pl.multiple_of(r0 + h * SUBLANES, SUBLANES), SUBLANES), :] = acc[h]
        return carry

    lax.fori_loop(0, rows // CONV_ROWS, chunk, 0)
    cn = _layer_norm(cbuf[...], lg_ref[...], lb_ref[...])
    cn = cn * _sigmoid(cn)
    y = _dot(cn.astype(BF16), w2_ref[...]) + b2_ref[...]
    _post_mixer(x, y, alpha, g_ref, b_ref, rw_ref, rb_ref, h1_ref, tok_ref, eid_ref, gate_ref)


def _const_spec(shape):
    nd = len(shape)
    return pl.BlockSpec(shape, lambda *_: (0,) * nd, pipeline_mode=pl.Buffered(1))


def _mixer(kind, x2d, hist2d, p, *, pos_tile, seq_tile, nseq, alpha, by_sequence=False):
    d = x2d.shape[-1]
    n = x2d.size // d
    width = SC_WIDTH if kind == "a" else CF_WIDTH
    rows = pos_tile * seq_tile
    hist_rows = (width - 1) * seq_tile
    seq_blocks = nseq // seq_tile
    steps = n // (rows * seq_blocks)
    if kind == "a":
        kern = functools.partial(_mixer_a_kernel, alpha=alpha, nseq=seq_tile)
        weights = (p["w_in"], p["conv_w"], p["w_out"])
        wspecs = [_const_spec((d, 3 * d)), _const_spec((8, d)), _const_spec((d, d))]
        scratch = [pltpu.VMEM((rows + hist_rows, d), F32)]
        if by_sequence:
            scratch.append(pltpu.VMEM((d // LANES, rows, LANES), F32))
    else:
        assert not by_sequence
        kern = functools.partial(_mixer_b_kernel, alpha=alpha, nseq=seq_tile)
        weights = (p["w_pw1"], p["b_pw1"], p["conv_w"], p["conv_b"], p["cln_g"], p["cln_b"], p["w_pw2"], p["b_pw2"])
        wspecs = [_const_spec((d, 2 * d)), _const_spec((1, 2 * d)), _const_spec((CF_WIDTH * SUBLANES, d)),
                  _const_spec((1, d)), _const_spec((1, d)), _const_spec((1, d)), _const_spec((d, d)),
                  _const_spec((1, d))]
        scratch = [pltpu.VMEM((rows + hist_rows, d), F32), pltpu.VMEM((rows, d), F32)]
    common = (p["ln1_g"], p["ln1_b"], p["rw"], p["rb"])
    cspecs = [_const_spec((1, d)), _const_spec((1, d)), _const_spec((ROUTER_ROWS, d)), _const_spec((ROUTER_ROWS, 1))]
    if by_sequence:
        assert seq_blocks == 1
        x_spec = pl.BlockSpec((nseq, pos_tile, d), lambda s, j: (0, j, 0))
    else:
        x_spec = pl.BlockSpec((rows, d), lambda s, j: (s * steps + j, 0))
    return pl.pallas_call(
        kern,
        grid=(seq_blocks, steps),
        in_specs=[x_spec, pl.BlockSpec((hist_rows, d), lambda s, j: (s, 0))] + wspecs + cspecs,
        out_specs=[
            pl.BlockSpec((rows, d), lambda s, j: (s * steps + j, 0)),
            pl.BlockSpec((rows * SUBLANES, LANES), lambda s, j: (s * steps + j, 0)),
            pl.BlockSpec((SUBLANES, rows), lambda s, j: (0, s * steps + j)),
            pl.BlockSpec((SUBLANES, rows), lambda s, j: (0, s * steps + j)),
            pl.BlockSpec((hist_rows, d), lambda s, j: (s, 0)),
        ],
        out_shape=[
            jax.ShapeDtypeStruct((n, d), F32),
            jax.ShapeDtypeStruct((n * SUBLANES, LANES), F32),
            jax.ShapeDtypeStruct((SUBLANES, n), I32),
            jax.ShapeDtypeStruct((SUBLANES, n), F32),
            jax.ShapeDtypeStruct((seq_blocks * hist_rows, d), F32),
        ],
        scratch_shapes=scratch,
        compiler_params=pltpu.CompilerParams(
            dimension_semantics=("arbitrary", "arbitrary"), vmem_limit_bytes=VMEM_LIMIT),
        name="mixer_" + kind,
    )(x2d, hist2d, *weights, *common)


def _rank_kernel(eid_ref, tab_ref, blk_ref, tri, tot, prior, pad_start, pad_chunks, *, total_rows):
    phase, i = pl.program_id(0), pl.program_id(1)
    t = eid_ref.shape[1]
    row = lax.broadcasted_iota(I32, (N_EXPERTS, t), 0)
    oh1 = (row == eid_ref[0:1, :]).astype(F32)
    oh2 = (row == eid_ref[1:2, :]).astype(F32)
    c1 = jnp.sum(oh1, axis=1, keepdims=True)
    c2 = jnp.sum(oh2, axis=1, keepdims=True)
    chunks = jnp.ceil((c1 + c2) * (1.0 / ROW_ALIGN))
    tab_ref[...] = jnp.zeros_like(tab_ref)

    @pl.when(jnp.logical_and(phase == 0, i == 0))
    def _():
        tot[...] = jnp.zeros_like(tot)
        a = lax.broadcasted_iota(I32, tri.shape, 0)
        b = lax.broadcasted_iota(I32, tri.shape, 1)
        tri[...] = (a <= b).astype(BF16)

    @pl.when(phase == 0)
    def _():
        tot[...] += chunks * ROW_ALIGN

    @pl.when(jnp.logical_and(phase == 1, i == 0))
    def _():
        cnt = tot[...].astype(I32)
        seg = ((cnt + (MOE_BLOCK - 1)) // MOE_BLOCK) * MOE_BLOCK
        r128 = lax.broadcasted_iota(I32, cnt.shape, 0)
        base = jnp.zeros_like(cnt)
        for e in range(N_EXPERTS - 1):
            base = base + jnp.where(r128 > e, seg[e:e + 1, :], 0)
        prior[...] = base.astype(F32)
        pad_start[...] = (base + cnt).astype(F32)
        pad_end = jnp.where(r128 == N_EXPERTS - 1, total_rows, base + seg)
        pad_chunks[...] = ((pad_end - base - cnt) // ROW_ALIGN).astype(F32)
        end_blk = (base + seg) // MOE_BLOCK
        lane = lax.broadcasted_iota(I32, (N_EXPERTS, BLOCK_LANES), 1)
        expert_of = jnp.sum((end_blk[:, 0:1] <= lane).astype(F32), axis=0, keepdims=True).astype(I32)
        active = (lane[0:1, :] < end_blk[N_EXPERTS - 1:N_EXPERTS, 0:1]).astype(I32)
        r8 = lax.broadcasted_iota(I32, (SUBLANES, BLOCK_LANES), 0)
        blk_ref[...] = jnp.where(r8 == 0, jnp.minimum(expert_of, N_EXPERTS - 1), jnp.where(r8 == 1, active, 0))

    @pl.when(phase == 1)
    def _():
        ohs = jnp.concatenate([oh1, oh2], axis=0).astype(BF16)
        groups, run = [], jnp.zeros((2 * N_EXPERTS, 1), F32)
        for g in range(t // LANES):
            part = _dot(ohs[:, g * LANES:(g + 1) * LANES], tri[...])
            groups.append(part + run)
            run = run + part[:, LANES - 1:LANES]
        cs = jnp.concatenate(groups, axis=1)
        ea = lax.broadcasted_iota(I32, (N_EXPERTS, N_EXPERTS), 0)
        eb = lax.broadcasted_iota(I32, (N_EXPERTS, N_EXPERTS), 1)
        chunks_b = jnp.broadcast_to(chunks, (N_EXPERTS, LANES))
        start = _dot((eb < ea).astype(BF16), chunks_b.astype(BF16)) * ROW_ALIGN
        s0 = start[:, 0:1]
        pos1 = jnp.sum(oh1 * (s0 + cs[0:N_EXPERTS, :] - 1.0), axis=0, keepdims=True)
        pos2 = jnp.sum(oh2 * (s0 + c1 + cs[N_EXPERTS:2 * N_EXPERTS, :] - 1.0), axis=0, keepdims=True)
        pos1 = (pos1 * SUBLANES).astype(I32)
        pos2 = (pos2 * SUBLANES).astype(I32)
        for q in range(POS_ROWS):
            tab_ref[q:q + 1, :] = pos1[:, q * LANES:(q + 1) * LANES]
            tab_ref[POS_ROWS + q:POS_ROWS + q + 1, :] = pos2[:, q * LANES:(q + 1) * LANES]
        diag = lax.broadcasted_iota(I32, (N_EXPERTS, LANES), 0) == lax.broadcasted_iota(I32, (N_EXPERTS, LANES), 1)
        to_lanes = lambda col: jnp.sum(jnp.where(diag, col, 0.0), axis=0, keepdims=True).astype(I32)
        tab_ref[RUN_START_ROW:RUN_START_ROW + 1, :] = to_lanes(start)
        tab_ref[RUN_CHUNKS_ROW:RUN_CHUNKS_ROW + 1, :] = to_lanes(chunks_b)
        tab_ref[RUN_DEST_ROW:RUN_DEST_ROW + 1, :] = to_lanes(prior[...])
        tab_ref[PAD_START_ROW:PAD_START_ROW + 1, :] = to_lanes(pad_start[...])
        tab_ref[PAD_CHUNKS_ROW:PAD_CHUNKS_ROW + 1, :] = to_lanes(pad_chunks[...])
        prior[...] += chunks * ROW_ALIGN


def _rank(eid, total_rows):
    n = eid.shape[1]
    tiles = n // TOK_TILE
    return pl.pallas_call(
        functools.partial(_rank_kernel, total_rows=total_rows),
        grid=(2, tiles),
        in_specs=[pl.BlockSpec((SUBLANES, TOK_TILE), lambda ph, i: (0, i))],
        out_specs=[
            pl.BlockSpec((None, TABLE_ROWS, LANES), lambda ph, i: (i * ph, 0, 0)),
            pl.BlockSpec((SUBLANES, BLOCK_LANES), lambda ph, i: (0, 0)),
        ],
        out_shape=[
            jax.ShapeDtypeStruct((tiles, TABLE_ROWS, LANES), I32),
            jax.ShapeDtypeStruct((SUBLANES, BLOCK_LANES), I32),
        ],
        scratch_shapes=[pltpu.VMEM((LANES, LANES), BF16)] + [pltpu.VMEM((N_EXPERTS, LANES), F32)] * 4,
        compiler_params=pltpu.CompilerParams(
            dimension_semantics=("arbitrary", "arbitrary"), vmem_limit_bytes=VMEM_LIMIT),
        name="rank",
    )(eid)


def _tile_at(ref, row):
    return ref.at[pl.ds(pl.multiple_of(row, SUBLANES), SUBLANES), :]


def _token_rows(ref, tok):
    return _tile_at(ref, tok * SUBLANES)


def _for_each_chunk(tab, fn, rows=(RUN_START_ROW, RUN_CHUNKS_ROW, RUN_DEST_ROW)):
    start_row, chunks_row, dest_row = rows
    per_big = BIG_CHUNK_ROWS // ROW_ALIGN

    def per_expert(e, carry):
        start, dest, chunks = tab[start_row, e], tab[dest_row, e], tab[chunks_row, e]
        nbig = chunks // per_big

        def piece(nrows):
            def body(q, c2):
                fn(pl.multiple_of(start + q * nrows, ROW_ALIGN), pl.multiple_of(dest + q * nrows, ROW_ALIGN), nrows)
                return c2
            return body

        lax.fori_loop(0, nbig, piece(BIG_CHUNK_ROWS), 0)
        lax.fori_loop(nbig * per_big, chunks, piece(ROW_ALIGN), 0)
        return carry

    lax.fori_loop(0, N_EXPERTS, per_expert, 0)


def _dispatch_kernel(*refs, tile_base, aliased, fill_pads):
    if aliased:
        tab_hbm, tok_ref, _, xs_hbm, tab, prev, loc, xl, zrows, sem_tab, sem_row = refs
    else:
        tab_hbm, tok_ref, xs_hbm, tab, prev, loc, xl, zrows, sem_tab, sem_row = refs
    t = tok_ref.shape[0] // SUBLANES
    i = pl.program_id(0)
    cp = pltpu.make_async_copy(tab_hbm.at[tile_base + i], tab, sem_tab)
    cp.start()

    @pl.when(i == 0)
    def _():
        loc[...] = jnp.zeros_like(loc)
        for e in range(N_EXPERTS):
            prev[0, e] = 0

    cp.wait()

    def place(q, carry):
        for lane in range(LANES):
            v = _token_rows(tok_ref, q * LANES + lane)[...]
            _tile_at(loc, tab[q, lane])[...] = v
            _tile_at(loc, tab[POS_ROWS + q, lane])[...] = v
        return carry

    lax.fori_loop(0, t // LANES, place, 0)

    def chunk_copy(local_row, global_row, nrows):
        return pltpu.make_async_copy(xl.at[pl.ds(local_row, nrows), :],
                                     xs_hbm.at[pl.ds(global_row, nrows), :], sem_row)

    prev_rows = (0, 1, 0)

    @pl.when(i > 0)
    def _():
        _for_each_chunk(prev, lambda *a: chunk_copy(*a).wait(), prev_rows)

    for c in range(SUBLANES):
        xl[:, c * LANES:(c + 1) * LANES] = loc[pl.ds(c, LOCAL_ROWS, stride=SUBLANES), :].astype(BF16)
    _for_each_chunk(tab, lambda *a: chunk_copy(*a).start())
    for e in range(N_EXPERTS):
        prev[1, e] = tab[RUN_CHUNKS_ROW, e]

    @pl.when(i == pl.num_programs(0) - 1)
    def _():
        _for_each_chunk(prev, lambda *a: chunk_copy(*a).wait(), prev_rows)

    if fill_pads:
        @pl.when(i == pl.num_programs(0) - 1)
        def _():
            zrows[...] = jnp.zeros_like(zrows)
            zero_copy = lambda _, g, nrows: pltpu.make_async_copy(
                zrows.at[pl.ds(0, nrows), :], xs_hbm.at[pl.ds(g, nrows), :], sem_row)
            pad_rows = (PAD_START_ROW, PAD_CHUNKS_ROW, PAD_START_ROW)
            _for_each_chunk(tab, lambda *a: zero_copy(*a).start(), pad_rows)
            _for_each_chunk(tab, lambda *a: zero_copy(*a).wait(), pad_rows)


def _dispatch(tabs, htok, xs, *, tile, tile_base, sorted_rows, fill_pads):
    n, d = htok.shape[0] // SUBLANES, SUBLANES * LANES
    aliased = xs is not None
    kern = functools.partial(_dispatch_kernel, tile_base=tile_base, aliased=aliased, fill_pads=fill_pads)
    in_specs = [pl.BlockSpec(memory_space=pl.ANY), pl.BlockSpec((tile * SUBLANES, LANES), lambda i: (i, 0))]
    args = [tabs, htok]
    if aliased:
        in_specs.append(pl.BlockSpec(memory_space=pl.ANY))
        args.append(xs)
    return pl.pallas_call(
        kern,
        grid=(n // tile,),
        in_specs=in_specs,
        out_specs=pl.BlockSpec(memory_space=pl.ANY),
        out_shape=jax.ShapeDtypeStruct((sorted_rows, d), BF16),
        scratch_shapes=[pltpu.SMEM((TABLE_ROWS, LANES), I32), pltpu.SMEM((SUBLANES, LANES), I32),
                        pltpu.VMEM((LOCAL_ROWS * SUBLANES, LANES), F32), pltpu.VMEM((LOCAL_ROWS, d), BF16),
                        pltpu.VMEM((BIG_CHUNK_ROWS, d), BF16), pltpu.SemaphoreType.DMA, pltpu.SemaphoreType.DMA],
        input_output_aliases={2: 0} if aliased else {},
        compiler_params=pltpu.CompilerParams(
            dimension_semantics=("arbitrary",), vmem_limit_bytes=VMEM_LIMIT),
        name="dispatch",
    )(*args)


def _combine_kernel(tab_hbm, gate_hbm, ys_hbm, h_ref, g_ref, b_ref, o_ref,
                    tabs, gates, yl, ytok, ztok, sem_tab, sem_gate, sem_row, *maybe_zout, tile_base, alpha):
    t = h_ref.shape[0]
    i = pl.program_id(0)
    last = pl.num_programs(0) - 1
    slot = i % 2
    tab = tabs.at[slot]
    table_copy = lambda tile, s: pltpu.make_async_copy(tab_hbm.at[tile_base + tile], tabs.at[s], sem_tab.at[s])
    cp_gate = pltpu.make_async_copy(gate_hbm.at[:, pl.ds(pl.multiple_of(i * t, LANES), t)], gates, sem_gate)

    def chunk_copy(local_row, global_row, nrows):
        return pltpu.make_async_copy(ys_hbm.at[pl.ds(global_row, nrows), :],
                                     yl.at[pl.ds(local_row, nrows), :], sem_row)

    @pl.when(i == 0)
    def _():
        yl[...] = jnp.zeros_like(yl)
        table_copy(0, 0).start()
        table_copy(0, 0).wait()
        _for_each_chunk(tabs.at[0], lambda *a: chunk_copy(*a).start())

    @pl.when(i < last)
    def _():
        table_copy(i + 1, 1 - slot).start()

    cp_gate.start()
    _for_each_chunk(tab, lambda *a: chunk_copy(*a).wait())
    for c in range(SUBLANES):
        ytok[pl.ds(c, LOCAL_ROWS, stride=SUBLANES), :] = yl[:, c * LANES:(c + 1) * LANES].astype(F32)

    @pl.when(i < last)
    def _():
        table_copy(i + 1, 1 - slot).wait()
        _for_each_chunk(tabs.at[1 - slot], lambda *a: chunk_copy(*a).start())

    cp_gate.wait()

    def gather(q, carry):
        for lane in range(LANES):
            n = q * LANES + lane
            _token_rows(ztok, n)[...] = (gates[0, n] * _tile_at(ytok, tab[q, lane])[...]
                                         + gates[1, n] * _tile_at(ytok, tab[POS_ROWS + q, lane])[...])
        return carry

    lax.fori_loop(0, t // LANES, gather, 0)
    f = jnp.concatenate([ztok[pl.ds(c, t, stride=SUBLANES), :] for c in range(SUBLANES)], axis=1)
    out = _layer_norm(alpha * h_ref[...] + f, g_ref[...], b_ref[...])
    if maybe_zout:
        _by_sequence(out, o_ref, *maybe_zout)
    else:
        o_ref[...] = out


def _combine(tabs, gate, ys, h2d, p, *, tile, tile_base, alpha, out_sequences=None):
    n, d = h2d.shape
    kern = functools.partial(_combine_kernel, tile_base=tile_base, alpha=alpha)
    scratch = [pltpu.SMEM((2, TABLE_ROWS, LANES), I32), pltpu.SMEM((SUBLANES, tile), F32),
               pltpu.VMEM((LOCAL_ROWS, d), BF16), pltpu.VMEM((LOCAL_ROWS * SUBLANES, LANES), F32),
               pltpu.VMEM((tile * SUBLANES, LANES), F32),
               pltpu.SemaphoreType.DMA((2,)), pltpu.SemaphoreType.DMA, pltpu.SemaphoreType.DMA]
    if out_sequences:
        npos = tile // out_sequences
        out_spec = pl.BlockSpec((out_sequences, npos, d), lambda i: (0, i, 0))
        out_shape = jax.ShapeDtypeStruct((out_sequences, n // out_sequences, d), F32)
        scratch.append(pltpu.VMEM((d // LANES, tile, LANES), F32))
    else:
        out_spec = pl.BlockSpec((tile, d), lambda i: (i, 0))
        out_shape = jax.ShapeDtypeStruct((n, d), F32)
    return pl.pallas_call(
        kern,
        grid=(n // tile,),
        in_specs=[
            pl.BlockSpec(memory_space=pl.ANY), pl.BlockSpec(memory_space=pl.ANY), pl.BlockSpec(memory_space=pl.ANY),
            pl.BlockSpec((tile, d), lambda i: (i, 0)),
            _const_spec((1, d)), _const_spec((1, d)),
        ],
        out_specs=out_spec,
        out_shape=out_shape,
        scratch_shapes=scratch,
        compiler_params=pltpu.CompilerParams(
            dimension_semantics=("arbitrary",), vmem_limit_bytes=VMEM_LIMIT),
        name="combine",
    )(tabs, gate, ys, h2d, p["ln2_g"], p["ln2_b"])


def _moe_kernel(blk_ref, x_ref, wg_ref, wu_ref, wd_ref, y_ref, wgb, wub, wdb):
    b = pl.program_id(0)

    @pl.when(jnp.logical_or(b == 0, blk_ref[0, b] != blk_ref[0, jnp.maximum(b - 1, 0)]))
    def _():
        wgb[...] = wg_ref[...].astype(BF16)
        wub[...] = wu_ref[...].astype(BF16)
        wdb[...] = wd_ref[...].astype(BF16)

    @pl.when(blk_ref[1, b] == 1)
    def _():
        x = x_ref[...]
        hg = _dot(x, wgb[...])
        hu = _dot(x, wub[...])
        hid = (hg * _sigmoid(hg) * hu).astype(BF16)
        y_ref[...] = _dot(hid, wdb[...]).astype(BF16)

    @pl.when(blk_ref[1, b] == 0)
    def _():
        y_ref[...] = jnp.zeros_like(y_ref)


def _moe(blk, xs, wg, wu, wd, layer):
    rows, d = xs.shape
    nblocks = rows // MOE_BLOCK
    f = wg.shape[-1]
    block = lambda b, blk: (jnp.where(blk[1, b] == 1, b, nblocks - 1), 0)
    return pl.pallas_call(
        _moe_kernel,
        grid_spec=pltpu.PrefetchScalarGridSpec(
            num_scalar_prefetch=1,
            grid=(nblocks,),
            in_specs=[
                pl.BlockSpec((MOE_BLOCK, d), block),
                pl.BlockSpec((None, None, d, f), lambda b, blk: (layer, blk[0, b], 0, 0)),
                pl.BlockSpec((None, None, d, f), lambda b, blk: (layer, blk[0, b], 0, 0)),
                pl.BlockSpec((None, None, f, d), lambda b, blk: (layer, blk[0, b], 0, 0)),
            ],
            out_specs=pl.BlockSpec((MOE_BLOCK, d), lambda b, blk: (b, 0)),
            scratch_shapes=[pltpu.VMEM((d, f), BF16), pltpu.VMEM((d, f), BF16), pltpu.VMEM((f, d), BF16)],
        ),
        out_shape=jax.ShapeDtypeStruct((rows, d), BF16),
        compiler_params=pltpu.CompilerParams(
            dimension_semantics=("arbitrary",), vmem_limit_bytes=VMEM_LIMIT),
        name="moe",
    )(blk, xs, wg, wu, wd)


def _sparse_moe(sets, p, *, layer, alpha, out_sequences=None):
    eids, bases, tiles = [], [], 0
    for h, _, e, _ in sets:
        n = h.shape[0]
        n_pad = -(-n // TOK_TILE) * TOK_TILE
        eids.append(jnp.concatenate([e, jnp.full((SUBLANES, n_pad - n), -1, I32)], axis=1))
        bases.append(tiles)
        tiles += n_pad // TOK_TILE
    n_tok = sum(s[0].shape[0] for s in sets)
    run_pad = tiles * N_EXPERTS * (ROW_ALIGN - 1)
    nblocks = -(-(2 * n_tok + run_pad + N_EXPERTS * (MOE_BLOCK - 1)) // MOE_BLOCK)
    assert nblocks <= BLOCK_LANES
    sorted_rows = nblocks * MOE_BLOCK
    tabs, blk = _rank(jnp.concatenate(eids, axis=1), sorted_rows)
    xs = None
    for k, ((h, htok, _, _), tb) in enumerate(zip(sets, bases)):
        xs = _dispatch(tabs, htok, xs, tile=min(TOK_TILE, h.shape[0]), tile_base=tb, sorted_rows=sorted_rows,
                       fill_pads=k == len(sets) - 1)
    ys = _moe(blk, xs, p["w_gate"], p["w_up"], p["w_down"], layer)
    return [_combine(tabs, g, ys, h, p, tile=min(TOK_TILE, h.shape[0]), tile_base=tb, alpha=alpha,
                     out_sequences=out_sequences if k == 0 else None)
            for k, ((h, _, _, g), tb) in enumerate(zip(sets, bases))]


def _pad_rows(w, rows):
    return jnp.concatenate([w, jnp.zeros((rows - w.shape[0],) + w.shape[1:], w.dtype)], axis=0)


def _router_params(gw, gb, ew, eb):
    d = gw.shape[0]
    rw = jnp.concatenate([gw.T, jnp.zeros((8 - N_GROUPS, d), gw.dtype), ew.T], axis=0).astype(BF16)
    rb = jnp.concatenate([gb, jnp.zeros((8 - N_GROUPS,), gb.dtype), eb])[:, None].astype(F32)
    return rw, rb


def _to_blocks(x, seq_tile):
    nseq, npos, d = x.shape
    return x.reshape(nseq // seq_tile, seq_tile, npos, d).swapaxes(1, 2).reshape(nseq * npos, d)


def _from_blocks(x2d, nseq, seq_tile):
    n, d = x2d.shape
    npos = n // nseq
    return x2d.reshape(nseq // seq_tile, npos, seq_tile, d).swapaxes(1, 2).reshape(nseq, npos, d)


def kernel(x_prompt, x_sample, state_conv_a, state_conv_b, meta_tokens, sc_w_in, sc_conv_w, sc_w_out, cf_w_pw1, cf_b_pw1, cf_conv_w, cf_conv_b, cf_ln_g, cf_ln_b, cf_w_pw2, cf_b_pw2, ln1_g, ln1_b, ln2_g, ln2_b, rt_group_w, rt_group_b, rt_expert_w, rt_expert_b, moe_w_gate, moe_w_up, moe_w_down):
    bsz, seq, d = x_prompt.shape
    dec_b, dec_t, _ = x_sample.shape
    n_meta = meta_tokens.shape[0]
    depth = ln1_g.shape[0]
    alpha = float((2 * depth) ** 0.25)
    row = lambda v: v[None, :].astype(F32)
    s_seq = 32
    p_pos = 512 // bsz

    layer = []
    for i in range(depth):
        rw, rb = _router_params(rt_group_w[i], rt_group_b[i], rt_expert_w[i], rt_expert_b[i])
        layer.append(dict(
            ln1_g=row(ln1_g[i]), ln1_b=row(ln1_b[i]), ln2_g=row(ln2_g[i]), ln2_b=row(ln2_b[i]), rw=rw, rb=rb,
            w_gate=moe_w_gate, w_up=moe_w_up, w_down=moe_w_down))
    pa = dict(layer[0], w_in=sc_w_in[0].astype(BF16), conv_w=_pad_rows(sc_conv_w[0], 8),
              w_out=sc_w_out[0].astype(BF16))
    pb = dict(layer[1], w_pw1=cf_w_pw1[0].astype(BF16), b_pw1=row(cf_b_pw1[0]),
              conv_w=jnp.repeat(cf_conv_w[0].astype(F32), SUBLANES, axis=0), conv_b=row(cf_conv_b[0]),
              cln_g=row(cf_ln_g[0]), cln_b=row(cf_ln_b[0]), w_pw2=cf_w_pw2[0].astype(BF16), b_pw2=row(cf_b_pw2[0]))

    xs = _to_blocks(x_sample, s_seq)
    xm = jnp.broadcast_to(meta_tokens.astype(F32)[:, None, :], (n_meta, bsz, d)).reshape(n_meta * bsz, d)
    mix_p = dict(pos_tile=p_pos, seq_tile=bsz, nseq=bsz, alpha=alpha)
    mix_s = dict(pos_tile=dec_t, seq_tile=s_seq, nseq=dec_b, alpha=alpha)
    mix_m = dict(pos_tile=n_meta, seq_tile=bsz, nseq=bsz, alpha=alpha)

    zero_a = jnp.zeros(((SC_WIDTH - 1) * bsz, d), F32)
    *set_m, tail_am = _mixer("a", xm, zero_a, pa, **mix_m)
    *set_p, tail_ap = _mixer("a", x_prompt, tail_am, pa, by_sequence=True, **mix_p)
    *set_s, tail_as = _mixer("a", xs, _to_blocks(state_conv_a[0], s_seq), pa, **mix_s)
    hp, hs, hm = _sparse_moe([set_p, set_s, set_m], pa, layer=0, alpha=alpha)

    zero_b = jnp.zeros(((CF_WIDTH - 1) * bsz, d), F32)
    *_, tail_bm = _mixer("b", hm, zero_b, pb, **mix_m)
    *set_p, tail_bp = _mixer("b", hp, tail_bm, pb, **mix_p)
    *set_s, tail_bs = _mixer("b", hs, _to_blocks(state_conv_b[0], s_seq), pb, **mix_s)
    y_prompt, hs = _sparse_moe([set_p, set_s], pb, layer=1, alpha=alpha, out_sequences=bsz)

    return (y_prompt, _from_blocks(hs, dec_b, s_seq),
            _from_blocks(tail_ap, bsz, bsz)[None], _from_blocks(tail_bp, bsz, bsz)[None],
            _from_blocks(tail_as, dec_b, s_seq)[None], _from_blocks(tail_bs, dec_b, s_seq)[None])
```

```python
import functools

import jax
import jax.numpy as jnp
from jax import lax
from jax.experimental import pallas as pl
from jax.experimental.pallas import tpu as pltpu

F32 = jnp.float32
BF16 = jnp.bfloat16
I32 = jnp.int32

LN_EPS = 1e-5
N_GROUPS = 4
EXPERTS_PER_GROUP = 8
N_EXPERTS = N_GROUPS * EXPERTS_PER_GROUP
SC_WIDTH = 3
CF_WIDTH = 31
ROUTER_ROWS = 8 + N_EXPERTS
SUBLANES = 8
LANES = 128
ROW_ALIGN = 16
BIG_CHUNK_ROWS = 64
CONV_ROWS = 16
VMEM_LIMIT = 56 * 1024 * 1024
MOE_BLOCK = 512
TOK_TILE = 1024
LOCAL_ROWS = 2 * TOK_TILE + N_EXPERTS * ROW_ALIGN
POS_ROWS = TOK_TILE // LANES
RUN_START_ROW, RUN_CHUNKS_ROW, RUN_DEST_ROW = 2 * POS_ROWS, 2 * POS_ROWS + 1, 2 * POS_ROWS + 2
PAD_START_ROW, PAD_CHUNKS_ROW = 2 * POS_ROWS + 3, 2 * POS_ROWS + 4
TABLE_ROWS = 2 * POS_ROWS + SUBLANES
BLOCK_LANES = 256


def _dot(a, b):
    return jnp.dot(a, b, preferred_element_type=F32)


def _layer_norm(z, g, b):
    mu = jnp.mean(z, axis=-1, keepdims=True)
    zc = z - mu
    var = jnp.mean(zc * zc, axis=-1, keepdims=True)
    return zc * lax.rsqrt(var + LN_EPS) * g + b


def _sigmoid(x):
    return 1.0 / (1.0 + jnp.exp(-x))


def _route(h1b, rw_ref, rb_ref):
    t = h1b.shape[0]
    logits = lax.dot_general(rw_ref[...], h1b, (((1,), (1,)), ((), ())),
                             preferred_element_type=F32) + rb_ref[...]
    g = [logits[i:i + 1, :] for i in range(N_GROUPS)]
    gmax = jnp.maximum(jnp.maximum(g[0], g[1]), jnp.maximum(g[2], g[3]))
    gidx = jnp.where(g[0] == gmax, 0, jnp.where(g[1] == gmax, 1, jnp.where(g[2] == gmax, 2, 3)))
    gsum = (jnp.exp(g[0] - gmax) + jnp.exp(g[1] - gmax)) + (jnp.exp(g[2] - gmax) + jnp.exp(g[3] - gmax))
    gp = 1.0 / gsum
    sel = logits[8 + 8 * (N_GROUPS - 1):8 + 8 * N_GROUPS, :]
    for gi in range(N_GROUPS - 2, -1, -1):
        sel = jnp.where(gidx == gi, logits[8 + 8 * gi:16 + 8 * gi, :], sel)
    row = lax.broadcasted_iota(I32, (EXPERTS_PER_GROUP, t), 0)
    m1 = jnp.max(sel, axis=0, keepdims=True)
    i1 = jnp.min(jnp.where(sel == m1, row, EXPERTS_PER_GROUP), axis=0, keepdims=True)
    sel2 = jnp.where(row == i1, -jnp.inf, sel)
    m2 = jnp.max(sel2, axis=0, keepdims=True)
    i2 = jnp.min(jnp.where(sel2 == m2, row, EXPERTS_PER_GROUP), axis=0, keepdims=True)
    d = jnp.exp(m2 - m1)
    w1 = 1.0 / (1.0 + d)
    w2 = d / (1.0 + d)
    return gidx * EXPERTS_PER_GROUP + i1, gidx * EXPERTS_PER_GROUP + i2, gp * w1, gp * w2


def _post_mixer(x, y, alpha, g_ref, b_ref, rw_ref, rb_ref, h1_ref, eid_ref, gate_ref):
    h1 = _layer_norm(alpha * x + y, g_ref[...], b_ref[...])
    h1_ref[...] = h1
    e1, e2, g1, g2 = _route(h1.astype(BF16), rw_ref, rb_ref)
    row8 = lax.broadcasted_iota(I32, (SUBLANES, h1.shape[0]), 0)
    eid_ref[...] = jnp.where(row8 == 0, e1, jnp.where(row8 == 1, e2, -1))
    gate_ref[...] = jnp.where(row8 == 0, g1, jnp.where(row8 == 1, g2, 0.0))


def _load_history(ubuf, hist_ref, rows, hist_rows):
    j = pl.program_id(1)

    @pl.when(j == 0)
    def _():
        ubuf[0:hist_rows, :] = hist_ref[...]

    @pl.when(j > 0)
    def _():
        ubuf[0:hist_rows, :] = ubuf[rows:rows + hist_rows, :]


def _position_major(x_ref, xin):
    nseq, npos, d = x_ref.shape
    for c in range(d // LANES):
        for s in range(nseq):
            xin[c, pl.ds(s, npos, stride=nseq), :] = x_ref[s, :, c * LANES:(c + 1) * LANES]
    return jnp.concatenate([xin[c] for c in range(d // LANES)], axis=1)


def _by_sequence(rows, o_ref, zout):
    nseq, npos, d = o_ref.shape
    for c in range(d // LANES):
        zout[c] = rows[:, c * LANES:(c + 1) * LANES]
    for c in range(d // LANES):
        for s in range(nseq):
            o_ref[s, :, c * LANES:(c + 1) * LANES] = zout[c, pl.ds(s, npos, stride=nseq), :]


def _mixer_a_kernel(x_ref, hist_ref, win_ref, cw_ref, wout_ref, g_ref, b_ref, rw_ref, rb_ref,
                    h1_ref, eid_ref, gate_ref, tail_ref, ubuf, *maybe_xin, alpha, nseq):
    rows, d = h1_ref.shape
    hist_rows = (SC_WIDTH - 1) * nseq
    _load_history(ubuf, hist_ref, rows, hist_rows)
    x = _position_major(x_ref, *maybe_xin) if maybe_xin else x_ref[...]
    xb = x.astype(BF16)
    ubuf[hist_rows:hist_rows + rows, :] = _dot(xb, win_ref[:, d:2 * d]) * _dot(xb, win_ref[:, 2 * d:3 * d])
    tail_ref[...] = ubuf[rows:rows + hist_rows, :]
    conv = cw_ref[0:1, :] * ubuf[0:rows, :]
    for k in range(1, SC_WIDTH):
        conv = conv + cw_ref[k:k + 1, :] * ubuf[k * nseq:k * nseq + rows, :]
    bg = _dot(xb, win_ref[:, 0:d])
    y = _dot((bg * conv).astype(BF16), wout_ref[...])
    _post_mixer(x, y, alpha, g_ref, b_ref, rw_ref, rb_ref, h1_ref, eid_ref, gate_ref)


def _mixer_b_kernel(x_ref, hist_ref, w1_ref, b1_ref, cw_ref, cb_ref, lg_ref, lb_ref, w2_ref, b2_ref,
                    g_ref, b_ref, rw_ref, rb_ref,
                    h1_ref, eid_ref, gate_ref, tail_ref, ubuf, cbuf, *, alpha, nseq):
    rows, d = x_ref.shape
    hist_rows = (CF_WIDTH - 1) * nseq
    _load_history(ubuf, hist_ref, rows, hist_rows)
    x = x_ref[...]
    xb = x.astype(BF16)
    a = _dot(xb, w1_ref[:, 0:d]) + b1_ref[:, 0:d]
    gl = _dot(xb, w1_ref[:, d:2 * d]) + b1_ref[:, d:2 * d]
    ubuf[hist_rows:hist_rows + rows, :] = a * _sigmoid(gl)
    tail_ref[...] = ubuf[rows:rows + hist_rows, :]
    halves = CONV_ROWS // SUBLANES

    def chunk(c, carry):
        r0 = pl.multiple_of(c * CONV_ROWS, CONV_ROWS)
        acc = [jnp.broadcast_to(cb_ref[...], (SUBLANES, d)) for _ in range(halves)]
        for k in range(CF_WIDTH):
            w = cw_ref[k * SUBLANES:(k + 1) * SUBLANES, :]
            for h in range(halves):
                start = pl.multiple_of(r0 + h * SUBLANES + k * nseq, SUBLANES)
                acc[h] = acc[h] + w * ubuf[pl.ds(start, SUBLANES), :]
        for h in range(halves):
            cbuf[pl.ds(pl.multiple_of(r0 + h * SUBLANES, SUBLANES), SUBLANES), :] = acc[h]
        return carry

    lax.fori_loop(0, rows // CONV_ROWS, chunk, 0)
    cn = _layer_norm(cbuf[...], lg_ref[...], lb_ref[...])
    cn = cn * _sigmoid(cn)
    y = _dot(cn.astype(BF16), w2_ref[...]) + b2_ref[...]
    _post_mixer(x, y, alpha, g_ref, b_ref, rw_ref, rb_ref, h1_ref, eid_ref, gate_ref)


def _const_spec(shape):
    nd = len(shape)
    return pl.BlockSpec(shape, lambda *_: (0,) * nd, pipeline_mode=pl.Buffered(1))


def _mixer(kind, x2d, hist2d, p, *, pos_tile, seq_tile, nseq, alpha, by_sequence=False):
    d = x2d.shape[-1]
    n = x2d.size // d
    width = SC_WIDTH if kind == "a" else CF_WIDTH
    rows = pos_tile * seq_tile
    hist_rows = (width - 1) * seq_tile
    seq_blocks = nseq // seq_tile
    steps = n // (rows * seq_blocks)
    if kind == "a":
        kern = functools.partial(_mixer_a_kernel, alpha=alpha, nseq=seq_tile)
        weights = (p["w_in"], p["conv_w"], p["w_out"])
        wspecs = [_const_spec((d, 3 * d)), _const_spec((8, d)), _const_spec((d, d))]
        scratch = [pltpu.VMEM((rows + hist_rows, d), F32)]
        if by_sequence:
            scratch.append(pltpu.VMEM((d // LANES, rows, LANES), F32))
    else:
        assert not by_sequence
        kern = functools.partial(_mixer_b_kernel, alpha=alpha, nseq=seq_tile)
        weights = (p["w_pw1"], p["b_pw1"], p["conv_w"], p["conv_b"], p["cln_g"], p["cln_b"], p["w_pw2"], p["b_pw2"])
        wspecs = [_const_spec((d, 2 * d)), _const_spec((1, 2 * d)), _const_spec((CF_WIDTH * SUBLANES, d)),
                  _const_spec((1, d)), _const_spec((1, d)), _const_spec((1, d)), _const_spec((d, d)),
                  _const_spec((1, d))]
        scratch = [pltpu.VMEM((rows + hist_rows, d), F32), pltpu.VMEM((rows, d), F32)]
    common = (p["ln1_g"], p["ln1_b"], p["rw"], p["rb"])
    cspecs = [_const_spec((1, d)), _const_spec((1, d)), _const_spec((ROUTER_ROWS, d)), _const_spec((ROUTER_ROWS, 1))]
    if by_sequence:
        assert seq_blocks == 1
        x_spec = pl.BlockSpec((nseq, pos_tile, d), lambda s, j: (0, j, 0))
    else:
        x_spec = pl.BlockSpec((rows, d), lambda s, j: (s * steps + j, 0))
    return pl.pallas_call(
        kern,
        grid=(seq_blocks, steps),
        in_specs=[x_spec, pl.BlockSpec((hist_rows, d), lambda s, j: (s, 0))] + wspecs + cspecs,
        out_specs=[
            pl.BlockSpec((rows, d), lambda s, j: (s * steps + j, 0)),
            pl.BlockSpec((SUBLANES, rows), lambda s, j: (0, s * steps + j)),
            pl.BlockSpec((SUBLANES, rows), lambda s, j: (0, s * steps + j)),
            pl.BlockSpec((hist_rows, d), lambda s, j: (s, 0)),
        ],
        out_shape=[
            jax.ShapeDtypeStruct((n, d), F32),
            jax.ShapeDtypeStruct((SUBLANES, n), I32),
            jax.ShapeDtypeStruct((SUBLANES, n), F32),
            jax.ShapeDtypeStruct((seq_blocks * hist_rows, d), F32),
        ],
        scratch_shapes=scratch,
        compiler_params=pltpu.CompilerParams(
            dimension_semantics=("arbitrary", "arbitrary"), vmem_limit_bytes=VMEM_LIMIT),
        name="mixer_" + kind,
    )(x2d, hist2d, *weights, *common)


def _rank_kernel(eid_ref, tab_ref, blk_ref, tri, tot, prior, pad_start, pad_chunks, *, total_rows):
    phase, i = pl.program_id(0), pl.program_id(1)
    t = eid_ref.shape[1]
    row = lax.broadcasted_iota(I32, (N_EXPERTS, t), 0)
    oh1 = (row == eid_ref[0:1, :]).astype(F32)
    oh2 = (row == eid_ref[1:2, :]).astype(F32)
    c1 = jnp.sum(oh1, axis=1, keepdims=True)
    c2 = jnp.sum(oh2, axis=1, keepdims=True)
    chunks = jnp.ceil((c1 + c2) * (1.0 / ROW_ALIGN))
    tab_ref[...] = jnp.zeros_like(tab_ref)

    @pl.when(jnp.logical_and(phase == 0, i == 0))
    def _():
        tot[...] = jnp.zeros_like(tot)
        a = lax.broadcasted_iota(I32, tri.shape, 0)
        b = lax.broadcasted_iota(I32, tri.shape, 1)
        tri[...] = (a <= b).astype(BF16)

    @pl.when(phase == 0)
    def _():
        tot[...] += chunks * ROW_ALIGN

    @pl.when(jnp.logical_and(phase == 1, i == 0))
    def _():
        cnt = tot[...].astype(I32)
        seg = ((cnt + (MOE_BLOCK - 1)) // MOE_BLOCK) * MOE_BLOCK
        r128 = lax.broadcasted_iota(I32, cnt.shape, 0)
        base = jnp.zeros_like(cnt)
        for e in range(N_EXPERTS - 1):
            base = base + jnp.where(r128 > e, seg[e:e + 1, :], 0)
        prior[...] = base.astype(F32)
        pad_start[...] = (base + cnt).astype(F32)
        pad_end = jnp.where(r128 == N_EXPERTS - 1, total_rows, base + seg)
        pad_chunks[...] = ((pad_end - base - cnt) // ROW_ALIGN).astype(F32)
        end_blk = (base + seg) // MOE_BLOCK
        lane = lax.broadcasted_iota(I32, (N_EXPERTS, BLOCK_LANES), 1)
        expert_of = jnp.sum((end_blk[:, 0:1] <= lane).astype(F32), axis=0, keepdims=True).astype(I32)
        active = (lane[0:1, :] < end_blk[N_EXPERTS - 1:N_EXPERTS, 0:1]).astype(I32)
        r8 = lax.broadcasted_iota(I32, (SUBLANES, BLOCK_LANES), 0)
        blk_ref[...] = jnp.where(r8 == 0, jnp.minimum(expert_of, N_EXPERTS - 1), jnp.where(r8 == 1, active, 0))

    @pl.when(phase == 1)
    def _():
        ohs = jnp.concatenate([oh1, oh2], axis=0).astype(BF16)
        groups, run = [], jnp.zeros((2 * N_EXPERTS, 1), F32)
        for g in range(t // LANES):
            part = _dot(ohs[:, g * LANES:(g + 1) * LANES], tri[...])
            groups.append(part + run)
            run = run + part[:, LANES - 1:LANES]
        cs = jnp.concatenate(groups, axis=1)
        ea = lax.broadcasted_iota(I32, (N_EXPERTS, N_EXPERTS), 0)
        eb = lax.broadcasted_iota(I32, (N_EXPERTS, N_EXPERTS), 1)
        chunks_b = jnp.broadcast_to(chunks, (N_EXPERTS, LANES))
        start = _dot((eb < ea).astype(BF16), chunks_b.astype(BF16)) * ROW_ALIGN
        s0 = start[:, 0:1]
        pos1 = jnp.sum(oh1 * (s0 + cs[0:N_EXPERTS, :] - 1.0), axis=0, keepdims=True)
        pos2 = jnp.sum(oh2 * (s0 + c1 + cs[N_EXPERTS:2 * N_EXPERTS, :] - 1.0), axis=0, keepdims=True)
        pos1 = (pos1 * SUBLANES).astype(I32)
        pos2 = (pos2 * SUBLANES).astype(I32)
        for q in range(POS_ROWS):
            tab_ref[q:q + 1, :] = pos1[:, q * LANES:(q + 1) * LANES]
            tab_ref[POS_ROWS + q:POS_ROWS + q + 1, :] = pos2[:, q * LANES:(q + 1) * LANES]
        diag = lax.broadcasted_iota(I32, (N_EXPERTS, LANES), 0) == lax.broadcasted_iota(I32, (N_EXPERTS, LANES), 1)
        to_lanes = lambda col: jnp.sum(jnp.where(diag, col, 0.0), axis=0, keepdims=True).astype(I32)
        tab_ref[RUN_START_ROW:RUN_START_ROW + 1, :] = to_lanes(start)
        tab_ref[RUN_CHUNKS_ROW:RUN_CHUNKS_ROW + 1, :] = to_lanes(chunks_b)
        tab_ref[RUN_DEST_ROW:RUN_DEST_ROW + 1, :] = to_lanes(prior[...])
        tab_ref[PAD_START_ROW:PAD_START_ROW + 1, :] = to_lanes(pad_start[...])
        tab_ref[PAD_CHUNKS_ROW:PAD_CHUNKS_ROW + 1, :] = to_lanes(pad_chunks[...])
        prior[...] += chunks * ROW_ALIGN


def _rank(eid, total_rows):
    n = eid.shape[1]
    tiles = n // TOK_TILE
    return pl.pallas_call(
        functools.partial(_rank_kernel, total_rows=total_rows),
        grid=(2, tiles),
        in_specs=[pl.BlockSpec((SUBLANES, TOK_TILE), lambda ph, i: (0, i))],
        out_specs=[
            pl.BlockSpec((None, TABLE_ROWS, LANES), lambda ph, i: (i * ph, 0, 0)),
            pl.BlockSpec((SUBLANES, BLOCK_LANES), lambda ph, i: (0, 0)),
        ],
        out_shape=[
            jax.ShapeDtypeStruct((tiles, TABLE_ROWS, LANES), I32),
            jax.ShapeDtypeStruct((SUBLANES, BLOCK_LANES), I32),
        ],
        scratch_shapes=[pltpu.VMEM((LANES, LANES), BF16)] + [pltpu.VMEM((N_EXPERTS, LANES), F32)] * 4,
        compiler_params=pltpu.CompilerParams(
            dimension_semantics=("arbitrary", "arbitrary"), vmem_limit_bytes=VMEM_LIMIT),
        name="rank",
    )(eid)


def _tile_at(ref, row):
    return ref.at[pl.ds(pl.multiple_of(row, SUBLANES), SUBLANES), :]


def _token_rows(ref, tok):
    return _tile_at(ref, tok * SUBLANES)


def _for_each_chunk(tab, fn, rows=(RUN_START_ROW, RUN_CHUNKS_ROW, RUN_DEST_ROW)):
    start_row, chunks_row, dest_row = rows
    per_big = BIG_CHUNK_ROWS // ROW_ALIGN
    total_big, total_small = 0, 0
    for e in range(N_EXPERTS):
        start, dest, chunks = tab[start_row, e], tab[dest_row, e], tab[chunks_row, e]
        nbig = chunks // per_big

        def piece(nrows, start=start, dest=dest):
            def body(q, carry):
                fn(pl.multiple_of(start + q * nrows, ROW_ALIGN), pl.multiple_of(dest + q * nrows, ROW_ALIGN), nrows)
                return carry
            return body

        lax.fori_loop(0, nbig, piece(BIG_CHUNK_ROWS), 0)
        lax.fori_loop(nbig * per_big, chunks, piece(ROW_ALIGN), 0)
        total_big, total_small = total_big + nbig, total_small + chunks - nbig * per_big
    return total_big, total_small


def _wait_pieces(count_ref, copy):
    for k, nrows in enumerate((BIG_CHUNK_ROWS, ROW_ALIGN)):
        def body(q, carry, nrows=nrows):
            copy(0, 0, nrows).wait()
            return carry
        lax.fori_loop(0, count_ref[0, k], body, 0)


def _dispatch_kernel(*refs, tile_base, aliased, fill_pads):
    if aliased:
        tab_hbm, h_ref, _, xs_hbm, tab, pending, tokbuf, loc, xl, zrows, sem_tab, sem_row = refs
    else:
        tab_hbm, h_ref, xs_hbm, tab, pending, tokbuf, loc, xl, zrows, sem_tab, sem_row = refs
    t = h_ref.shape[0]
    i = pl.program_id(0)
    cp = pltpu.make_async_copy(tab_hbm.at[tile_base + i], tab, sem_tab)
    cp.start()
    for c in range(SUBLANES):
        tokbuf[pl.ds(c, t, stride=SUBLANES), :] = h_ref[:, c * LANES:(c + 1) * LANES]

    @pl.when(i == 0)
    def _():
        loc[...] = jnp.zeros_like(loc)

    cp.wait()

    def place(q, carry):
        for lane in range(LANES):
            v = _token_rows(tokbuf, q * LANES + lane)[...]
            _tile_at(loc, tab[q, lane])[...] = v
            _tile_at(loc, tab[POS_ROWS + q, lane])[...] = v
        return carry

    lax.fori_loop(0, t // LANES, place, 0)

    def chunk_copy(local_row, global_row, nrows):
        return pltpu.make_async_copy(xl.at[pl.ds(local_row, nrows), :],
                                     xs_hbm.at[pl.ds(global_row, nrows), :], sem_row)

    @pl.when(i > 0)
    def _():
        _wait_pieces(pending, chunk_copy)

    for c in range(SUBLANES):
        xl[:, c * LANES:(c + 1) * LANES] = loc[pl.ds(c, LOCAL_ROWS, stride=SUBLANES), :].astype(BF16)
    pending[0, 0], pending[0, 1] = _for_each_chunk(tab, lambda *a: chunk_copy(*a).start())

    @pl.when(i == pl.num_programs(0) - 1)
    def _():
        _wait_pieces(pending, chunk_copy)

    if fill_pads:
        @pl.when(i == pl.num_programs(0) - 1)
        def _():
            zrows[...] = jnp.zeros_like(zrows)
            zero_copy = lambda _, g, nrows: pltpu.make_async_copy(
                zrows.at[pl.ds(0, nrows), :], xs_hbm.at[pl.ds(g, nrows), :], sem_row)
            pad_rows = (PAD_START_ROW, PAD_CHUNKS_ROW, PAD_START_ROW)
            pending[0, 0], pending[0, 1] = _for_each_chunk(tab, lambda *a: zero_copy(*a).start(), pad_rows)
            _wait_pieces(pending, zero_copy)


def _dispatch(tabs, h2d, xs, *, tile, tile_base, sorted_rows, fill_pads):
    n, d = h2d.shape
    aliased = xs is not None
    kern = functools.partial(_dispatch_kernel, tile_base=tile_base, aliased=aliased, fill_pads=fill_pads)
    in_specs = [pl.BlockSpec(memory_space=pl.ANY), pl.BlockSpec((tile, d), lambda i: (i, 0))]
    args = [tabs, h2d]
    if aliased:
        in_specs.append(pl.BlockSpec(memory_space=pl.ANY))
        args.append(xs)
    return pl.pallas_call(
        kern,
        grid=(n // tile,),
        in_specs=in_specs,
        out_specs=pl.BlockSpec(memory_space=pl.ANY),
        out_shape=jax.ShapeDtypeStruct((sorted_rows, d), BF16),
        scratch_shapes=[pltpu.SMEM((TABLE_ROWS, LANES), I32), pltpu.SMEM((SUBLANES, LANES), I32),
                        pltpu.VMEM((tile * SUBLANES, LANES), F32),
                        pltpu.VMEM((LOCAL_ROWS * SUBLANES, LANES), F32), pltpu.VMEM((LOCAL_ROWS, d), BF16),
                        pltpu.VMEM((BIG_CHUNK_ROWS, d), BF16), pltpu.SemaphoreType.DMA, pltpu.SemaphoreType.DMA],
        input_output_aliases={2: 0} if aliased else {},
        compiler_params=pltpu.CompilerParams(
            dimension_semantics=("arbitrary",), vmem_limit_bytes=VMEM_LIMIT),
        name="dispatch",
    )(*args)


def _combine_kernel(tab_hbm, gate_hbm, ys_hbm, h_ref, g_ref, b_ref, o_ref,
                    tabs, pending, gates, yl, ytok, ztok, sem_tab, sem_gate, sem_row, *maybe_zout, tile_base, alpha):
    t = h_ref.shape[0]
    i = pl.program_id(0)
    last = pl.num_programs(0) - 1
    slot = i % 2
    tab = tabs.at[slot]
    table_copy = lambda tile, s: pltpu.make_async_copy(tab_hbm.at[tile_base + tile], tabs.at[s], sem_tab.at[s])
    cp_gate = pltpu.make_async_copy(gate_hbm.at[:, pl.ds(pl.multiple_of(i * t, LANES), t)], gates, sem_gate)

    def chunk_copy(local_row, global_row, nrows):
        return pltpu.make_async_copy(ys_hbm.at[pl.ds(global_row, nrows), :],
                                     yl.at[pl.ds(local_row, nrows), :], sem_row)

    @pl.when(i == 0)
    def _():
        yl[...] = jnp.zeros_like(yl)
        table_copy(0, 0).start()
        table_copy(0, 0).wait()
        pending[0, 0], pending[0, 1] = _for_each_chunk(tabs.at[0], lambda *a: chunk_copy(*a).start())

    @pl.when(i < last)
    def _():
        table_copy(i + 1, 1 - slot).start()

    cp_gate.start()
    _wait_pieces(pending, chunk_copy)
    for c in range(SUBLANES):
        ytok[pl.ds(c, LOCAL_ROWS, stride=SUBLANES), :] = yl[:, c * LANES:(c + 1) * LANES].astype(F32)

    @pl.when(i < last)
    def _():
        table_copy(i + 1, 1 - slot).wait()
        pending[0, 0], pending[0, 1] = _for_each_chunk(tabs.at[1 - slot], lambda *a: chunk_copy(*a).start())

    cp_gate.wait()

    def gather(q, carry):
        for lane in range(LANES):
            n = q * LANES + lane
            _token_rows(ztok, n)[...] = (gates[0, n] * _tile_at(ytok, tab[q, lane])[...]
                                         + gates[1, n] * _tile_at(ytok, tab[POS_ROWS + q, lane])[...])
        return carry

    lax.fori_loop(0, t // LANES, gather, 0)
    f = jnp.concatenate([ztok[pl.ds(c, t, stride=SUBLANES), :] for c in range(SUBLANES)], axis=1)
    out = _layer_norm(alpha * h_ref[...] + f, g_ref[...], b_ref[...])
    if maybe_zout:
        _by_sequence(out, o_ref, *maybe_zout)
    else:
        o_ref[...] = out


def _combine(tabs, gate, ys, h2d, p, *, tile, tile_base, alpha, out_sequences=None):
    n, d = h2d.shape
    kern = functools.partial(_combine_kernel, tile_base=tile_base, alpha=alpha)
    scratch = [pltpu.SMEM((2, TABLE_ROWS, LANES), I32), pltpu.SMEM((SUBLANES, LANES), I32),
               pltpu.SMEM((SUBLANES, tile), F32),
               pltpu.VMEM((LOCAL_ROWS, d), BF16), pltpu.VMEM((LOCAL_ROWS * SUBLANES, LANES), F32),
               pltpu.VMEM((tile * SUBLANES, LANES), F32),
               pltpu.SemaphoreType.DMA((2,)), pltpu.SemaphoreType.DMA, pltpu.SemaphoreType.DMA]
    if out_sequences:
        npos = tile // out_sequences
        out_spec = pl.BlockSpec((out_sequences, npos, d), lambda i: (0, i, 0))
        out_shape = jax.ShapeDtypeStruct((out_sequences, n // out_sequences, d), F32)
        scratch.append(pltpu.VMEM((d // LANES, tile, LANES), F32))
    else:
        out_spec = pl.BlockSpec((tile, d), lambda i: (i, 0))
        out_shape = jax.ShapeDtypeStruct((n, d), F32)
    return pl.pallas_call(
        kern,
        grid=(n // tile,),
        in_specs=[
            pl.BlockSpec(memory_space=pl.ANY), pl.BlockSpec(memory_space=pl.ANY), pl.BlockSpec(memory_space=pl.ANY),
            pl.BlockSpec((tile, d), lambda i: (i, 0)),
            _const_spec((1, d)), _const_spec((1, d)),
        ],
        out_specs=out_spec,
        out_shape=out_shape,
        scratch_shapes=scratch,
        compiler_params=pltpu.CompilerParams(
            dimension_semantics=("arbitrary",), vmem_limit_bytes=VMEM_LIMIT),
        name="combine",
    )(tabs, gate, ys, h2d, p["ln2_g"], p["ln2_b"])


def _moe_kernel(blk_ref, x_ref, wg_ref, wu_ref, wd_ref, y_ref, wgb, wub, wdb):
    b = pl.program_id(0)

    @pl.when(jnp.logical_or(b == 0, blk_ref[0, b] != blk_ref[0, jnp.maximum(b - 1, 0)]))
    def _():
        wgb[...] = wg_ref[...].astype(BF16)
        wub[...] = wu_ref[...].astype(BF16)
        wdb[...] = wd_ref[...].astype(BF16)

    @pl.when(blk_ref[1, b] == 1)
    def _():
        x = x_ref[...]
        hg = _dot(x, wgb[...])
        hu = _dot(x, wub[...])
        hid = (hg * _sigmoid(hg) * hu).astype(BF16)
        y_ref[...] = _dot(hid, wdb[...]).astype(BF16)

    @pl.when(blk_ref[1, b] == 0)
    def _():
        y_ref[...] = jnp.zeros_like(y_ref)


def _moe(blk, xs, wg, wu, wd, layer):
    rows, d = xs.shape
    nblocks = rows // MOE_BLOCK
    f = wg.shape[-1]
    block = lambda b, blk: (jnp.where(blk[1, b] == 1, b, nblocks - 1), 0)
    return pl.pallas_call(
        _moe_kernel,
        grid_spec=pltpu.PrefetchScalarGridSpec(
            num_scalar_prefetch=1,
            grid=(nblocks,),
            in_specs=[
                pl.BlockSpec((MOE_BLOCK, d), block),
                pl.BlockSpec((None, None, d, f), lambda b, blk: (layer, blk[0, b], 0, 0)),
                pl.BlockSpec((None, None, d, f), lambda b, blk: (layer, blk[0, b], 0, 0)),
                pl.BlockSpec((None, None, f, d), lambda b, blk: (layer, blk[0, b], 0, 0)),
            ],
            out_specs=pl.BlockSpec((MOE_BLOCK, d), lambda b, blk: (b, 0)),
            scratch_shapes=[pltpu.VMEM((d, f), BF16), pltpu.VMEM((d, f), BF16), pltpu.VMEM((f, d), BF16)],
        ),
        out_shape=jax.ShapeDtypeStruct((rows, d), BF16),
        compiler_params=pltpu.CompilerParams(
            dimension_semantics=("arbitrary",), vmem_limit_bytes=VMEM_LIMIT),
        name="moe",
    )(blk, xs, wg, wu, wd)


def _sparse_moe(sets, p, *, layer, alpha, out_sequences=None):
    eids, bases, tiles = [], [], 0
    for h, e, _ in sets:
        n = h.shape[0]
        n_pad = -(-n // TOK_TILE) * TOK_TILE
        eids.append(jnp.concatenate([e, jnp.full((SUBLANES, n_pad - n), -1, I32)], axis=1))
        bases.append(tiles)
        tiles += n_pad // TOK_TILE
    n_tok = sum(h.shape[0] for h, _, _ in sets)
    run_pad = tiles * N_EXPERTS * (ROW_ALIGN - 1)
    nblocks = -(-(2 * n_tok + run_pad + N_EXPERTS * (MOE_BLOCK - 1)) // MOE_BLOCK)
    assert nblocks <= BLOCK_LANES
    sorted_rows = nblocks * MOE_BLOCK
    tabs, blk = _rank(jnp.concatenate(eids, axis=1), sorted_rows)
    xs = None
    for k, ((h, _, _), tb) in enumerate(zip(sets, bases)):
        xs = _dispatch(tabs, h, xs, tile=min(TOK_TILE, h.shape[0]), tile_base=tb, sorted_rows=sorted_rows,
                       fill_pads=k == len(sets) - 1)
    ys = _moe(blk, xs, p["w_gate"], p["w_up"], p["w_down"], layer)
    return [_combine(tabs, g, ys, h, p, tile=min(TOK_TILE, h.shape[0]), tile_base=tb, alpha=alpha,
                     out_sequences=out_sequences if k == 0 else None)
            for k, ((h, _, g), tb) in enumerate(zip(sets, bases))]


def _pad_rows(w, rows):
    return jnp.concatenate([w, jnp.zeros((rows - w.shape[0],) + w.shape[1:], w.dtype)], axis=0)


def _router_params(gw, gb, ew, eb):
    d = gw.shape[0]
    rw = jnp.concatenate([gw.T, jnp.zeros((8 - N_GROUPS, d), gw.dtype), ew.T], axis=0).astype(BF16)
    rb = jnp.concatenate([gb, jnp.zeros((8 - N_GROUPS,), gb.dtype), eb])[:, None].astype(F32)
    return rw, rb


def _to_blocks(x, seq_tile):
    nseq, npos, d = x.shape
    return x.reshape(nseq // seq_tile, seq_tile, npos, d).swapaxes(1, 2).reshape(nseq * npos, d)


def _from_blocks(x2d, nseq, seq_tile):
    n, d = x2d.shape
    npos = n // nseq
    return x2d.reshape(nseq // seq_tile, npos, seq_tile, d).swapaxes(1, 2).reshape(nseq, npos, d)


def kernel(x_prompt, x_sample, state_conv_a, state_conv_b, meta_tokens, sc_w_in, sc_conv_w, sc_w_out, cf_w_pw1, cf_b_pw1, cf_conv_w, cf_conv_b, cf_ln_g, cf_ln_b, cf_w_pw2, cf_b_pw2, ln1_g, ln1_b, ln2_g, ln2_b, rt_group_w, rt_group_b, rt_expert_w, rt_expert_b, moe_w_gate, moe_w_up, moe_w_down):
    bsz, seq, d = x_prompt.shape
    dec_b, dec_t, _ = x_sample.shape
    n_meta = meta_tokens.shape[0]
    depth = ln1_g.shape[0]
    alpha = float((2 * depth) ** 0.25)
    row = lambda v: v[None, :].astype(F32)
    s_seq = 32
    p_pos = 512 // bsz

    layer = []
    for i in range(depth):
        rw, rb = _router_params(rt_group_w[i], rt_group_b[i], rt_expert_w[i], rt_expert_b[i])
        layer.append(dict(
            ln1_g=row(ln1_g[i]), ln1_b=row(ln1_b[i]), ln2_g=row(ln2_g[i]), ln2_b=row(ln2_b[i]), rw=rw, rb=rb,
            w_gate=moe_w_gate, w_up=moe_w_up, w_down=moe_w_down))
    pa = dict(layer[0], w_in=sc_w_in[0].astype(BF16), conv_w=_pad_rows(sc_conv_w[0], 8),
              w_out=sc_w_out[0].astype(BF16))
    pb = dict(layer[1], w_pw1=cf_w_pw1[0].astype(BF16), b_pw1=row(cf_b_pw1[0]),
              conv_w=jnp.repeat(cf_conv_w[0].astype(F32), SUBLANES, axis=0), conv_b=row(cf_conv_b[0]),
              cln_g=row(cf_ln_g[0]), cln_b=row(cf_ln_b[0]), w_pw2=cf_w_pw2[0].astype(BF16), b_pw2=row(cf_b_pw2[0]))

    xs = _to_blocks(x_sample, s_seq)
    xm = jnp.broadcast_to(meta_tokens.astype(F32)[:, None, :], (n_meta, bsz, d)).reshape(n_meta * bsz, d)
    mix_p = dict(pos_tile=p_pos, seq_tile=bsz, nseq=bsz, alpha=alpha)
    mix_s = dict(pos_tile=dec_t, seq_tile=s_seq, nseq=dec_b, alpha=alpha)
    mix_m = dict(pos_tile=n_meta, seq_tile=bsz, nseq=bsz, alpha=alpha)

    zero_a = jnp.zeros(((SC_WIDTH - 1) * bsz, d), F32)
    hm, em, gm, tail_am = _mixer("a", xm, zero_a, pa, **mix_m)
    hp, ep, gp, tail_ap = _mixer("a", x_prompt, tail_am, pa, by_sequence=True, **mix_p)
    hs, es, gs, tail_as = _mixer("a", xs, _to_blocks(state_conv_a[0], s_seq), pa, **mix_s)
    hp, hs, hm = _sparse_moe([(hp, ep, gp), (hs, es, gs), (hm, em, gm)], pa, layer=0, alpha=alpha)

    zero_b = jnp.zeros(((CF_WIDTH - 1) * bsz, d), F32)
    _, _, _, tail_bm = _mixer("b", hm, zero_b, pb, **mix_m)
    hp, ep, gp, tail_bp = _mixer("b", hp, tail_bm, pb, **mix_p)
    hs, es, gs, tail_bs = _mixer("b", hs, _to_blocks(state_conv_b[0], s_seq), pb, **mix_s)
    y_prompt, hs = _sparse_moe([(hp, ep, gp), (hs, es, gs)], pb, layer=1, alpha=alpha, out_sequences=bsz)

    return (y_prompt, _from_blocks(hs, dec_b, s_seq),
            _from_blocks(tail_ap, bsz, bsz)[None], _from_blocks(tail_bp, bsz, bsz)[None],
            _from_blocks(tail_as, dec_b, s_seq)[None], _from_blocks(tail_bs, dec_b, s_seq)[None])
```

```python
import functools

import jax
import jax.numpy as jnp
from jax import lax
from jax.experimental import pallas as pl
from jax.experimental.pallas import tpu as pltpu

F32 = jnp.float32
BF16 = jnp.bfloat16
I32 = jnp.int32

LN_EPS = 1e-5
N_GROUPS = 4
EXPERTS_PER_GROUP = 8
N_EXPERTS = N_GROUPS * EXPERTS_PER_GROUP
SC_WIDTH = 3
CF_WIDTH = 31
ROUTER_ROWS = 8 + N_EXPERTS
SUBLANES = 8
LANES = 128
ROW_ALIGN = 16
BIG_CHUNK_ROWS = 64
CONV_ROWS = 16
VMEM_LIMIT = 56 * 1024 * 1024
MOE_BLOCK = 512
TOK_TILE = 1024
LOCAL_ROWS = 2 * TOK_TILE + N_EXPERTS * ROW_ALIGN
POS_ROWS = TOK_TILE // LANES
RUN_START_ROW, RUN_CHUNKS_ROW, RUN_DEST_ROW = 2 * POS_ROWS, 2 * POS_ROWS + 1, 2 * POS_ROWS + 2
PAD_START_ROW, PAD_CHUNKS_ROW = 2 * POS_ROWS + 3, 2 * POS_ROWS + 4
TABLE_ROWS = 2 * POS_ROWS + SUBLANES
BLOCK_LANES = 256


def _dot(a, b):
    return jnp.dot(a, b, preferred_element_type=F32)


def _layer_norm(z, g, b):
    mu = jnp.mean(z, axis=-1, keepdims=True)
    zc = z - mu
    var = jnp.mean(zc * zc, axis=-1, keepdims=True)
    return zc * lax.rsqrt(var + LN_EPS) * g + b


def _sigmoid(x):
    return 1.0 / (1.0 + jnp.exp(-x))


def _route(h1b, rw_ref, rb_ref):
    t = h1b.shape[0]
    logits = lax.dot_general(rw_ref[...], h1b, (((1,), (1,)), ((), ())),
                             preferred_element_type=F32) + rb_ref[...]
    g = [logits[i:i + 1, :] for i in range(N_GROUPS)]
    gmax = jnp.maximum(jnp.maximum(g[0], g[1]), jnp.maximum(g[2], g[3]))
    gidx = jnp.where(g[0] == gmax, 0, jnp.where(g[1] == gmax, 1, jnp.where(g[2] == gmax, 2, 3)))
    gsum = (jnp.exp(g[0] - gmax) + jnp.exp(g[1] - gmax)) + (jnp.exp(g[2] - gmax) + jnp.exp(g[3] - gmax))
    gp = 1.0 / gsum
    sel = logits[8 + 8 * (N_GROUPS - 1):8 + 8 * N_GROUPS, :]
    for gi in range(N_GROUPS - 2, -1, -1):
        sel = jnp.where(gidx == gi, logits[8 + 8 * gi:16 + 8 * gi, :], sel)
    row = lax.broadcasted_iota(I32, (EXPERTS_PER_GROUP, t), 0)
    m1 = jnp.max(sel, axis=0, keepdims=True)
    i1 = jnp.min(jnp.where(sel == m1, row, EXPERTS_PER_GROUP), axis=0, keepdims=True)
    sel2 = jnp.where(row == i1, -jnp.inf, sel)
    m2 = jnp.max(sel2, axis=0, keepdims=True)
    i2 = jnp.min(jnp.where(sel2 == m2, row, EXPERTS_PER_GROUP), axis=0, keepdims=True)
    d = jnp.exp(m2 - m1)
    w1 = 1.0 / (1.0 + d)
    w2 = d / (1.0 + d)
    return gidx * EXPERTS_PER_GROUP + i1, gidx * EXPERTS_PER_GROUP + i2, gp * w1, gp * w2


def _post_mixer(x, y, alpha, g_ref, b_ref, rw_ref, rb_ref, h1_ref, eid_ref, gate_ref):
    h1 = _layer_norm(alpha * x + y, g_ref[...], b_ref[...])
    h1_ref[...] = h1
    e1, e2, g1, g2 = _route(h1.astype(BF16), rw_ref, rb_ref)
    row8 = lax.broadcasted_iota(I32, (SUBLANES, h1.shape[0]), 0)
    eid_ref[...] = jnp.where(row8 == 0, e1, jnp.where(row8 == 1, e2, -1))
    gate_ref[...] = jnp.where(row8 == 0, g1, jnp.where(row8 == 1, g2, 0.0))


def _load_history(ubuf, hist_ref, rows, hist_rows):
    j = pl.program_id(1)

    @pl.when(j == 0)
    def _():
        ubuf[0:hist_rows, :] = hist_ref[...]

    @pl.when(j > 0)
    def _():
        ubuf[0:hist_rows, :] = ubuf[rows:rows + hist_rows, :]


def _position_major(x_ref, xin):
    nseq, npos, d = x_ref.shape
    for c in range(d // LANES):
        for s in range(nseq):
            xin[c, pl.ds(s, npos, stride=nseq), :] = x_ref[s, :, c * LANES:(c + 1) * LANES]
    return jnp.concatenate([xin[c] for c in range(d // LANES)], axis=1)


def _by_sequence(rows, o_ref, zout):
    nseq, npos, d = o_ref.shape
    for c in range(d // LANES):
        zout[c] = rows[:, c * LANES:(c + 1) * LANES]
    for c in range(d // LANES):
        for s in range(nseq):
            o_ref[s, :, c * LANES:(c + 1) * LANES] = zout[c, pl.ds(s, npos, stride=nseq), :]


def _mixer_a_kernel(x_ref, hist_ref, win_ref, cw_ref, wout_ref, g_ref, b_ref, rw_ref, rb_ref,
                    h1_ref, eid_ref, gate_ref, tail_ref, ubuf, *maybe_xin, alpha, nseq):
    rows, d = h1_ref.shape
    hist_rows = (SC_WIDTH - 1) * nseq
    _load_history(ubuf, hist_ref, rows, hist_rows)
    x = _position_major(x_ref, *maybe_xin) if maybe_xin else x_ref[...]
    xb = x.astype(BF16)
    ubuf[hist_rows:hist_rows + rows, :] = _dot(xb, win_ref[:, d:2 * d]) * _dot(xb, win_ref[:, 2 * d:3 * d])
    tail_ref[...] = ubuf[rows:rows + hist_rows, :]
    conv = cw_ref[0:1, :] * ubuf[0:rows, :]
    for k in range(1, SC_WIDTH):
        conv = conv + cw_ref[k:k + 1, :] * ubuf[k * nseq:k * nseq + rows, :]
    bg = _dot(xb, win_ref[:, 0:d])
    y = _dot((bg * conv).astype(BF16), wout_ref[...])
    _post_mixer(x, y, alpha, g_ref, b_ref, rw_ref, rb_ref, h1_ref, eid_ref, gate_ref)


def _mixer_b_kernel(x_ref, hist_ref, w1_ref, b1_ref, cw_ref, cb_ref, lg_ref, lb_ref, w2_ref, b2_ref,
                    g_ref, b_ref, rw_ref, rb_ref,
                    h1_ref, eid_ref, gate_ref, tail_ref, ubuf, cbuf, *, alpha, nseq):
    rows, d = x_ref.shape
    hist_rows = (CF_WIDTH - 1) * nseq
    _load_history(ubuf, hist_ref, rows, hist_rows)
    x = x_ref[...]
    xb = x.astype(BF16)
    a = _dot(xb, w1_ref[:, 0:d]) + b1_ref[:, 0:d]
    gl = _dot(xb, w1_ref[:, d:2 * d]) + b1_ref[:, d:2 * d]
    ubuf[hist_rows:hist_rows + rows, :] = a * _sigmoid(gl)
    tail_ref[...] = ubuf[rows:rows + hist_rows, :]
    halves = CONV_ROWS // SUBLANES

    def chunk(c, carry):
        r0 = pl.multiple_of(c * CONV_ROWS, CONV_ROWS)
        acc = [jnp.broadcast_to(cb_ref[...], (SUBLANES, d)) for _ in range(halves)]
        for k in range(CF_WIDTH):
            w = cw_ref[k * SUBLANES:(k + 1) * SUBLANES, :]
            for h in range(halves):
                start = pl.multiple_of(r0 + h * SUBLANES + k * nseq, SUBLANES)
                acc[h] = acc[h] + w * ubuf[pl.ds(start, SUBLANES), :]
        for h in range(halves):
            cbuf[pl.ds(pl.multiple_of(r0 + h * SUBLANES, SUBLANES), SUBLANES), :] = acc[h]
        return carry

    lax.fori_loop(0, rows // CONV_ROWS, chunk, 0)
    cn = _layer_norm(cbuf[...], lg_ref[...], lb_ref[...])
    cn = cn * _sigmoid(cn)
    y = _dot(cn.astype(BF16), w2_ref[...]) + b2_ref[...]
    _post_mixer(x, y, alpha, g_ref, b_ref, rw_ref, rb_ref, h1_ref, eid_ref, gate_ref)


def _const_spec(shape):
    nd = len(shape)
    return pl.BlockSpec(shape, lambda *_: (0,) * nd, pipeline_mode=pl.Buffered(1))


def _mixer(kind, x2d, hist2d, p, *, pos_tile, seq_tile, nseq, alpha, by_sequence=False):
    d = x2d.shape[-1]
    n = x2d.size // d
    width = SC_WIDTH if kind == "a" else CF_WIDTH
    rows = pos_tile * seq_tile
    hist_rows = (width - 1) * seq_tile
    seq_blocks = nseq // seq_tile
    steps = n // (rows * seq_blocks)
    if kind == "a":
        kern = functools.partial(_mixer_a_kernel, alpha=alpha, nseq=seq_tile)
        weights = (p["w_in"], p["conv_w"], p["w_out"])
        wspecs = [_const_spec((d, 3 * d)), _const_spec((8, d)), _const_spec((d, d))]
        scratch = [pltpu.VMEM((rows + hist_rows, d), F32)]
        if by_sequence:
            scratch.append(pltpu.VMEM((d // LANES, rows, LANES), F32))
    else:
        assert not by_sequence
        kern = functools.partial(_mixer_b_kernel, alpha=alpha, nseq=seq_tile)
        weights = (p["w_pw1"], p["b_pw1"], p["conv_w"], p["conv_b"], p["cln_g"], p["cln_b"], p["w_pw2"], p["b_pw2"])
        wspecs = [_const_spec((d, 2 * d)), _const_spec((1, 2 * d)), _const_spec((CF_WIDTH * SUBLANES, d)),
                  _const_spec((1, d)), _const_spec((1, d)), _const_spec((1, d)), _const_spec((d, d)),
                  _const_spec((1, d))]
        scratch = [pltpu.VMEM((rows + hist_rows, d), F32), pltpu.VMEM((rows, d), F32)]
    common = (p["ln1_g"], p["ln1_b"], p["rw"], p["rb"])
    cspecs = [_const_spec((1, d)), _const_spec((1, d)), _const_spec((ROUTER_ROWS, d)), _const_spec((ROUTER_ROWS, 1))]
    if by_sequence:
        assert seq_blocks == 1
        x_spec = pl.BlockSpec((nseq, pos_tile, d), lambda s, j: (0, j, 0))
    else:
        x_spec = pl.BlockSpec((rows, d), lambda s, j: (s * steps + j, 0))
    return pl.pallas_call(
        kern,
        grid=(seq_blocks, steps),
        in_specs=[x_spec, pl.BlockSpec((hist_rows, d), lambda s, j: (s, 0))] + wspecs + cspecs,
        out_specs=[
            pl.BlockSpec((rows, d), lambda s, j: (s * steps + j, 0)),
            pl.BlockSpec((SUBLANES, rows), lambda s, j: (0, s * steps + j)),
            pl.BlockSpec((SUBLANES, rows), lambda s, j: (0, s * steps + j)),
            pl.BlockSpec((hist_rows, d), lambda s, j: (s, 0)),
        ],
        out_shape=[
            jax.ShapeDtypeStruct((n, d), F32),
            jax.ShapeDtypeStruct((SUBLANES, n), I32),
            jax.ShapeDtypeStruct((SUBLANES, n), F32),
            jax.ShapeDtypeStruct((seq_blocks * hist_rows, d), F32),
        ],
        scratch_shapes=scratch,
        compiler_params=pltpu.CompilerParams(
            dimension_semantics=("arbitrary", "arbitrary"), vmem_limit_bytes=VMEM_LIMIT),
        name="mixer_" + kind,
    )(x2d, hist2d, *weights, *common)


def _rank_kernel(eid_ref, eid_all_ref, tab_ref, blk_ref, tri, prior, pad_start, pad_chunks, *, total_rows):
    i = pl.program_id(0)
    t = eid_ref.shape[1]
    row = lax.broadcasted_iota(I32, (N_EXPERTS, t), 0)

    def pair_counts(e1, e2):
        oh1 = (row == e1).astype(F32)
        oh2 = (row == e2).astype(F32)
        return oh1, oh2, jnp.sum(oh1, axis=1, keepdims=True), jnp.sum(oh2, axis=1, keepdims=True)

    oh1, oh2, c1, c2 = pair_counts(eid_ref[0:1, :], eid_ref[1:2, :])
    chunks = jnp.ceil((c1 + c2) * (1.0 / ROW_ALIGN))
    tab_ref[...] = jnp.zeros_like(tab_ref)

    @pl.when(i == 0)
    def _():
        a = lax.broadcasted_iota(I32, tri.shape, 0)
        b = lax.broadcasted_iota(I32, tri.shape, 1)
        tri[...] = (a <= b).astype(BF16)

        def tile_rows(k, tot):
            cols = pl.ds(pl.multiple_of(k * t, t), t)
            _, _, k1, k2 = pair_counts(eid_all_ref[0:1, cols], eid_all_ref[1:2, cols])
            return tot + jnp.ceil((k1 + k2) * (1.0 / ROW_ALIGN)) * ROW_ALIGN

        tot = lax.fori_loop(0, eid_all_ref.shape[1] // t, tile_rows, jnp.zeros((N_EXPERTS, 1), F32))
        cnt = jnp.broadcast_to(tot, (N_EXPERTS, LANES)).astype(I32)
        seg = ((cnt + (MOE_BLOCK - 1)) // MOE_BLOCK) * MOE_BLOCK
        r128 = lax.broadcasted_iota(I32, cnt.shape, 0)
        base = jnp.zeros_like(cnt)
        for e in range(N_EXPERTS - 1):
            base = base + jnp.where(r128 > e, seg[e:e + 1, :], 0)
        prior[...] = base.astype(F32)
        pad_start[...] = (base + cnt).astype(F32)
        pad_end = jnp.where(r128 == N_EXPERTS - 1, total_rows, base + seg)
        pad_chunks[...] = ((pad_end - base - cnt) // ROW_ALIGN).astype(F32)
        end_blk = (base + seg) // MOE_BLOCK
        lane = lax.broadcasted_iota(I32, (N_EXPERTS, BLOCK_LANES), 1)
        expert_of = jnp.sum((end_blk[:, 0:1] <= lane).astype(F32), axis=0, keepdims=True).astype(I32)
        active = (lane[0:1, :] < end_blk[N_EXPERTS - 1:N_EXPERTS, 0:1]).astype(I32)
        r8 = lax.broadcasted_iota(I32, (SUBLANES, BLOCK_LANES), 0)
        blk_ref[...] = jnp.where(r8 == 0, jnp.minimum(expert_of, N_EXPERTS - 1), jnp.where(r8 == 1, active, 0))

    ohs = jnp.concatenate([oh1, oh2], axis=0).astype(BF16)
    groups, run = [], jnp.zeros((2 * N_EXPERTS, 1), F32)
    for g in range(t // LANES):
        part = _dot(ohs[:, g * LANES:(g + 1) * LANES], tri[...])
        groups.append(part + run)
        run = run + part[:, LANES - 1:LANES]
    cs = jnp.concatenate(groups, axis=1)
    ea = lax.broadcasted_iota(I32, (N_EXPERTS, N_EXPERTS), 0)
    eb = lax.broadcasted_iota(I32, (N_EXPERTS, N_EXPERTS), 1)
    chunks_b = jnp.broadcast_to(chunks, (N_EXPERTS, LANES))
    start = _dot((eb < ea).astype(BF16), chunks_b.astype(BF16)) * ROW_ALIGN
    s0 = start[:, 0:1]
    pos1 = jnp.sum(oh1 * (s0 + cs[0:N_EXPERTS, :] - 1.0), axis=0, keepdims=True)
    pos2 = jnp.sum(oh2 * (s0 + c1 + cs[N_EXPERTS:2 * N_EXPERTS, :] - 1.0), axis=0, keepdims=True)
    pos1 = (pos1 * SUBLANES).astype(I32)
    pos2 = (pos2 * SUBLANES).astype(I32)
    for q in range(POS_ROWS):
        tab_ref[q:q + 1, :] = pos1[:, q * LANES:(q + 1) * LANES]
        tab_ref[POS_ROWS + q:POS_ROWS + q + 1, :] = pos2[:, q * LANES:(q + 1) * LANES]
    diag = lax.broadcasted_iota(I32, (N_EXPERTS, LANES), 0) == lax.broadcasted_iota(I32, (N_EXPERTS, LANES), 1)
    to_lanes = lambda col: jnp.sum(jnp.where(diag, col, 0.0), axis=0, keepdims=True).astype(I32)
    tab_ref[RUN_START_ROW:RUN_START_ROW + 1, :] = to_lanes(start)
    tab_ref[RUN_CHUNKS_ROW:RUN_CHUNKS_ROW + 1, :] = to_lanes(chunks_b)
    tab_ref[RUN_DEST_ROW:RUN_DEST_ROW + 1, :] = to_lanes(prior[...])
    tab_ref[PAD_START_ROW:PAD_START_ROW + 1, :] = to_lanes(pad_start[...])
    tab_ref[PAD_CHUNKS_ROW:PAD_CHUNKS_ROW + 1, :] = to_lanes(pad_chunks[...])
    prior[...] += chunks * ROW_ALIGN


def _rank(eid, total_rows):
    n = eid.shape[1]
    tiles = n // TOK_TILE
    return pl.pallas_call(
        functools.partial(_rank_kernel, total_rows=total_rows),
        grid=(tiles,),
        in_specs=[pl.BlockSpec((SUBLANES, TOK_TILE), lambda i: (0, i)), _const_spec((SUBLANES, n))],
        out_specs=[
            pl.BlockSpec((None, TABLE_ROWS, LANES), lambda i: (i, 0, 0)),
            pl.BlockSpec((SUBLANES, BLOCK_LANES), lambda i: (0, 0)),
        ],
        out_shape=[
            jax.ShapeDtypeStruct((tiles, TABLE_ROWS, LANES), I32),
            jax.ShapeDtypeStruct((SUBLANES, BLOCK_LANES), I32),
        ],
        scratch_shapes=[pltpu.VMEM((LANES, LANES), BF16)] + [pltpu.VMEM((N_EXPERTS, LANES), F32)] * 3,
        compiler_params=pltpu.CompilerParams(
            dimension_semantics=("arbitrary",), vmem_limit_bytes=VMEM_LIMIT),
        name="rank",
    )(eid, eid)


def _tile_at(ref, row):
    return ref.at[pl.ds(pl.multiple_of(row, SUBLANES), SUBLANES), :]


def _token_rows(ref, tok):
    return _tile_at(ref, tok * SUBLANES)


def _for_each_chunk(tab, fn, rows=(RUN_START_ROW, RUN_CHUNKS_ROW, RUN_DEST_ROW)):
    start_row, chunks_row, dest_row = rows
    per_big = BIG_CHUNK_ROWS // ROW_ALIGN
    total_big, total_small = 0, 0
    for e in range(N_EXPERTS):
        start, dest, chunks = tab[start_row, e], tab[dest_row, e], tab[chunks_row, e]
        nbig = chunks // per_big

        def piece(nrows, start=start, dest=dest):
            def body(q, carry):
                fn(pl.multiple_of(start + q * nrows, ROW_ALIGN), pl.multiple_of(dest + q * nrows, ROW_ALIGN), nrows)
                return carry
            return body

        lax.fori_loop(0, nbig, piece(BIG_CHUNK_ROWS), 0)
        lax.fori_loop(nbig * per_big, chunks, piece(ROW_ALIGN), 0)
        total_big, total_small = total_big + nbig, total_small + chunks - nbig * per_big
    return total_big, total_small


def _wait_pieces(count_ref, copy):
    for k, nrows in enumerate((BIG_CHUNK_ROWS, ROW_ALIGN)):
        def body(q, carry, nrows=nrows):
            copy(0, 0, nrows).wait()
            return carry
        lax.fori_loop(0, count_ref[0, k], body, 0)


def _dispatch_kernel(*refs, tile_base, aliased, fill_pads):
    if aliased:
        tab_hbm, h_ref, _, xs_hbm, tab, pending, tokbuf, loc, xl, zrows, sem_tab, sem_row = refs
    else:
        tab_hbm, h_ref, xs_hbm, tab, pending, tokbuf, loc, xl, zrows, sem_tab, sem_row = refs
    t = h_ref.shape[0]
    i = pl.program_id(0)
    cp = pltpu.make_async_copy(tab_hbm.at[tile_base + i], tab, sem_tab)
    cp.start()
    for c in range(SUBLANES):
        tokbuf[pl.ds(c, t, stride=SUBLANES), :] = h_ref[:, c * LANES:(c + 1) * LANES]

    @pl.when(i == 0)
    def _():
        loc[...] = jnp.zeros_like(loc)

    cp.wait()

    def place(q, carry):
        for lane in range(LANES):
            v = _token_rows(tokbuf, q * LANES + lane)[...]
            _tile_at(loc, tab[q, lane])[...] = v
            _tile_at(loc, tab[POS_ROWS + q, lane])[...] = v
        return carry

    lax.fori_loop(0, t // LANES, place, 0)

    def chunk_copy(local_row, global_row, nrows):
        return pltpu.make_async_copy(xl.at[pl.ds(local_row, nrows), :],
                                     xs_hbm.at[pl.ds(global_row, nrows), :], sem_row)

    @pl.when(i > 0)
    def _():
        _wait_pieces(pending, chunk_copy)

    for c in range(SUBLANES):
        xl[:, c * LANES:(c + 1) * LANES] = loc[pl.ds(c, LOCAL_ROWS, stride=SUBLANES), :].astype(BF16)
    pending[0, 0], pending[0, 1] = _for_each_chunk(tab, lambda *a: chunk_copy(*a).start())

    @pl.when(i == pl.num_programs(0) - 1)
    def _():
        _wait_pieces(pending, chunk_copy)

    if fill_pads:
        @pl.when(i == pl.num_programs(0) - 1)
        def _():
            zrows[...] = jnp.zeros_like(zrows)
            zero_copy = lambda _, g, nrows: pltpu.make_async_copy(
                zrows.at[pl.ds(0, nrows), :], xs_hbm.at[pl.ds(g, nrows), :], sem_row)
            pad_rows = (PAD_START_ROW, PAD_CHUNKS_ROW, PAD_START_ROW)
            pending[0, 0], pending[0, 1] = _for_each_chunk(tab, lambda *a: zero_copy(*a).start(), pad_rows)
            _wait_pieces(pending, zero_copy)


def _dispatch(tabs, h2d, xs, *, tile, tile_base, sorted_rows, fill_pads):
    n, d = h2d.shape
    aliased = xs is not None
    kern = functools.partial(_dispatch_kernel, tile_base=tile_base, aliased=aliased, fill_pads=fill_pads)
    in_specs = [pl.BlockSpec(memory_space=pl.ANY), pl.BlockSpec((tile, d), lambda i: (i, 0))]
    args = [tabs, h2d]
    if aliased:
        in_specs.append(pl.BlockSpec(memory_space=pl.ANY))
        args.append(xs)
    return pl.pallas_call(
        kern,
        grid=(n // tile,),
        in_specs=in_specs,
        out_specs=pl.BlockSpec(memory_space=pl.ANY),
        out_shape=jax.ShapeDtypeStruct((sorted_rows, d), BF16),
        scratch_shapes=[pltpu.SMEM((TABLE_ROWS, LANES), I32), pltpu.SMEM((SUBLANES, LANES), I32),
                        pltpu.VMEM((tile * SUBLANES, LANES), F32),
                        pltpu.VMEM((LOCAL_ROWS * SUBLANES, LANES), F32), pltpu.VMEM((LOCAL_ROWS, d), BF16),
                        pltpu.VMEM((BIG_CHUNK_ROWS, d), BF16), pltpu.SemaphoreType.DMA, pltpu.SemaphoreType.DMA],
        input_output_aliases={2: 0} if aliased else {},
        compiler_params=pltpu.CompilerParams(
            dimension_semantics=("arbitrary",), vmem_limit_bytes=VMEM_LIMIT),
        name="dispatch",
    )(*args)


def _combine_kernel(tab_hbm, gate_hbm, ys_hbm, h_ref, g_ref, b_ref, o_ref,
                    tabs, pending, gates, yl, ytok, ztok, sem_tab, sem_gate, sem_row, *maybe_zout, tile_base, alpha):
    t = h_ref.shape[0]
    i = pl.program_id(0)
    last = pl.num_programs(0) - 1
    slot = i % 2
    tab = tabs.at[slot]
    table_copy = lambda tile, s: pltpu.make_async_copy(tab_hbm.at[tile_base + tile], tabs.at[s], sem_tab.at[s])
    cp_gate = pltpu.make_async_copy(gate_hbm.at[:, pl.ds(pl.multiple_of(i * t, LANES), t)], gates, sem_gate)

    def chunk_copy(local_row, global_row, nrows):
        return pltpu.make_async_copy(ys_hbm.at[pl.ds(global_row, nrows), :],
                                     yl.at[pl.ds(local_row, nrows), :], sem_row)

    @pl.when(i == 0)
    def _():
        yl[...] = jnp.zeros_like(yl)
        table_copy(0, 0).start()
        table_copy(0, 0).wait()
        pending[0, 0], pending[0, 1] = _for_each_chunk(tabs.at[0], lambda *a: chunk_copy(*a).start())

    @pl.when(i < last)
    def _():
        table_copy(i + 1, 1 - slot).start()

    cp_gate.start()
    _wait_pieces(pending, chunk_copy)
    for c in range(SUBLANES):
        ytok[pl.ds(c, LOCAL_ROWS, stride=SUBLANES), :] = yl[:, c * LANES:(c + 1) * LANES].astype(F32)

    @pl.when(i < last)
    def _():
        table_copy(i + 1, 1 - slot).wait()
        pending[0, 0], pending[0, 1] = _for_each_chunk(tabs.at[1 - slot], lambda *a: chunk_copy(*a).start())

    cp_gate.wait()

    def gather(q, carry):
        for lane in range(LANES):
            n = q * LANES + lane
            _token_rows(ztok, n)[...] = (gates[0, n] * _tile_at(ytok, tab[q, lane])[...]
                                         + gates[1, n] * _tile_at(ytok, tab[POS_ROWS + q, lane])[...])
        return carry

    lax.fori_loop(0, t // LANES, gather, 0)
    f = jnp.concatenate([ztok[pl.ds(c, t, stride=SUBLANES), :] for c in range(SUBLANES)], axis=1)
    out = _layer_norm(alpha * h_ref[...] + f, g_ref[...], b_ref[...])
    if maybe_zout:
        _by_sequence(out, o_ref, *maybe_zout)
    else:
        o_ref[...] = out


def _combine(tabs, gate, ys, h2d, p, *, tile, tile_base, alpha, out_sequences=None):
    n, d = h2d.shape
    kern = functools.partial(_combine_kernel, tile_base=tile_base, alpha=alpha)
    scratch = [pltpu.SMEM((2, TABLE_ROWS, LANES), I32), pltpu.SMEM((SUBLANES, LANES), I32),
               pltpu.SMEM((SUBLANES, tile), F32),
               pltpu.VMEM((LOCAL_ROWS, d), BF16), pltpu.VMEM((LOCAL_ROWS * SUBLANES, LANES), F32),
               pltpu.VMEM((tile * SUBLANES, LANES), F32),
               pltpu.SemaphoreType.DMA((2,)), pltpu.SemaphoreType.DMA, pltpu.SemaphoreType.DMA]
    if out_sequences:
        npos = tile // out_sequences
        out_spec = pl.BlockSpec((out_sequences, npos, d), lambda i: (0, i, 0))
        out_shape = jax.ShapeDtypeStruct((out_sequences, n // out_sequences, d), F32)
        scratch.append(pltpu.VMEM((d // LANES, tile, LANES), F32))
    else:
        out_spec = pl.BlockSpec((tile, d), lambda i: (i, 0))
        out_shape = jax.ShapeDtypeStruct((n, d), F32)
    return pl.pallas_call(
        kern,
        grid=(n // tile,),
        in_specs=[
            pl.BlockSpec(memory_space=pl.ANY), pl.BlockSpec(memory_space=pl.ANY), pl.BlockSpec(memory_space=pl.ANY),
            pl.BlockSpec((tile, d), lambda i: (i, 0)),
            _const_spec((1, d)), _const_spec((1, d)),
        ],
        out_specs=out_spec,
        out_shape=out_shape,
        scratch_shapes=scratch,
        compiler_params=pltpu.CompilerParams(
            dimension_semantics=("arbitrary",), vmem_limit_bytes=VMEM_LIMIT),
        name="combine",
    )(tabs, gate, ys, h2d, p["ln2_g"], p["ln2_b"])


def _moe_kernel(blk_ref, x_ref, wg_ref, wu_ref, wd_ref, y_ref, wgb, wub, wdb):
    b = pl.program_id(0)

    @pl.when(jnp.logical_or(b == 0, blk_ref[0, b] != blk_ref[0, jnp.maximum(b - 1, 0)]))
    def _():
        wgb[...] = wg_ref[...].astype(BF16)
        wub[...] = wu_ref[...].astype(BF16)
        wdb[...] = wd_ref[...].astype(BF16)

    @pl.when(blk_ref[1, b] == 1)
    def _():
        x = x_ref[...]
        hg = _dot(x, wgb[...])
        hu = _dot(x, wub[...])
        hid = (hg * _sigmoid(hg) * hu).astype(BF16)
        y_ref[...] = _dot(hid, wdb[...]).astype(BF16)

    @pl.when(blk_ref[1, b] == 0)
    def _():
        y_ref[...] = jnp.zeros_like(y_ref)


def _moe(blk, xs, wg, wu, wd, layer):
    rows, d = xs.shape
    nblocks = rows // MOE_BLOCK
    f = wg.shape[-1]
    block = lambda b, blk: (jnp.where(blk[1, b] == 1, b, nblocks - 1), 0)
    return pl.pallas_call(
        _moe_kernel,
        grid_spec=pltpu.PrefetchScalarGridSpec(
            num_scalar_prefetch=1,
            grid=(nblocks,),
            in_specs=[
                pl.BlockSpec((MOE_BLOCK, d), block),
                pl.BlockSpec((None, None, d, f), lambda b, blk: (layer, blk[0, b], 0, 0)),
                pl.BlockSpec((None, None, d, f), lambda b, blk: (layer, blk[0, b], 0, 0)),
                pl.BlockSpec((None, None, f, d), lambda b, blk: (layer, blk[0, b], 0, 0)),
            ],
            out_specs=pl.BlockSpec((MOE_BLOCK, d), lambda b, blk: (b, 0)),
            scratch_shapes=[pltpu.VMEM((d, f), BF16), pltpu.VMEM((d, f), BF16), pltpu.VMEM((f, d), BF16)],
        ),
        out_shape=jax.ShapeDtypeStruct((rows, d), BF16),
        compiler_params=pltpu.CompilerParams(
            dimension_semantics=("arbitrary",), vmem_limit_bytes=VMEM_LIMIT),
        name="moe",
    )(blk, xs, wg, wu, wd)


def _sparse_moe(sets, p, *, layer, alpha, out_sequences=None):
    eids, bases, tiles = [], [], 0
    for h, e, _ in sets:
        n = h.shape[0]
        n_pad = -(-n // TOK_TILE) * TOK_TILE
        eids.append(jnp.concatenate([e, jnp.full((SUBLANES, n_pad - n), -1, I32)], axis=1))
        bases.append(tiles)
        tiles += n_pad // TOK_TILE
    n_tok = sum(h.shape[0] for h, _, _ in sets)
    run_pad = tiles * N_EXPERTS * (ROW_ALIGN - 1)
    nblocks = -(-(2 * n_tok + run_pad + N_EXPERTS * (MOE_BLOCK - 1)) // MOE_BLOCK)
    assert nblocks <= BLOCK_LANES
    sorted_rows = nblocks * MOE_BLOCK
    tabs, blk = _rank(jnp.concatenate(eids, axis=1), sorted_rows)
    xs = None
    for k, ((h, _, _), tb) in enumerate(zip(sets, bases)):
        xs = _dispatch(tabs, h, xs, tile=min(TOK_TILE, h.shape[0]), tile_base=tb, sorted_rows=sorted_rows,
                       fill_pads=k == len(sets) - 1)
    ys = _moe(blk, xs, p["w_gate"], p["w_up"], p["w_down"], layer)
    return [_combine(tabs, g, ys, h, p, tile=min(TOK_TILE, h.shape[0]), tile_base=tb, alpha=alpha,
                     out_sequences=out_sequences if k == 0 else None)
            for k, ((h, _, g), tb) in enumerate(zip(sets, bases))]


def _pad_rows(w, rows):
    return jnp.concatenate([w, jnp.zeros((rows - w.shape[0],) + w.shape[1:], w.dtype)], axis=0)


def _router_params(gw, gb, ew, eb):
    d = gw.shape[0]
    rw = jnp.concatenate([gw.T, jnp.zeros((8 - N_GROUPS, d), gw.dtype), ew.T], axis=0).astype(BF16)
    rb = jnp.concatenate([gb, jnp.zeros((8 - N_GROUPS,), gb.dtype), eb])[:, None].astype(F32)
    return rw, rb


def _to_blocks(x, seq_tile):
    nseq, npos, d = x.shape
    return x.reshape(nseq // seq_tile, seq_tile, npos, d).swapaxes(1, 2).reshape(nseq * npos, d)


def _from_blocks(x2d, nseq, seq_tile):
    n, d = x2d.shape
    npos = n // nseq
    return x2d.reshape(nseq // seq_tile, npos, seq_tile, d).swapaxes(1, 2).reshape(nseq, npos, d)


def kernel(x_prompt, x_sample, state_conv_a, state_conv_b, meta_tokens, sc_w_in, sc_conv_w, sc_w_out, cf_w_pw1, cf_b_pw1, cf_conv_w, cf_conv_b, cf_ln_g, cf_ln_b, cf_w_pw2, cf_b_pw2, ln1_g, ln1_b, ln2_g, ln2_b, rt_group_w, rt_group_b, rt_expert_w, rt_expert_b, moe_w_gate, moe_w_up, moe_w_down):
    bsz, seq, d = x_prompt.shape
    dec_b, dec_t, _ = x_sample.shape
    n_meta = meta_tokens.shape[0]
    depth = ln1_g.shape[0]
    alpha = float((2 * depth) ** 0.25)
    row = lambda v: v[None, :].astype(F32)
    s_seq = 32
    p_pos = 512 // bsz

    layer = []
    for i in range(depth):
        rw, rb = _router_params(rt_group_w[i], rt_group_b[i], rt_expert_w[i], rt_expert_b[i])
        layer.append(dict(
            ln1_g=row(ln1_g[i]), ln1_b=row(ln1_b[i]), ln2_g=row(ln2_g[i]), ln2_b=row(ln2_b[i]), rw=rw, rb=rb,
            w_gate=moe_w_gate, w_up=moe_w_up, w_down=moe_w_down))
    pa = dict(layer[0], w_in=sc_w_in[0].astype(BF16), conv_w=_pad_rows(sc_conv_w[0], 8),
              w_out=sc_w_out[0].astype(BF16))
    pb = dict(layer[1], w_pw1=cf_w_pw1[0].astype(BF16), b_pw1=row(cf_b_pw1[0]),
              conv_w=jnp.repeat(cf_conv_w[0].astype(F32), SUBLANES, axis=0), conv_b=row(cf_conv_b[0]),
              cln_g=row(cf_ln_g[0]), cln_b=row(cf_ln_b[0]), w_pw2=cf_w_pw2[0].astype(BF16), b_pw2=row(cf_b_pw2[0]))

    xs = _to_blocks(x_sample, s_seq)
    xm = jnp.broadcast_to(meta_tokens.astype(F32)[:, None, :], (n_meta, bsz, d)).reshape(n_meta * bsz, d)
    mix_p = dict(pos_tile=p_pos, seq_tile=bsz, nseq=bsz, alpha=alpha)
    mix_s = dict(pos_tile=dec_t, seq_tile=s_seq, nseq=dec_b, alpha=alpha)
    mix_m = dict(pos_tile=n_meta, seq_tile=bsz, nseq=bsz, alpha=alpha)

    zero_a = jnp.zeros(((SC_WIDTH - 1) * bsz, d), F32)
    hm, em, gm, tail_am = _mixer("a", xm, zero_a, pa, **mix_m)
    hp, ep, gp, tail_ap = _mixer("a", x_prompt, tail_am, pa, by_sequence=True, **mix_p)
    hs, es, gs, tail_as = _mixer("a", xs, _to_blocks(state_conv_a[0], s_seq), pa, **mix_s)
    hp, hs, hm = _sparse_moe([(hp, ep, gp), (hs, es, gs), (hm, em, gm)], pa, layer=0, alpha=alpha)

    zero_b = jnp.zeros(((CF_WIDTH - 1) * bsz, d), F32)
    _, _, _, tail_bm = _mixer("b", hm, zero_b, pb, **mix_m)
    hp, ep, gp, tail_bp = _mixer("b", hp, tail_bm, pb, **mix_p)
    hs, es, gs, tail_bs = _mixer("b", hs, _to_blocks(state_conv_b[0], s_seq), pb, **mix_s)
    y_prompt, hs = _sparse_moe([(hp, ep, gp), (hs, es, gs)], pb, layer=1, alpha=alpha, out_sequences=bsz)

    return (y_prompt, _from_blocks(hs, dec_b, s_seq),
            _from_blocks(tail_ap, bsz, bsz)[None], _from_blocks(tail_bp, bsz, bsz)[None],
            _from_blocks(tail_as, dec_b, s_seq)[None], _from_blocks(tail_bs, dec_b, s_seq)[None])
```

```python
import functools

import jax
import jax.numpy as jnp
from jax import lax
from jax.experimental import pallas as pl
from jax.experimental.pallas import tpu as pltpu

F32 = jnp.float32
BF16 = jnp.bfloat16
I32 = jnp.int32

LN_EPS = 1e-5
N_GROUPS = 4
EXPERTS_PER_GROUP = 8
N_EXPERTS = N_GROUPS * EXPERTS_PER_GROUP
SC_WIDTH = 3
CF_WIDTH = 31
ROUTER_ROWS = 8 + N_EXPERTS
SUBLANES = 8
LANES = 128
ROW_ALIGN = 16
BIG_CHUNK_ROWS = 64
DMA_THREADS = 2
CONV_ROWS = 16
VMEM_LIMIT = 56 * 1024 * 1024
MOE_BLOCK = 512
TOK_TILE = 1024
LOCAL_ROWS = 2 * TOK_TILE + N_EXPERTS * ROW_ALIGN
POS_ROWS = TOK_TILE // LANES
RUN_START_ROW, RUN_CHUNKS_ROW, RUN_DEST_ROW = 2 * POS_ROWS, 2 * POS_ROWS + 1, 2 * POS_ROWS + 2
PAD_START_ROW, PAD_CHUNKS_ROW = 2 * POS_ROWS + 3, 2 * POS_ROWS + 4
TABLE_ROWS = 2 * POS_ROWS + SUBLANES
BLOCK_LANES = 256


def _dot(a, b):
    return jnp.dot(a, b, preferred_element_type=F32)


def _layer_norm(z, g, b):
    mu = jnp.mean(z, axis=-1, keepdims=True)
    zc = z - mu
    var = jnp.mean(zc * zc, axis=-1, keepdims=True)
    return zc * lax.rsqrt(var + LN_EPS) * g + b


def _sigmoid(x):
    return 1.0 / (1.0 + jnp.exp(-x))


def _route(h1b, rw_ref, rb_ref):
    t = h1b.shape[0]
    logits = lax.dot_general(rw_ref[...], h1b, (((1,), (1,)), ((), ())),
                             preferred_element_type=F32) + rb_ref[...]
    g = [logits[i:i + 1, :] for i in range(N_GROUPS)]
    gmax = jnp.maximum(jnp.maximum(g[0], g[1]), jnp.maximum(g[2], g[3]))
    gidx = jnp.where(g[0] == gmax, 0, jnp.where(g[1] == gmax, 1, jnp.where(g[2] == gmax, 2, 3)))
    gsum = (jnp.exp(g[0] - gmax) + jnp.exp(g[1] - gmax)) + (jnp.exp(g[2] - gmax) + jnp.exp(g[3] - gmax))
    gp = 1.0 / gsum
    sel = logits[8 + 8 * (N_GROUPS - 1):8 + 8 * N_GROUPS, :]
    for gi in range(N_GROUPS - 2, -1, -1):
        sel = jnp.where(gidx == gi, logits[8 + 8 * gi:16 + 8 * gi, :], sel)
    row = lax.broadcasted_iota(I32, (EXPERTS_PER_GROUP, t), 0)
    m1 = jnp.max(sel, axis=0, keepdims=True)
    i1 = jnp.min(jnp.where(sel == m1, row, EXPERTS_PER_GROUP), axis=0, keepdims=True)
    sel2 = jnp.where(row == i1, -jnp.inf, sel)
    m2 = jnp.max(sel2, axis=0, keepdims=True)
    i2 = jnp.min(jnp.where(sel2 == m2, row, EXPERTS_PER_GROUP), axis=0, keepdims=True)
    d = jnp.exp(m2 - m1)
    w1 = 1.0 / (1.0 + d)
    w2 = d / (1.0 + d)
    return gidx * EXPERTS_PER_GROUP + i1, gidx * EXPERTS_PER_GROUP + i2, gp * w1, gp * w2


def _post_mixer(x, y, alpha, g_ref, b_ref, rw_ref, rb_ref, h1_ref, eid_ref, gate_ref):
    h1 = _layer_norm(alpha * x + y, g_ref[...], b_ref[...])
    h1_ref[...] = h1
    e1, e2, g1, g2 = _route(h1.astype(BF16), rw_ref, rb_ref)
    row8 = lax.broadcasted_iota(I32, (SUBLANES, h1.shape[0]), 0)
    eid_ref[...] = jnp.where(row8 == 0, e1, jnp.where(row8 == 1, e2, -1))
    gate_ref[...] = jnp.where(row8 == 0, g1, jnp.where(row8 == 1, g2, 0.0))


def _load_history(ubuf, hist_ref, rows, hist_rows):
    j = pl.program_id(1)

    @pl.when(j == 0)
    def _():
        ubuf[0:hist_rows, :] = hist_ref[...]

    @pl.when(j > 0)
    def _():
        ubuf[0:hist_rows, :] = ubuf[rows:rows + hist_rows, :]


def _position_major(x_ref, xin):
    nseq, npos, d = x_ref.shape
    for c in range(d // LANES):
        for s in range(nseq):
            xin[c, pl.ds(s, npos, stride=nseq), :] = x_ref[s, :, c * LANES:(c + 1) * LANES]
    return jnp.concatenate([xin[c] for c in range(d // LANES)], axis=1)


def _by_sequence(rows, o_ref, zout):
    nseq, npos, d = o_ref.shape
    for c in range(d // LANES):
        zout[c] = rows[:, c * LANES:(c + 1) * LANES]
    for c in range(d // LANES):
        for s in range(nseq):
            o_ref[s, :, c * LANES:(c + 1) * LANES] = zout[c, pl.ds(s, npos, stride=nseq), :]


def _mixer_a_kernel(x_ref, hist_ref, win_ref, cw_ref, wout_ref, g_ref, b_ref, rw_ref, rb_ref,
                    h1_ref, eid_ref, gate_ref, tail_ref, ubuf, *maybe_xin, alpha, nseq):
    rows, d = h1_ref.shape
    hist_rows = (SC_WIDTH - 1) * nseq
    _load_history(ubuf, hist_ref, rows, hist_rows)
    x = _position_major(x_ref, *maybe_xin) if maybe_xin else x_ref[...]
    xb = x.astype(BF16)
    ubuf[hist_rows:hist_rows + rows, :] = _dot(xb, win_ref[:, d:2 * d]) * _dot(xb, win_ref[:, 2 * d:3 * d])
    tail_ref[...] = ubuf[rows:rows + hist_rows, :]
    conv = cw_ref[0:1, :] * ubuf[0:rows, :]
    for k in range(1, SC_WIDTH):
        conv = conv + cw_ref[k:k + 1, :] * ubuf[k * nseq:k * nseq + rows, :]
    bg = _dot(xb, win_ref[:, 0:d])
    y = _dot((bg * conv).astype(BF16), wout_ref[...])
    _post_mixer(x, y, alpha, g_ref, b_ref, rw_ref, rb_ref, h1_ref, eid_ref, gate_ref)


def _mixer_b_kernel(x_ref, hist_ref, w1_ref, b1_ref, cw_ref, cb_ref, lg_ref, lb_ref, w2_ref, b2_ref,
                    g_ref, b_ref, rw_ref, rb_ref,
                    h1_ref, eid_ref, gate_ref, tail_ref, ubuf, cbuf, *, alpha, nseq):
    rows, d = x_ref.shape
    hist_rows = (CF_WIDTH - 1) * nseq
    _load_history(ubuf, hist_ref, rows, hist_rows)
    x = x_ref[...]
    xb = x.astype(BF16)
    a = _dot(xb, w1_ref[:, 0:d]) + b1_ref[:, 0:d]
    gl = _dot(xb, w1_ref[:, d:2 * d]) + b1_ref[:, d:2 * d]
    ubuf[hist_rows:hist_rows + rows, :] = a * _sigmoid(gl)
    tail_ref[...] = ubuf[rows:rows + hist_rows, :]
    halves = CONV_ROWS // SUBLANES

    def chunk(c, carry):
        r0 = pl.multiple_of(c * CONV_ROWS, CONV_ROWS)
        acc = [jnp.broadcast_to(cb_ref[...], (SUBLANES, d)) for _ in range(halves)]
        for k in range(CF_WIDTH):
            w = cw_ref[k * SUBLANES:(k + 1) * SUBLANES, :]
            for h in range(halves):
                start = pl.multiple_of(r0 + h * SUBLANES + k * nseq, SUBLANES)
                acc[h] = acc[h] + w * ubuf[pl.ds(start, SUBLANES), :]
        for h in range(halves):
            cbuf[pl.ds(pl.multiple_of(r0 + h * SUBLANES, SUBLANES), SUBLANES), :] = acc[h]
        return carry

    lax.fori_loop(0, rows // CONV_ROWS, chunk, 0)
    cn = _layer_norm(cbuf[...], lg_ref[...], lb_ref[...])
    cn = cn * _sigmoid(cn)
    y = _dot(cn.astype(BF16), w2_ref[...]) + b2_ref[...]
    _post_mixer(x, y, alpha, g_ref, b_ref, rw_ref, rb_ref, h1_ref, eid_ref, gate_ref)


def _const_spec(shape):
    nd = len(shape)
    return pl.BlockSpec(shape, lambda *_: (0,) * nd, pipeline_mode=pl.Buffered(1))


def _mixer(kind, x2d, hist2d, p, *, pos_tile, seq_tile, nseq, alpha, by_sequence=False):
    d = x2d.shape[-1]
    n = x2d.size // d
    width = SC_WIDTH if kind == "a" else CF_WIDTH
    rows = pos_tile * seq_tile
    hist_rows = (width - 1) * seq_tile
    seq_blocks = nseq // seq_tile
    steps = n // (rows * seq_blocks)
    if kind == "a":
        kern = functools.partial(_mixer_a_kernel, alpha=alpha, nseq=seq_tile)
        weights = (p["w_in"], p["conv_w"], p["w_out"])
        wspecs = [_const_spec((d, 3 * d)), _const_spec((8, d)), _const_spec((d, d))]
        scratch = [pltpu.VMEM((rows + hist_rows, d), F32)]
        if by_sequence:
            scratch.append(pltpu.VMEM((d // LANES, rows, LANES), F32))
    else:
        assert not by_sequence
        kern = functools.partial(_mixer_b_kernel, alpha=alpha, nseq=seq_tile)
        weights = (p["w_pw1"], p["b_pw1"], p["conv_w"], p["conv_b"], p["cln_g"], p["cln_b"], p["w_pw2"], p["b_pw2"])
        wspecs = [_const_spec((d, 2 * d)), _const_spec((1, 2 * d)), _const_spec((CF_WIDTH * SUBLANES, d)),
                  _const_spec((1, d)), _const_spec((1, d)), _const_spec((1, d)), _const_spec((d, d)),
                  _const_spec((1, d))]
        scratch = [pltpu.VMEM((rows + hist_rows, d), F32), pltpu.VMEM((rows, d), F32)]
    common = (p["ln1_g"], p["ln1_b"], p["rw"], p["rb"])
    cspecs = [_const_spec((1, d)), _const_spec((1, d)), _const_spec((ROUTER_ROWS, d)), _const_spec((ROUTER_ROWS, 1))]
    if by_sequence:
        assert seq_blocks == 1
        x_spec = pl.BlockSpec((nseq, pos_tile, d), lambda s, j: (0, j, 0))
    else:
        x_spec = pl.BlockSpec((rows, d), lambda s, j: (s * steps + j, 0))
    return pl.pallas_call(
        kern,
        grid=(seq_blocks, steps),
        in_specs=[x_spec, pl.BlockSpec((hist_rows, d), lambda s, j: (s, 0))] + wspecs + cspecs,
        out_specs=[
            pl.BlockSpec((rows, d), lambda s, j: (s * steps + j, 0)),
            pl.BlockSpec((SUBLANES, rows), lambda s, j: (0, s * steps + j)),
            pl.BlockSpec((SUBLANES, rows), lambda s, j: (0, s * steps + j)),
            pl.BlockSpec((hist_rows, d), lambda s, j: (s, 0)),
        ],
        out_shape=[
            jax.ShapeDtypeStruct((n, d), F32),
            jax.ShapeDtypeStruct((SUBLANES, n), I32),
            jax.ShapeDtypeStruct((SUBLANES, n), F32),
            jax.ShapeDtypeStruct((seq_blocks * hist_rows, d), F32),
        ],
        scratch_shapes=scratch,
        compiler_params=pltpu.CompilerParams(
            dimension_semantics=("arbitrary", "arbitrary"), vmem_limit_bytes=VMEM_LIMIT),
        name="mixer_" + kind,
    )(x2d, hist2d, *weights, *common)


def _rank_kernel(eid_ref, eid_all_ref, tab_ref, blk_ref, tri, prior, pad_start, pad_chunks, *, total_rows):
    i = pl.program_id(0)
    t = eid_ref.shape[1]
    row = lax.broadcasted_iota(I32, (N_EXPERTS, t), 0)

    def pair_counts(e1, e2):
        oh1 = (row == e1).astype(F32)
        oh2 = (row == e2).astype(F32)
        return oh1, oh2, jnp.sum(oh1, axis=1, keepdims=True), jnp.sum(oh2, axis=1, keepdims=True)

    oh1, oh2, c1, c2 = pair_counts(eid_ref[0:1, :], eid_ref[1:2, :])
    chunks = jnp.ceil((c1 + c2) * (1.0 / ROW_ALIGN))
    tab_ref[...] = jnp.zeros_like(tab_ref)

    @pl.when(i == 0)
    def _():
        a = lax.broadcasted_iota(I32, tri.shape, 0)
        b = lax.broadcasted_iota(I32, tri.shape, 1)
        tri[...] = (a <= b).astype(BF16)

        def tile_rows(k, tot):
            cols = pl.ds(pl.multiple_of(k * t, t), t)
            _, _, k1, k2 = pair_counts(eid_all_ref[0:1, cols], eid_all_ref[1:2, cols])
            return tot + jnp.ceil((k1 + k2) * (1.0 / ROW_ALIGN)) * ROW_ALIGN

        tot = lax.fori_loop(0, eid_all_ref.shape[1] // t, tile_rows, jnp.zeros((N_EXPERTS, 1), F32))
        cnt = jnp.broadcast_to(tot, (N_EXPERTS, LANES)).astype(I32)
        seg = ((cnt + (MOE_BLOCK - 1)) // MOE_BLOCK) * MOE_BLOCK
        r128 = lax.broadcasted_iota(I32, cnt.shape, 0)
        base = jnp.zeros_like(cnt)
        for e in range(N_EXPERTS - 1):
            base = base + jnp.where(r128 > e, seg[e:e + 1, :], 0)
        prior[...] = base.astype(F32)
        pad_start[...] = (base + cnt).astype(F32)
        pad_end = jnp.where(r128 == N_EXPERTS - 1, total_rows, base + seg)
        pad_chunks[...] = ((pad_end - base - cnt) // ROW_ALIGN).astype(F32)
        end_blk = (base + seg) // MOE_BLOCK
        lane = lax.broadcasted_iota(I32, (N_EXPERTS, BLOCK_LANES), 1)
        expert_of = jnp.sum((end_blk[:, 0:1] <= lane).astype(F32), axis=0, keepdims=True).astype(I32)
        active = (lane[0:1, :] < end_blk[N_EXPERTS - 1:N_EXPERTS, 0:1]).astype(I32)
        r8 = lax.broadcasted_iota(I32, (SUBLANES, BLOCK_LANES), 0)
        blk_ref[...] = jnp.where(r8 == 0, jnp.minimum(expert_of, N_EXPERTS - 1), jnp.where(r8 == 1, active, 0))

    ohs = jnp.concatenate([oh1, oh2], axis=0).astype(BF16)
    groups, run = [], jnp.zeros((2 * N_EXPERTS, 1), F32)
    for g in range(t // LANES):
        part = _dot(ohs[:, g * LANES:(g + 1) * LANES], tri[...])
        groups.append(part + run)
        run = run + part[:, LANES - 1:LANES]
    cs = jnp.concatenate(groups, axis=1)
    ea = lax.broadcasted_iota(I32, (N_EXPERTS, N_EXPERTS), 0)
    eb = lax.broadcasted_iota(I32, (N_EXPERTS, N_EXPERTS), 1)
    chunks_b = jnp.broadcast_to(chunks, (N_EXPERTS, LANES))
    start = _dot((eb < ea).astype(BF16), chunks_b.astype(BF16)) * ROW_ALIGN
    s0 = start[:, 0:1]
    pos1 = jnp.sum(oh1 * (s0 + cs[0:N_EXPERTS, :] - 1.0), axis=0, keepdims=True)
    pos2 = jnp.sum(oh2 * (s0 + c1 + cs[N_EXPERTS:2 * N_EXPERTS, :] - 1.0), axis=0, keepdims=True)
    pos1 = (pos1 * SUBLANES).astype(I32)
    pos2 = (pos2 * SUBLANES).astype(I32)
    for q in range(POS_ROWS):
        tab_ref[q:q + 1, :] = pos1[:, q * LANES:(q + 1) * LANES]
        tab_ref[POS_ROWS + q:POS_ROWS + q + 1, :] = pos2[:, q * LANES:(q + 1) * LANES]
    diag = lax.broadcasted_iota(I32, (N_EXPERTS, LANES), 0) == lax.broadcasted_iota(I32, (N_EXPERTS, LANES), 1)
    to_lanes = lambda col: jnp.sum(jnp.where(diag, col, 0.0), axis=0, keepdims=True).astype(I32)
    tab_ref[RUN_START_ROW:RUN_START_ROW + 1, :] = to_lanes(start)
    tab_ref[RUN_CHUNKS_ROW:RUN_CHUNKS_ROW + 1, :] = to_lanes(chunks_b)
    tab_ref[RUN_DEST_ROW:RUN_DEST_ROW + 1, :] = to_lanes(prior[...])
    tab_ref[PAD_START_ROW:PAD_START_ROW + 1, :] = to_lanes(pad_start[...])
    tab_ref[PAD_CHUNKS_ROW:PAD_CHUNKS_ROW + 1, :] = to_lanes(pad_chunks[...])
    prior[...] += chunks * ROW_ALIGN


def _rank(eid, total_rows):
    n = eid.shape[1]
    tiles = n // TOK_TILE
    return pl.pallas_call(
        functools.partial(_rank_kernel, total_rows=total_rows),
        grid=(tiles,),
        in_specs=[pl.BlockSpec((SUBLANES, TOK_TILE), lambda i: (0, i)), _const_spec((SUBLANES, n))],
        out_specs=[
            pl.BlockSpec((None, TABLE_ROWS, LANES), lambda i: (i, 0, 0)),
            pl.BlockSpec((SUBLANES, BLOCK_LANES), lambda i: (0, 0)),
        ],
        out_shape=[
            jax.ShapeDtypeStruct((tiles, TABLE_ROWS, LANES), I32),
            jax.ShapeDtypeStruct((SUBLANES, BLOCK_LANES), I32),
        ],
        scratch_shapes=[pltpu.VMEM((LANES, LANES), BF16)] + [pltpu.VMEM((N_EXPERTS, LANES), F32)] * 3,
        compiler_params=pltpu.CompilerParams(
            dimension_semantics=("arbitrary",), vmem_limit_bytes=VMEM_LIMIT),
        name="rank",
    )(eid, eid)


def _tile_at(ref, row):
    return ref.at[pl.ds(pl.multiple_of(row, SUBLANES), SUBLANES), :]


def _token_rows(ref, tok):
    return _tile_at(ref, tok * SUBLANES)


def _for_each_chunk(tab, fn, rows=(RUN_START_ROW, RUN_CHUNKS_ROW, RUN_DEST_ROW)):
    start_row, chunks_row, dest_row = rows
    per_big = BIG_CHUNK_ROWS // ROW_ALIGN
    total_big, total_small = 0, 0
    for e in range(N_EXPERTS):
        start, dest, chunks = tab[start_row, e], tab[dest_row, e], tab[chunks_row, e]
        nbig = chunks // per_big

        def piece(nrows, start=start, dest=dest, thread=e % DMA_THREADS):
            def body(q, carry):
                fn(pl.multiple_of(start + q * nrows, ROW_ALIGN), pl.multiple_of(dest + q * nrows, ROW_ALIGN), nrows,
                   thread)
                return carry
            return body

        lax.fori_loop(0, nbig, piece(BIG_CHUNK_ROWS), 0)
        lax.fori_loop(nbig * per_big, chunks, piece(ROW_ALIGN), 0)
        total_big, total_small = total_big + nbig, total_small + chunks - nbig * per_big
    return total_big, total_small


def _wait_pieces(count_ref, copy):
    for k, nrows in enumerate((BIG_CHUNK_ROWS, ROW_ALIGN)):
        def body(q, carry, nrows=nrows):
            copy(0, 0, nrows).wait()
            return carry
        lax.fori_loop(0, count_ref[0, k], body, 0)


def _dispatch_kernel(*refs, tile_base, aliased, fill_pads):
    if aliased:
        tab_hbm, h_ref, _, xs_hbm, tab, pending, tokbuf, loc, xl, zrows, sem_tab, sem_row = refs
    else:
        tab_hbm, h_ref, xs_hbm, tab, pending, tokbuf, loc, xl, zrows, sem_tab, sem_row = refs
    t = h_ref.shape[0]
    i = pl.program_id(0)
    cp = pltpu.make_async_copy(tab_hbm.at[tile_base + i], tab, sem_tab)
    cp.start()
    for c in range(SUBLANES):
        tokbuf[pl.ds(c, t, stride=SUBLANES), :] = h_ref[:, c * LANES:(c + 1) * LANES]

    @pl.when(i == 0)
    def _():
        loc[...] = jnp.zeros_like(loc)

    cp.wait()

    def place(q, carry):
        for lane in range(LANES):
            v = _token_rows(tokbuf, q * LANES + lane)[...]
            _tile_at(loc, tab[q, lane])[...] = v
            _tile_at(loc, tab[POS_ROWS + q, lane])[...] = v
        return carry

    lax.fori_loop(0, t // LANES, place, 0)

    def chunk_copy(local_row, global_row, nrows):
        return pltpu.make_async_copy(xl.at[pl.ds(local_row, nrows), :],
                                     xs_hbm.at[pl.ds(global_row, nrows), :], sem_row)

    @pl.when(i > 0)
    def _():
        _wait_pieces(pending, chunk_copy)

    for c in range(SUBLANES):
        xl[:, c * LANES:(c + 1) * LANES] = loc[pl.ds(c, LOCAL_ROWS, stride=SUBLANES), :].astype(BF16)
    pending[0, 0], pending[0, 1] = _for_each_chunk(tab, lambda *a: chunk_copy(*a[:3]).start(priority=a[3]))

    @pl.when(i == pl.num_programs(0) - 1)
    def _():
        _wait_pieces(pending, chunk_copy)

    if fill_pads:
        @pl.when(i == pl.num_programs(0) - 1)
        def _():
            zrows[...] = jnp.zeros_like(zrows)
            zero_copy = lambda _, g, nrows: pltpu.make_async_copy(
                zrows.at[pl.ds(0, nrows), :], xs_hbm.at[pl.ds(g, nrows), :], sem_row)
            pad_rows = (PAD_START_ROW, PAD_CHUNKS_ROW, PAD_START_ROW)
            pending[0, 0], pending[0, 1] = _for_each_chunk(tab, lambda *a: zero_copy(*a[:3]).start(priority=a[3]), pad_rows)
            _wait_pieces(pending, zero_copy)


def _dispatch(tabs, h2d, xs, *, tile, tile_base, sorted_rows, fill_pads):
    n, d = h2d.shape
    aliased = xs is not None
    kern = functools.partial(_dispatch_kernel, tile_base=tile_base, aliased=aliased, fill_pads=fill_pads)
    in_specs = [pl.BlockSpec(memory_space=pl.ANY), pl.BlockSpec((tile, d), lambda i: (i, 0))]
    args = [tabs, h2d]
    if aliased:
        in_specs.append(pl.BlockSpec(memory_space=pl.ANY))
        args.append(xs)
    return pl.pallas_call(
        kern,
        grid=(n // tile,),
        in_specs=in_specs,
        out_specs=pl.BlockSpec(memory_space=pl.ANY),
        out_shape=jax.ShapeDtypeStruct((sorted_rows, d), BF16),
        scratch_shapes=[pltpu.SMEM((TABLE_ROWS, LANES), I32), pltpu.SMEM((SUBLANES, LANES), I32),
                        pltpu.VMEM((tile * SUBLANES, LANES), F32),
                        pltpu.VMEM((LOCAL_ROWS * SUBLANES, LANES), F32), pltpu.VMEM((LOCAL_ROWS, d), BF16),
                        pltpu.VMEM((BIG_CHUNK_ROWS, d), BF16), pltpu.SemaphoreType.DMA, pltpu.SemaphoreType.DMA],
        input_output_aliases={2: 0} if aliased else {},
        compiler_params=pltpu.CompilerParams(
            dimension_semantics=("arbitrary",), vmem_limit_bytes=VMEM_LIMIT),
        name="dispatch",
    )(*args)


def _combine_kernel(tab_hbm, gate_hbm, ys_hbm, h_ref, g_ref, b_ref, o_ref,
                    tabs, pending, gates, yl, ytok, ztok, sem_tab, sem_gate, sem_row, *maybe_zout, tile_base, alpha):
    t = h_ref.shape[0]
    i = pl.program_id(0)
    last = pl.num_programs(0) - 1
    slot = i % 2
    tab = tabs.at[slot]
    table_copy = lambda tile, s: pltpu.make_async_copy(tab_hbm.at[tile_base + tile], tabs.at[s], sem_tab.at[s])
    cp_gate = pltpu.make_async_copy(gate_hbm.at[:, pl.ds(pl.multiple_of(i * t, LANES), t)], gates, sem_gate)

    def chunk_copy(local_row, global_row, nrows):
        return pltpu.make_async_copy(ys_hbm.at[pl.ds(global_row, nrows), :],
                                     yl.at[pl.ds(local_row, nrows), :], sem_row)

    @pl.when(i == 0)
    def _():
        yl[...] = jnp.zeros_like(yl)
        table_copy(0, 0).start()
        table_copy(0, 0).wait()
        pending[0, 0], pending[0, 1] = _for_each_chunk(tabs.at[0], lambda *a: chunk_copy(*a[:3]).start(priority=a[3]))

    @pl.when(i < last)
    def _():
        table_copy(i + 1, 1 - slot).start()

    cp_gate.start()
    _wait_pieces(pending, chunk_copy)
    for c in range(SUBLANES):
        ytok[pl.ds(c, LOCAL_ROWS, stride=SUBLANES), :] = yl[:, c * LANES:(c + 1) * LANES].astype(F32)

    @pl.when(i < last)
    def _():
        table_copy(i + 1, 1 - slot).wait()
        pending[0, 0], pending[0, 1] = _for_each_chunk(tabs.at[1 - slot], lambda *a: chunk_copy(*a[:3]).start(priority=a[3]))

    cp_gate.wait()

    def gather(q, carry):
        for lane in range(LANES):
            n = q * LANES + lane
            _token_rows(ztok, n)[...] = (gates[0, n] * _tile_at(ytok, tab[q, lane])[...]
                                         + gates[1, n] * _tile_at(ytok, tab[POS_ROWS + q, lane])[...])
        return carry

    lax.fori_loop(0, t // LANES, gather, 0)
    f = jnp.concatenate([ztok[pl.ds(c, t, stride=SUBLANES), :] for c in range(SUBLANES)], axis=1)
    out = _layer_norm(alpha * h_ref[...] + f, g_ref[...], b_ref[...])
    if maybe_zout:
        _by_sequence(out, o_ref, *maybe_zout)
    else:
        o_ref[...] = out


def _combine(tabs, gate, ys, h2d, p, *, tile, tile_base, alpha, out_sequences=None):
    n, d = h2d.shape
    kern = functools.partial(_combine_kernel, tile_base=tile_base, alpha=alpha)
    scratch = [pltpu.SMEM((2, TABLE_ROWS, LANES), I32), pltpu.SMEM((SUBLANES, LANES), I32),
               pltpu.SMEM((SUBLANES, tile), F32),
               pltpu.VMEM((LOCAL_ROWS, d), BF16), pltpu.VMEM((LOCAL_ROWS * SUBLANES, LANES), F32),
               pltpu.VMEM((tile * SUBLANES, LANES), F32),
               pltpu.SemaphoreType.DMA((2,)), pltpu.SemaphoreType.DMA, pltpu.SemaphoreType.DMA]
    if out_sequences:
        npos = tile // out_sequences
        out_spec = pl.BlockSpec((out_sequences, npos, d), lambda i: (0, i, 0))
        out_shape = jax.ShapeDtypeStruct((out_sequences, n // out_sequences, d), F32)
        scratch.append(pltpu.VMEM((d // LANES, tile, LANES), F32))
    else:
        out_spec = pl.BlockSpec((tile, d), lambda i: (i, 0))
        out_shape = jax.ShapeDtypeStruct((n, d), F32)
    return pl.pallas_call(
        kern,
        grid=(n // tile,),
        in_specs=[
            pl.BlockSpec(memory_space=pl.ANY), pl.BlockSpec(memory_space=pl.ANY), pl.BlockSpec(memory_space=pl.ANY),
            pl.BlockSpec((tile, d), lambda i: (i, 0)),
            _const_spec((1, d)), _const_spec((1, d)),
        ],
        out_specs=out_spec,
        out_shape=out_shape,
        scratch_shapes=scratch,
        compiler_params=pltpu.CompilerParams(
            dimension_semantics=("arbitrary",), vmem_limit_bytes=VMEM_LIMIT),
        name="combine",
    )(tabs, gate, ys, h2d, p["ln2_g"], p["ln2_b"])


def _moe_kernel(blk_ref, x_ref, wg_ref, wu_ref, wd_ref, y_ref, wgb, wub, wdb):
    b = pl.program_id(0)

    @pl.when(jnp.logical_or(b == 0, blk_ref[0, b] != blk_ref[0, jnp.maximum(b - 1, 0)]))
    def _():
        wgb[...] = wg_ref[...].astype(BF16)
        wub[...] = wu_ref[...].astype(BF16)
        wdb[...] = wd_ref[...].astype(BF16)

    @pl.when(blk_ref[1, b] == 1)
    def _():
        x = x_ref[...]
        hg = _dot(x, wgb[...])
        hu = _dot(x, wub[...])
        hid = (hg * _sigmoid(hg) * hu).astype(BF16)
        y_ref[...] = _dot(hid, wdb[...]).astype(BF16)

    @pl.when(blk_ref[1, b] == 0)
    def _():
        y_ref[...] = jnp.zeros_like(y_ref)


def _moe(blk, xs, wg, wu, wd, layer):
    rows, d = xs.shape
    nblocks = rows // MOE_BLOCK
    f = wg.shape[-1]
    block = lambda b, blk: (jnp.where(blk[1, b] == 1, b, nblocks - 1), 0)
    return pl.pallas_call(
        _moe_kernel,
        grid_spec=pltpu.PrefetchScalarGridSpec(
            num_scalar_prefetch=1,
            grid=(nblocks,),
            in_specs=[
                pl.BlockSpec((MOE_BLOCK, d), block),
                pl.BlockSpec((None, None, d, f), lambda b, blk: (layer, blk[0, b], 0, 0)),
                pl.BlockSpec((None, None, d, f), lambda b, blk: (layer, blk[0, b], 0, 0)),
                pl.BlockSpec((None, None, f, d), lambda b, blk: (layer, blk[0, b], 0, 0)),
            ],
            out_specs=pl.BlockSpec((MOE_BLOCK, d), lambda b, blk: (b, 0)),
            scratch_shapes=[pltpu.VMEM((d, f), BF16), pltpu.VMEM((d, f), BF16), pltpu.VMEM((f, d), BF16)],
        ),
        out_shape=jax.ShapeDtypeStruct((rows, d), BF16),
        compiler_params=pltpu.CompilerParams(
            dimension_semantics=("arbitrary",), vmem_limit_bytes=VMEM_LIMIT),
        name="moe",
    )(blk, xs, wg, wu, wd)


def _sparse_moe(sets, p, *, layer, alpha, out_sequences=None):
    eids, bases, tiles = [], [], 0
    for h, e, _ in sets:
        n = h.shape[0]
        n_pad = -(-n // TOK_TILE) * TOK_TILE
        eids.append(jnp.concatenate([e, jnp.full((SUBLANES, n_pad - n), -1, I32)], axis=1))
        bases.append(tiles)
        tiles += n_pad // TOK_TILE
    n_tok = sum(h.shape[0] for h, _, _ in sets)
    run_pad = tiles * N_EXPERTS * (ROW_ALIGN - 1)
    nblocks = -(-(2 * n_tok + run_pad + N_EXPERTS * (MOE_BLOCK - 1)) // MOE_BLOCK)
    assert nblocks <= BLOCK_LANES
    sorted_rows = nblocks * MOE_BLOCK
    tabs, blk = _rank(jnp.concatenate(eids, axis=1), sorted_rows)
    xs = None
    for k, ((h, _, _), tb) in enumerate(zip(sets, bases)):
        xs = _dispatch(tabs, h, xs, tile=min(TOK_TILE, h.shape[0]), tile_base=tb, sorted_rows=sorted_rows,
                       fill_pads=k == len(sets) - 1)
    ys = _moe(blk, xs, p["w_gate"], p["w_up"], p["w_down"], layer)
    return [_combine(tabs, g, ys, h, p, tile=min(TOK_TILE, h.shape[0]), tile_base=tb, alpha=alpha,
                     out_sequences=out_sequences if k == 0 else None)
            for k, ((h, _, g), tb) in enumerate(zip(sets, bases))]


def _pad_rows(w, rows):
    return jnp.concatenate([w, jnp.zeros((rows - w.shape[0],) + w.shape[1:], w.dtype)], axis=0)


def _router_params(gw, gb, ew, eb):
    d = gw.shape[0]
    rw = jnp.concatenate([gw.T, jnp.zeros((8 - N_GROUPS, d), gw.dtype), ew.T], axis=0).astype(BF16)
    rb = jnp.concatenate([gb, jnp.zeros((8 - N_GROUPS,), gb.dtype), eb])[:, None].astype(F32)
    return rw, rb


def _to_blocks(x, seq_tile):
    nseq, npos, d = x.shape
    return x.reshape(nseq // seq_tile, seq_tile, npos, d).swapaxes(1, 2).reshape(nseq * npos, d)


def _from_blocks(x2d, nseq, seq_tile):
    n, d = x2d.shape
    npos = n // nseq
    return x2d.reshape(nseq // seq_tile, npos, seq_tile, d).swapaxes(1, 2).reshape(nseq, npos, d)


def kernel(x_prompt, x_sample, state_conv_a, state_conv_b, meta_tokens, sc_w_in, sc_conv_w, sc_w_out, cf_w_pw1, cf_b_pw1, cf_conv_w, cf_conv_b, cf_ln_g, cf_ln_b, cf_w_pw2, cf_b_pw2, ln1_g, ln1_b, ln2_g, ln2_b, rt_group_w, rt_group_b, rt_expert_w, rt_expert_b, moe_w_gate, moe_w_up, moe_w_down):
    bsz, seq, d = x_prompt.shape
    dec_b, dec_t, _ = x_sample.shape
    n_meta = meta_tokens.shape[0]
    depth = ln1_g.shape[0]
    alpha = float((2 * depth) ** 0.25)
    row = lambda v: v[None, :].astype(F32)
    s_seq = 32
    p_pos = 512 // bsz

    layer = []
    for i in range(depth):
        rw, rb = _router_params(rt_group_w[i], rt_group_b[i], rt_expert_w[i], rt_expert_b[i])
        layer.append(dict(
            ln1_g=row(ln1_g[i]), ln1_b=row(ln1_b[i]), ln2_g=row(ln2_g[i]), ln2_b=row(ln2_b[i]), rw=rw, rb=rb,
            w_gate=moe_w_gate, w_up=moe_w_up, w_down=moe_w_down))
    pa = dict(layer[0], w_in=sc_w_in[0].astype(BF16), conv_w=_pad_rows(sc_conv_w[0], 8),
              w_out=sc_w_out[0].astype(BF16))
    pb = dict(layer[1], w_pw1=cf_w_pw1[0].astype(BF16), b_pw1=row(cf_b_pw1[0]),
              conv_w=jnp.repeat(cf_conv_w[0].astype(F32), SUBLANES, axis=0), conv_b=row(cf_conv_b[0]),
              cln_g=row(cf_ln_g[0]), cln_b=row(cf_ln_b[0]), w_pw2=cf_w_pw2[0].astype(BF16), b_pw2=row(cf_b_pw2[0]))

    xs = _to_blocks(x_sample, s_seq)
    xm = jnp.broadcast_to(meta_tokens.astype(F32)[:, None, :], (n_meta, bsz, d)).reshape(n_meta * bsz, d)
    mix_p = dict(pos_tile=p_pos, seq_tile=bsz, nseq=bsz, alpha=alpha)
    mix_s = dict(pos_tile=dec_t, seq_tile=s_seq, nseq=dec_b, alpha=alpha)
    mix_m = dict(pos_tile=n_meta, seq_tile=bsz, nseq=bsz, alpha=alpha)

    zero_a = jnp.zeros(((SC_WIDTH - 1) * bsz, d), F32)
    hm, em, gm, tail_am = _mixer("a", xm, zero_a, pa, **mix_m)
    hp, ep, gp, tail_ap = _mixer("a", x_prompt, tail_am, pa, by_sequence=True, **mix_p)
    hs, es, gs, tail_as = _mixer("a", xs, _to_blocks(state_conv_a[0], s_seq), pa, **mix_s)
    hp, hs, hm = _sparse_moe([(hp, ep, gp), (hs, es, gs), (hm, em, gm)], pa, layer=0, alpha=alpha)

    zero_b = jnp.zeros(((CF_WIDTH - 1) * bsz, d), F32)
    _, _, _, tail_bm = _mixer("b", hm, zero_b, pb, **mix_m)
    hp, ep, gp, tail_bp = _mixer("b", hp, tail_bm, pb, **mix_p)
    hs, es, gs, tail_bs = _mixer("b", hs, _to_blocks(state_conv_b[0], s_seq), pb, **mix_s)
    y_prompt, hs = _sparse_moe([(hp, ep, gp), (hs, es, gs)], pb, layer=1, alpha=alpha, out_sequences=bsz)

    return (y_prompt, _from_blocks(hs, dec_b, s_seq),
            _from_blocks(tail_ap, bsz, bsz)[None], _from_blocks(tail_bp, bsz, bsz)[None],
            _from_blocks(tail_as, dec_b, s_seq)[None], _from_blocks(tail_bs, dec_b, s_seq)[None])
```
